```python
import math
import jax, jax.numpy as jnp
from jax import lax
import numpy as np

D_MODEL = 4096
BATCH = 1
SEQ = 8192
DEPTH = 2

GRID_W = 64
CTX_LEN = 256
D_MIX = D_MODEL
MLSTM_HEADS = 4
MLSTM_DV = D_MIX // 2 // MLSTM_HEADS
MLSTM_DQK = MLSTM_DV // 2
MLSTM_W = MLSTM_HEADS * MLSTM_DV
SSM_W = D_MIX - MLSTM_W
SSM_HEAD_DIM = 64
SSM_HEADS = SSM_W // SSM_HEAD_DIM
SSM_GROUPS = 8
SSM_STATE = 128
SSM_CONV = 5
SSM_CONV_CH = SSM_W + 2 * SSM_GROUPS * SSM_STATE
CHUNK = 64
IN_SPLITS = (MLSTM_HEADS * MLSTM_DQK, MLSTM_HEADS * MLSTM_DQK, MLSTM_W, MLSTM_W, 4 * MLSTM_HEADS,
             SSM_W, SSM_CONV_CH, 2 * SSM_HEADS)
D_IN = 2 * MLSTM_HEADS * MLSTM_DQK + 2 * MLSTM_W + 4 * MLSTM_HEADS + SSM_W + SSM_CONV_CH + 2 * SSM_HEADS
D_FF = 11008
N_EXPERTS = 8
TOP_K = 2
D_FF_EXPERT = 4096
MOE_BLOCK = 128
N_DENSE = (DEPTH + 1) // 2
N_MOE = DEPTH // 2
EPS = 1e-6

kernel_name = "hybrid_mlstm_ssd_prefix_dit_block"


def rmsnorm(x, g):
    xf = x.astype(jnp.float32)
    return xf * lax.rsqrt(jnp.mean(xf * xf, axis=-1, keepdims=True) + EPS) * g


def to_colmajor(a):
    b, l, ch = a.shape
    rows = l // GRID_W
    return a.reshape(b, rows, GRID_W, ch).transpose(0, 2, 1, 3).reshape(b, l, ch)


def from_colmajor(a):
    b, l, ch = a.shape
    rows = l // GRID_W
    return a.reshape(b, GRID_W, rows, ch).transpose(0, 2, 1, 3).reshape(b, l, ch)


def split_proj(p):
    idx = np.cumsum(IN_SPLITS)[:-1].tolist()
    return jnp.split(p, idx, axis=-1)


def dwconv_silu(u, w, bias):
    out = lax.conv_general_dilated(u, w[:, None, :].astype(u.dtype), window_strides=(1,),
                                   padding=[(SSM_CONV // 2, SSM_CONV // 2)],
                                   dimension_numbers=('NWC', 'WIO', 'NWC'),
                                   feature_group_count=u.shape[-1])
    return jax.nn.silu(out + bias)


def mlstm_chunked(q, k, v, i_pre, logf, state0):
    b, nh, t, dk = q.shape
    dv = v.shape[-1]
    nc = t // CHUNK
    q = q.reshape(b, nh, nc, CHUNK, dk)
    k = k.reshape(b, nh, nc, CHUNK, dk)
    v = v.reshape(b, nh, nc, CHUNK, dv)
    i_pre = i_pre.reshape(b, nh, nc, CHUNK)
    bcum = jnp.cumsum(logf.reshape(b, nh, nc, CHUNK), axis=-1)
    causal = jnp.tril(jnp.ones((CHUNK, CHUNK), bool))
    logw = jnp.where(causal, bcum[..., :, None] - bcum[..., None, :] + i_pre[..., None, :], -jnp.inf)
    m_intra = jnp.max(logw, axis=-1)
    s = jnp.einsum('bhctd,bhcsd->bhcts', q, k) * jnp.exp(logw - m_intra[..., None])
    num_intra = jnp.einsum('bhcts,bhcsv->bhctv', s, v)
    den_intra = jnp.sum(s, axis=-1)
    g_end = bcum[..., -1:] - bcum + i_pre

    def step(carry, inp):
        c_st, n_st, m_st = carry
        qc, kc, vc, bc, gc, mic, nic, dic = inp
        inter_log = bc + m_st[..., None]
        m_t = jnp.maximum(inter_log, mic)
        a_inter = jnp.exp(inter_log - m_t)
        a_intra = jnp.exp(mic - m_t)
        num = a_inter[..., None] * jnp.einsum('bhtd,bhdv->bhtv', qc, c_st) + a_intra[..., None] * nic
        den = a_inter * jnp.einsum('bhtd,bhd->bht', qc, n_st) + a_intra * dic
        h = num / jnp.maximum(jnp.abs(den), jnp.exp(-m_t))[..., None]
        b_last = bc[..., -1]
        m_new = jnp.maximum(b_last + m_st, jnp.max(gc, axis=-1))
        wg = jnp.exp(gc - m_new[..., None])
        dec = jnp.exp(b_last + m_st - m_new)
        c_new = dec[..., None, None] * c_st + jnp.einsum('bhs,bhsd,bhsv->bhdv', wg, kc, vc)
        n_new = dec[..., None] * n_st + jnp.einsum('bhs,bhsd->bhd', wg, kc)
        return (c_new, n_new, m_new), h

    xs = tuple(jnp.moveaxis(a, 2, 0) for a in (q, k, v, bcum, g_end, m_intra, num_intra, den_intra))
    state, h = lax.scan(step, state0, xs)
    return jnp.moveaxis(h, 0, 2).reshape(b, nh, t, dv), state


def ssd_chunked(xs, dt, bm, cm, a, state0):
    b, t, h, p = xs.shape
    g, n = bm.shape[2], bm.shape[3]
    hg = h // g
    nc = t // CHUNK
    xs = xs.reshape(b, nc, CHUNK, g, hg, p)
    dt = dt.reshape(b, nc, CHUNK, g, hg)
    bm = bm.reshape(b, nc, CHUNK, g, n)
    cm = cm.reshape(b, nc, CHUNK, g, n)
    acum = jnp.cumsum(dt * a.reshape(g, hg), axis=2)
    causal = jnp.tril(jnp.ones((CHUNK, CHUNK), bool))[:, :, None, None]
    seg = acum[:, :, :, None] - acum[:, :, None, :]
    decay = jnp.exp(jnp.where(causal, seg, -jnp.inf))
    cb = jnp.einsum('bclgn,bcsgn->bclsg', cm, bm)
    mix = cb[..., None] * decay * dt[:, :, None]
    y_diag = jnp.einsum('bclsgh,bcsghp->bclghp', mix, xs)
    w_end = jnp.exp(acum[:, :, -1:] - acum) * dt
    chunk_decay = jnp.exp(acum[:, :, -1])

    def step(state, inp):
        xc, bc, cc, ac, wc, dc = inp
        y_off = jnp.einsum('blgn,bghnp->blghp', cc, state) * jnp.exp(ac)[..., None]
        state = dc[..., None, None] * state + jnp.einsum('blgn,blgh,blghp->bghnp', bc, wc, xc)
        return state, y_off

    scan_in = tuple(jnp.moveaxis(u, 1, 0) for u in (xs, bm, cm, acum, w_end, chunk_decay))
    state, y_off = lax.scan(step, state0, scan_in)
    y = y_diag + jnp.moveaxis(y_off, 0, 1)
    return y.reshape(b, t, h, p), state


def mlstm_heads(q, k, v, gates):
    b, t, _ = q.shape
    q = q.astype(jnp.float32).reshape(b, t, MLSTM_HEADS, MLSTM_DQK).transpose(0, 2, 1, 3)
    k = k.astype(jnp.float32).reshape(b, t, MLSTM_HEADS, MLSTM_DQK).transpose(0, 2, 1, 3) * (MLSTM_DQK ** -0.5)
    v = v.astype(jnp.float32).reshape(b, t, MLSTM_HEADS, MLSTM_DV).transpose(0, 2, 1, 3)
    g = gates.astype(jnp.float32).reshape(b, t, 4, MLSTM_HEADS).transpose(2, 0, 3, 1)
    fwd = (q, k, v, g[0], jax.nn.log_sigmoid(g[2]))
    bwd = (q, k, v, g[1], jax.nn.log_sigmoid(g[3]))
    return fwd, bwd


def mlstm_bidirectional(ctx_f, ctx_b, lat_f, lat_b):
    b = ctx_f[0].shape[0]
    zero = (jnp.zeros((b, MLSTM_HEADS, MLSTM_DQK, MLSTM_DV), jnp.float32),
            jnp.zeros((b, MLSTM_HEADS, MLSTM_DQK), jnp.float32),
            jnp.zeros((b, MLSTM_HEADS), jnp.float32))
    rev = lambda ins: tuple(jnp.flip(u, axis=2) for u in ins)
    hc_f, st = mlstm_chunked(*ctx_f, zero)
    hl_f, _ = mlstm_chunked(*lat_f, st)
    hc_b, st = mlstm_chunked(*rev(ctx_b), zero)
    hl_b, _ = mlstm_chunked(*rev(lat_b), st)
    return hc_f + jnp.flip(hc_b, axis=2), hl_f + jnp.flip(hl_b, axis=2)


def ssd_heads(xbc, dt_raw, dt_bias):
    b, t, _ = xbc.shape
    gn = SSM_GROUPS * SSM_STATE
    xbc = xbc.astype(jnp.float32)
    xs = xbc[..., :SSM_W].reshape(b, t, SSM_HEADS, SSM_HEAD_DIM)
    bm = xbc[..., SSM_W:SSM_W + gn].reshape(b, t, SSM_GROUPS, SSM_STATE)
    cm = xbc[..., SSM_W + gn:].reshape(b, t, SSM_GROUPS, SSM_STATE)
    dt = jax.nn.softplus(dt_raw.astype(jnp.float32).reshape(b, t, 2, SSM_HEADS) + dt_bias)
    return (xs, dt[:, :, 0], bm, cm), (xs, dt[:, :, 1], bm, cm)


def ssd_bidirectional(ctx_f, ctx_b, lat_f, lat_b, a):
    b = ctx_f[0].shape[0]
    zero = jnp.zeros((b, SSM_GROUPS, SSM_HEADS // SSM_GROUPS, SSM_STATE, SSM_HEAD_DIM), jnp.float32)
    rev = lambda ins: tuple(jnp.flip(u, axis=1) for u in ins)
    yc_f, st = ssd_chunked(*ctx_f, a[0], zero)
    yl_f, _ = ssd_chunked(*lat_f, a[0], st)
    yc_b, st = ssd_chunked(*rev(ctx_b), a[1], zero)
    yl_b, _ = ssd_chunked(*rev(lat_b), a[1], st)
    return yc_f + jnp.flip(yc_b, axis=1), yl_f + jnp.flip(yl_b, axis=1)


def hybrid_mixer(nc, nl, w_in, gate_b, conv_w, conv_b, a_log, dt_bias, d_skip, mlstm_g, ssm_g, need_ctx):
    qc, kc, vc, oc, gc, zc, xbcc, dtc = split_proj(nc @ w_in)
    ql, kl, vl, ol, gl, zl, xbcl, dtl = split_proj(nl @ w_in)
    mc_f, mc_b = mlstm_heads(qc, kc, vc, gc + gate_b)
    ml_f, ml_b = mlstm_heads(ql, kl, vl, gl + gate_b)
    hc, hl = mlstm_bidirectional(mc_f, mc_b, ml_f, ml_b)
    sc_f, sc_b = ssd_heads(dwconv_silu(xbcc, conv_w, conv_b), dtc, dt_bias)
    sl_f, sl_b = ssd_heads(dwconv_silu(to_colmajor(xbcl), conv_w, conv_b), to_colmajor(dtl), dt_bias)
    a = -jnp.exp(a_log.astype(jnp.float32))
    yc, yl = ssd_bidirectional(sc_f, sc_b, sl_f, sl_b, a)

    def finish(h, o, y, xs, z, reorder):
        b, t = z.shape[0], z.shape[1]
        m_out = rmsnorm(h.transpose(0, 2, 1, 3), mlstm_g.reshape(MLSTM_HEADS, MLSTM_DV))
        m_out = m_out.reshape(b, t, MLSTM_W) * jax.nn.sigmoid(o.astype(jnp.float32))
        y = reorder((y + d_skip[:, None] * xs).reshape(b, t, SSM_W))
        s_out = rmsnorm(y * jax.nn.silu(z.astype(jnp.float32)), ssm_g)
        return jnp.concatenate([m_out, s_out], axis=-1)

    out_l = finish(hl, ol, yl, sl_f[0], zl, from_colmajor)
    out_c = finish(hc, oc, yc, sc_f[0], zc, lambda u: u) if need_ctx else None
    return out_c, out_l


def swiglu(h, w1, w3, w2):
    return (jax.nn.silu(h @ w1) * (h @ w3)) @ w2


def moe_swiglu(h, router_w, w1, w3, w2):
    b, t, dm = h.shape
    xt = h.reshape(-1, dm)
    ntok = xt.shape[0]
    logits = (xt @ router_w).astype(jnp.float32)
    top_val, top_idx = lax.top_k(logits, TOP_K)
    gates = jax.nn.softmax(top_val, axis=-1)
    n_assign = ntok * TOP_K
    e_flat = top_idx.reshape(-1)
    t_flat = jnp.repeat(jnp.arange(ntok, dtype=jnp.int32), TOP_K)
    g_flat = gates.reshape(-1)
    order = jnp.argsort(e_flat)
    e_sorted = e_flat[order]
    counts = jnp.bincount(e_flat, length=N_EXPERTS)
    padded = (counts + MOE_BLOCK - 1) // MOE_BLOCK * MOE_BLOCK
    start = jnp.cumsum(counts) - counts
    pend = jnp.cumsum(padded)
    pstart = pend - padded
    dest = pstart[e_sorted] + (jnp.arange(n_assign) - start[e_sorted])
    n_blocks = (n_assign + N_EXPERTS * (MOE_BLOCK - 1) + MOE_BLOCK - 1) // MOE_BLOCK
    n_rows = n_blocks * MOE_BLOCK
    row_tok = jnp.full((n_rows,), ntok, jnp.int32).at[dest].set(t_flat[order])
    row_gate = jnp.zeros((n_rows,), jnp.float32).at[dest].set(g_flat[order])
    block_expert = jnp.minimum(jnp.searchsorted(pend, jnp.arange(n_blocks) * MOE_BLOCK, side='right'), N_EXPERTS - 1)
    xpad = jnp.concatenate([xt, jnp.zeros((1, dm), xt.dtype)], axis=0)
    xr = xpad[row_tok].reshape(n_blocks, MOE_BLOCK, dm)

    def expert_block(args):
        xb, e = args
        return swiglu(xb, w1[e], w3[e], w2[e])

    yr = lax.map(expert_block, (xr, block_expert)).reshape(n_rows, dm)
    out = jnp.zeros((ntok + 1, dm), yr.dtype).at[row_tok].add(yr * row_gate[:, None].astype(yr.dtype))
    return out[:ntok].reshape(b, t, dm)


def channel_mix(h, layer, ffn_w1, ffn_w3, ffn_w2, router_w, moe_w1, moe_w3, moe_w2):
    i = layer // 2
    if layer % 2 == 0:
        return swiglu(h, ffn_w1[i], ffn_w3[i], ffn_w2[i])
    return moe_swiglu(h, router_w[i], moe_w1[i], moe_w3[i], moe_w2[i])


def setup_inputs(seed: int = 0) -> dict:
    key = jax.random.key(seed)
    ks = jax.random.split(key, 32)
    f32 = jnp.float32
    nrm = lambda k, shape, scale: jax.random.normal(k, shape, f32) * scale
    x = nrm(ks[0], (BATCH, SEQ, D_MODEL), 1.0)
    c = nrm(ks[1], (BATCH, D_MODEL), 1.0)
    ctx = nrm(ks[2], (BATCH, CTX_LEN, D_MODEL), 1.0)
    c_ctx = nrm(ks[3], (D_MODEL,), 1.0)
    w_mod = nrm(ks[4], (DEPTH, D_MODEL, 6 * D_MODEL), 0.5 * D_MODEL ** -0.5)
    b_mod = nrm(ks[5], (DEPTH, 6 * D_MODEL), 0.02)
    norm1_g = 1.0 + nrm(ks[6], (DEPTH, D_MODEL), 0.02)
    norm2_g = 1.0 + nrm(ks[7], (DEPTH, D_MODEL), 0.02)
    w_in = nrm(ks[8], (DEPTH, D_MODEL, D_IN), D_MODEL ** -0.5)
    i_bias = nrm(ks[9], (DEPTH, 2 * MLSTM_HEADS), 0.1)
    f_bias = jnp.tile(jnp.linspace(3.0, 6.0, MLSTM_HEADS, dtype=f32), 2)[None] + nrm(ks[10], (DEPTH, 2 * MLSTM_HEADS), 0.1)
    gate_b = jnp.concatenate([i_bias, f_bias], axis=-1)
    conv_w = nrm(ks[11], (DEPTH, SSM_CONV, SSM_CONV_CH), SSM_CONV ** -0.5)
    conv_b = nrm(ks[12], (DEPTH, SSM_CONV_CH), 0.02)
    a_log = jnp.log(jax.random.uniform(ks[13], (DEPTH, 2, SSM_HEADS), f32, 1.0, 16.0))
    dt0 = jnp.exp(jax.random.uniform(ks[14], (DEPTH, 2, SSM_HEADS), f32, math.log(1e-3), math.log(1e-1)))
    dt_bias = dt0 + jnp.log(-jnp.expm1(-dt0))
    d_skip = 1.0 + nrm(ks[15], (DEPTH, SSM_HEADS), 0.1)
    mlstm_g = 1.0 + nrm(ks[16], (DEPTH, MLSTM_W), 0.02)
    ssm_g = 1.0 + nrm(ks[17], (DEPTH, SSM_W), 0.02)
    w_out = nrm(ks[18], (DEPTH, D_MIX, D_MODEL), D_MIX ** -0.5)
    ffn_w1 = nrm(ks[19], (N_DENSE, D_MODEL, D_FF), D_MODEL ** -0.5)
    ffn_w3 = nrm(ks[20], (N_DENSE, D_MODEL, D_FF), D_MODEL ** -0.5)
    ffn_w2 = nrm(ks[21], (N_DENSE, D_FF, D_MODEL), D_FF ** -0.5)
    router_w = nrm(ks[22], (N_MOE, D_MODEL, N_EXPERTS), D_MODEL ** -0.5)
    moe_w1 = nrm(ks[23], (N_MOE, N_EXPERTS, D_MODEL, D_FF_EXPERT), D_MODEL ** -0.5)
    moe_w3 = nrm(ks[24], (N_MOE, N_EXPERTS, D_MODEL, D_FF_EXPERT), D_MODEL ** -0.5)
    moe_w2 = nrm(ks[25], (N_MOE, N_EXPERTS, D_FF_EXPERT, D_MODEL), D_FF_EXPERT ** -0.5)
    final_g = 1.0 + nrm(ks[26], (D_MODEL,), 0.02)
    return {"x": x, "c": c, "ctx": ctx, "c_ctx": c_ctx, "w_mod": w_mod, "b_mod": b_mod,
            "norm1_g": norm1_g, "norm2_g": norm2_g, "w_in": w_in, "gate_b": gate_b,
            "conv_w": conv_w, "conv_b": conv_b, "a_log": a_log, "dt_bias": dt_bias, "d_skip": d_skip,
            "mlstm_g": mlstm_g, "ssm_g": ssm_g, "w_out": w_out, "ffn_w1": ffn_w1, "ffn_w3": ffn_w3,
            "ffn_w2": ffn_w2, "router_w": router_w, "moe_w1": moe_w1, "moe_w3": moe_w3,
            "moe_w2": moe_w2, "final_g": final_g}


def reference(x, c, ctx, c_ctx, w_mod, b_mod, norm1_g, norm2_g, w_in, gate_b, conv_w, conv_b, a_log,
              dt_bias, d_skip, mlstm_g, ssm_g, w_out, ffn_w1, ffn_w3, ffn_w2, router_w, moe_w1, moe_w3,
              moe_w2, final_g):
    s_lat = jax.nn.silu(c.astype(jnp.float32))
    s_ctx = jax.nn.silu(c_ctx.astype(jnp.float32))
    for layer in range(DEPTH):
        need_ctx = layer < DEPTH - 1
        mod_l = jnp.split((s_lat @ w_mod[layer] + b_mod[layer])[:, None, :], 6, axis=-1)
        mod_c = jnp.split(s_ctx @ w_mod[layer] + b_mod[layer], 6, axis=-1)
        nl = rmsnorm(x, norm1_g[layer]) * (1.0 + mod_l[1]) + mod_l[0]
        nc = rmsnorm(ctx, norm1_g[layer]) * (1.0 + mod_c[1]) + mod_c[0]
        out_c, out_l = hybrid_mixer(nc, nl, w_in[layer], gate_b[layer], conv_w[layer], conv_b[layer],
                                    a_log[layer], dt_bias[layer], d_skip[layer], mlstm_g[layer],
                                    ssm_g[layer], need_ctx)
        x = x + mod_l[2] * (out_l @ w_out[layer])
        h_l = rmsnorm(x, norm2_g[layer]) * (1.0 + mod_l[4]) + mod_l[3]
        x = x + mod_l[5] * channel_mix(h_l, layer, ffn_w1, ffn_w3, ffn_w2, router_w, moe_w1, moe_w3, moe_w2)
        if need_ctx:
            ctx = ctx + mod_c[2] * (out_c @ w_out[layer])
            h_c = rmsnorm(ctx, norm2_g[layer]) * (1.0 + mod_c[4]) + mod_c[3]
            ctx = ctx + mod_c[5] * channel_mix(h_c, layer, ffn_w1, ffn_w3, ffn_w2, router_w, moe_w1, moe_w3, moe_w2)
    return rmsnorm(x, final_g)
```

```python
import functools
import math

import jax
import jax.numpy as jnp
from jax import lax
from jax.experimental import pallas as pl
from jax.experimental.pallas import tpu as pltpu

GRID_W = 64
SSM_GROUPS = 8
TOP_K = 2
EPS = 1e-6

V7X_VMEM_LIMIT_BYTES = 56 * 1024 * 1024
LANES = 128
SCAN_CHUNK = 256
ROW_TILE = 256
MM_TM = 768
MM_TN = 1024
GLU_TN = 512
FFN_PAD = 1024
MOE_TM = 512
GATHER_ROWS = 256
COMBINE_ROWS = 128

F32 = jnp.float32
BF16 = jnp.bfloat16
HIGHEST = lax.Precision.HIGHEST


def _pick(dim, pref, mult):
    if dim <= pref:
        return dim
    t = (pref // mult) * mult
    while t >= mult:
        if dim % t == 0:
            return t
        t -= mult
    return dim


def _cparams(*sem):
    return pltpu.CompilerParams(dimension_semantics=sem, vmem_limit_bytes=V7X_VMEM_LIMIT_BYTES)


def _sigmoid(v):
    return 1.0 / (1.0 + jnp.exp(-v))


def _silu(v):
    return v * _sigmoid(v)


def _softplus(v):
    return jnp.maximum(v, 0.0) + jnp.log1p(jnp.exp(-jnp.abs(v)))


def _log_sigmoid(v):
    return -_softplus(-v)


def _mod_kernel(c_ref, w_ref, b_ref, o_ref):
    s = _silu(c_ref[...])
    o_ref[...] = jnp.dot(s, w_ref[...], preferred_element_type=F32, precision=HIGHEST) + b_ref[...]


def _modulation(cvec, w_mod, b_mod):
    depth, d, n = w_mod.shape
    tn = _pick(n, 1024, LANES)
    return pl.pallas_call(
        _mod_kernel,
        grid=(depth, n // tn),
        in_specs=[pl.BlockSpec((8, d), lambda l, j: (0, 0)),
                  pl.BlockSpec((None, d, tn), lambda l, j: (l, 0, j)),
                  pl.BlockSpec((None, 1, tn), lambda l, j: (l, 0, j))],
        out_specs=pl.BlockSpec((None, 8, tn), lambda l, j: (l, 0, j)),
        out_shape=jax.ShapeDtypeStruct((depth, 8, n), F32),
        compiler_params=_cparams("parallel", "parallel"),
        name="adaln_modulation",
    )(cvec, w_mod, b_mod.reshape(depth, 1, n))


def _norm_kernel(x_ref, g_ref, sc_ref, sh_ref, o_ref):
    x = x_ref[...]
    r = lax.rsqrt(jnp.mean(x * x, axis=-1, keepdims=True) + EPS)
    o_ref[...] = (x * r * g_ref[...] * (1.0 + sc_ref[...]) + sh_ref[...]).astype(o_ref.dtype)


def _norm_router_kernel(x_ref, g_ref, sc_ref, sh_ref, rw_ref, o_ref, lg_ref):
    x = x_ref[...]
    r = lax.rsqrt(jnp.mean(x * x, axis=-1, keepdims=True) + EPS)
    h = x * r * g_ref[...] * (1.0 + sc_ref[...]) + sh_ref[...]
    o_ref[...] = h.astype(o_ref.dtype)
    lg_ref[...] = jnp.dot(h, rw_ref[...], preferred_element_type=F32, precision=HIGHEST)


def _norm_mod(xs, g, scale, shift, n_ctx, router_w=None):
    s, d = xs.shape
    r = _pick(math.gcd(n_ctx, s - n_ctx), ROW_TILE, 8)
    nct = n_ctx // r
    row_spec = pl.BlockSpec((r, d), lambda i: (i, 0))
    mod_spec = pl.BlockSpec((None, 1, d), lambda i: (jnp.where(i < nct, 0, 1), 0, 0))
    in_specs = [row_spec, pl.BlockSpec((1, d), lambda i: (0, 0)), mod_spec, mod_spec]
    if router_w is None:
        return pl.pallas_call(
            _norm_kernel, grid=(s // r,), in_specs=in_specs, out_specs=row_spec,
            out_shape=jax.ShapeDtypeStruct((s, d), BF16),
            compiler_params=_cparams("parallel"), name="rmsnorm_modulate",
        )(xs, g.reshape(1, d), scale, shift)
    ne = router_w.shape[1]
    rw = jnp.zeros((d, LANES), F32).at[:, :ne].set(router_w)
    return pl.pallas_call(
        _norm_router_kernel, grid=(s // r,),
        in_specs=in_specs + [pl.BlockSpec((d, LANES), lambda i: (0, 0))],
        out_specs=[row_spec, pl.BlockSpec((r, LANES), lambda i: (i, 0))],
        out_shape=[jax.ShapeDtypeStruct((s, d), BF16), jax.ShapeDtypeStruct((s, LANES), F32)],
        compiler_params=_cparams("parallel"), name="rmsnorm_modulate_router",
    )(xs, g.reshape(1, d), scale, shift, rw)


def _mm_kernel(a_ref, w_ref, o_ref):
    o_ref[...] = jnp.dot(a_ref[...], w_ref[...], preferred_element_type=F32).astype(o_ref.dtype)


def _matmul(a, w, out_dtype):
    m, k = a.shape
    n = w.shape[1]
    tm = _pick(m, MM_TM, 8)
    tn = _pick(n, MM_TN, LANES)
    return pl.pallas_call(
        _mm_kernel, grid=(m // tm, n // tn),
        in_specs=[pl.BlockSpec((tm, k), lambda i, j: (i, 0)), pl.BlockSpec((k, tn), lambda i, j: (0, j))],
        out_specs=pl.BlockSpec((tm, tn), lambda i, j: (i, j)),
        out_shape=jax.ShapeDtypeStruct((m, n), out_dtype),
        compiler_params=_cparams("parallel", "parallel"), name="matmul",
    )(a, w)


def _mm_res_kernel(a_ref, w_ref, res_ref, gate_ref, o_ref, acc_ref, *, nk, tm, n_ctx):
    k = pl.program_id(2)

    @pl.when(k == 0)
    def _():
        acc_ref[...] = jnp.zeros_like(acc_ref)

    acc_ref[...] += jnp.dot(a_ref[...], w_ref[...], preferred_element_type=F32)

    @pl.when(k == nk - 1)
    def _():
        rows = pl.program_id(0) * tm + lax.broadcasted_iota(jnp.int32, (tm, 1), 0)
        gate = jnp.where(rows < n_ctx, gate_ref[0], gate_ref[1])
        o_ref[...] = res_ref[...] + gate * acc_ref[...]


def _matmul_residual(a, w, res, gate, n_ctx, tk_pref):
    m, kdim = a.shape
    n = w.shape[1]
    tm = _pick(m, MM_TM, 8)
    tn = _pick(n, MM_TN, LANES)
    tk = _pick(kdim, tk_pref, LANES)
    nk = kdim // tk
    return pl.pallas_call(
        functools.partial(_mm_res_kernel, nk=nk, tm=tm, n_ctx=n_ctx),
        grid=(m // tm, n // tn, nk),
        in_specs=[pl.BlockSpec((tm, tk), lambda i, j, k: (i, k)),
                  pl.BlockSpec((tk, tn), lambda i, j, k: (k, j)),
                  pl.BlockSpec((tm, tn), lambda i, j, k: (i, j)),
                  pl.BlockSpec((2, 1, tn), lambda i, j, k: (0, 0, j))],
        out_specs=pl.BlockSpec((tm, tn), lambda i, j, k: (i, j)),
        out_shape=jax.ShapeDtypeStruct((m, n), F32),
        scratch_shapes=[pltpu.VMEM((tm, tn), F32)],
        compiler_params=_cparams("parallel", "parallel", "arbitrary"), name="matmul_residual",
    )(a, w, res, gate)


def _tile_maps(nj):
    def row(i, j, te, tv, tl):
        return jnp.where(tv[i] > 0, i, tl[0])

    def col(i, j, te, tv, tl):
        return jnp.where(tv[i] > 0, j, nj - 1)

    return row, col


def _glu_kernel(te_ref, tv_ref, tl_ref, a_ref, w1_ref, w3_ref, o_ref):
    used = tv_ref[pl.program_id(0)] > 0

    @pl.when(used)
    def _():
        a = a_ref[...]
        h1 = jnp.dot(a, w1_ref[...], preferred_element_type=F32)
        h3 = jnp.dot(a, w3_ref[...], preferred_element_type=F32)
        o_ref[...] = (_silu(h1) * h3).astype(o_ref.dtype)

    @pl.when(jnp.logical_not(used))
    def _():
        o_ref[...] = jnp.zeros_like(o_ref)


def _glu(a, w1, w3, tm, tile_expert, tile_valid, tile_last):
    m, k = a.shape
    n = w1.shape[2]
    tn = _pick(n, GLU_TN, LANES)
    nj = n // tn
    row, col = _tile_maps(nj)
    wspec = pl.BlockSpec((None, k, tn), lambda i, j, te, tv, tl: (te[row(i, j, te, tv, tl)], 0, col(i, j, te, tv, tl)))
    return pl.pallas_call(
        _glu_kernel,
        grid_spec=pltpu.PrefetchScalarGridSpec(
            num_scalar_prefetch=3, grid=(m // tm, nj),
            in_specs=[pl.BlockSpec((tm, k), lambda i, j, te, tv, tl: (row(i, j, te, tv, tl), 0)), wspec, wspec],
            out_specs=pl.BlockSpec((tm, tn), lambda i, j, te, tv, tl: (i, j))),
        out_shape=jax.ShapeDtypeStruct((m, n), BF16),
        compiler_params=_cparams("arbitrary", "arbitrary"), name="swiglu_up",
    )(tile_expert, tile_valid, tile_last, a, w1, w3)


def _mm_expert_kernel(te_ref, tv_ref, tl_ref, a_ref, w_ref, o_ref):
    used = tv_ref[pl.program_id(0)] > 0

    @pl.when(used)
    def _():
        o_ref[...] = jnp.dot(a_ref[...], w_ref[...], preferred_element_type=F32).astype(o_ref.dtype)

    @pl.when(jnp.logical_not(used))
    def _():
        o_ref[...] = jnp.zeros_like(o_ref)


def _matmul_expert(a, w, tm, tile_expert, tile_valid, tile_last, out_dtype):
    m, k = a.shape
    n = w.shape[2]
    tn = _pick(n, MM_TN, LANES)
    nj = n // tn
    row, col = _tile_maps(nj)
    return pl.pallas_call(
        _mm_expert_kernel,
        grid_spec=pltpu.PrefetchScalarGridSpec(
            num_scalar_prefetch=3, grid=(m // tm, nj),
            in_specs=[pl.BlockSpec((tm, k), lambda i, j, te, tv, tl: (row(i, j, te, tv, tl), 0)),
                      pl.BlockSpec((None, k, tn),
                                   lambda i, j, te, tv, tl: (te[row(i, j, te, tv, tl)], 0, col(i, j, te, tv, tl)))],
            out_specs=pl.BlockSpec((tm, tn), lambda i, j, te, tv, tl: (i, j))),
        out_shape=jax.ShapeDtypeStruct((m, n), out_dtype),
        compiler_params=_cparams("arbitrary", "arbitrary"), name="expert_down",
    )(tile_expert, tile_valid, tile_last, a, w)


def _chunk_maps(nct, nlt):
    def fwd(s):
        return s

    def bwd(s):
        return jnp.where(s < nct, nct - 1 - s, 2 * nct + nlt - 1 - s)

    return fwd, bwd


def _tri(l, lower):
    r = lax.broadcasted_iota(jnp.int32, (l, l), 0)
    c = lax.broadcasted_iota(jnp.int32, (l, l), 1)
    return (r >= c) if lower else (r <= c)


def _mlstm_direction(q, k, v, g, gt, c_ref, n_ref, m_ref, o_ref, *, fwd, scale):
    l = q.shape[0]
    ci, cf = (0, 2) if fwd else (1, 3)
    low = _tri(l, True).astype(F32)
    up = _tri(l, False).astype(F32)
    b_col = jnp.dot(low if fwd else up, _log_sigmoid(g), preferred_element_type=F32, precision=HIGHEST)[:, cf:cf + 1]
    b_row = jnp.dot(_log_sigmoid(gt), up if fwd else low, preferred_element_type=F32, precision=HIGHEST)[cf:cf + 1, :]
    i_col = g[:, ci:ci + 1]
    i_row = gt[ci:ci + 1, :]
    causal = _tri(l, fwd)
    logw = jnp.where(causal, b_col - b_row + i_row, -jnp.inf)
    m_intra = jnp.max(logw, axis=-1, keepdims=True)
    m_st = m_ref[...]
    inter_log = b_col + m_st
    m_t = jnp.maximum(inter_log, m_intra)
    a_inter = jnp.exp(inter_log - m_t)
    ks = (k.astype(F32) * scale)
    ksb = ks.astype(BF16)
    qk = lax.dot_general(q, ksb, (((1,), (1,)), ((), ())), preferred_element_type=F32)
    s = qk * jnp.exp(logw - m_t)
    c_st = c_ref[...]
    n_st = n_ref[...]
    num = a_inter * jnp.dot(q, c_st.astype(BF16), preferred_element_type=F32) \
        + jnp.dot(s.astype(BF16), v, preferred_element_type=F32)
    den = a_inter * jnp.sum(q.astype(F32) * n_st, axis=-1, keepdims=True) + jnp.sum(s, axis=-1, keepdims=True)
    o_ref[...] = (num / jnp.maximum(jnp.abs(den), jnp.exp(-m_t))).astype(o_ref.dtype)
    b_last = b_col[l - 1:l, :] if fwd else b_col[0:1, :]
    g_col = b_last - b_col + i_col
    m_new = jnp.maximum(b_last + m_st, jnp.max(g_col, axis=0, keepdims=True))
    kw = ks * jnp.exp(g_col - m_new)
    dec = jnp.exp(b_last + m_st - m_new)
    c_ref[...] = dec * c_st + lax.dot_general(kw.astype(BF16), v, (((0,), (0,)), ((), ())),
                                              preferred_element_type=F32)
    n_ref[...] = dec * n_st + jnp.sum(kw, axis=0, keepdims=True)
    m_ref[...] = m_new


def _mlstm_kernel(qf, kf, vf, gf, gtf, qb, kb, vb, gb, gtb, bias, bias_t, hf, hb,
                  cf, nf, mf, cb, nb, mb, *, scale):
    @pl.when(pl.program_id(1) == 0)
    def _():
        for ref in (cf, nf, mf, cb, nb, mb):
            ref[...] = jnp.zeros_like(ref)

    _mlstm_direction(qf[...], kf[...], vf[...], gf[...] + bias[...], gtf[...] + bias_t[...],
                     cf, nf, mf, hf, fwd=True, scale=scale)
    _mlstm_direction(qb[...], kb[...], vb[...], gb[...] + bias[...], gtb[...] + bias_t[...],
                     cb, nb, mb, hb, fwd=False, scale=scale)


def _mlstm(big, gates, gate_b, nh, dqk, dv, n_ctx, col_q, col_k, col_v):
    s = big.shape[0]
    l = _pick(math.gcd(n_ctx, s - n_ctx), SCAN_CHUNK, 8)
    nct, nlt = n_ctx // l, (s - n_ctx) // l
    fwd, bwd = _chunk_maps(nct, nlt)
    gh = gates.reshape(s, 4, nh).transpose(2, 0, 1)
    g_cols = jnp.zeros((nh, s, LANES), F32).at[:, :, :4].set(gh)
    g_rows = jnp.zeros((nh, 8, s), F32).at[:, :4, :].set(gh.transpose(0, 2, 1))
    bh = gate_b.reshape(4, nh).T
    b_cols = jnp.zeros((nh, 1, LANES), F32).at[:, 0, :4].set(bh)
    b_rows = jnp.zeros((nh, 8, 1), F32).at[:, :4, 0].set(bh)
    qo, ko, vo = col_q // dqk, col_k // dqk, col_v // dv

    def specs(cm):
        return [pl.BlockSpec((l, dqk), lambda h, t: (cm(t), qo + h)),
                pl.BlockSpec((l, dqk), lambda h, t: (cm(t), ko + h)),
                pl.BlockSpec((l, dv), lambda h, t: (cm(t), vo + h)),
                pl.BlockSpec((None, l, LANES), lambda h, t: (h, cm(t), 0)),
                pl.BlockSpec((None, 8, l), lambda h, t: (h, 0, cm(t)))]

    out_f = pl.BlockSpec((l, dv), lambda h, t: (fwd(t), h))
    out_b = pl.BlockSpec((l, dv), lambda h, t: (bwd(t), h))
    state = [pltpu.VMEM((dqk, dv), F32), pltpu.VMEM((1, dqk), F32), pltpu.VMEM((1, 1), F32)]
    return pl.pallas_call(
        functools.partial(_mlstm_kernel, scale=dqk ** -0.5),
        grid=(nh, nct + nlt),
        in_specs=specs(fwd) + specs(bwd) + [pl.BlockSpec((None, 1, LANES), lambda h, t: (h, 0, 0)),
                                             pl.BlockSpec((None, 8, 1), lambda h, t: (h, 0, 0))],
        out_specs=[out_f, out_b],
        out_shape=[jax.ShapeDtypeStruct((s, nh * dv), F32)] * 2,
        scratch_shapes=state + state,
        compiler_params=_cparams("parallel", "arbitrary"), name="mlstm_scan",
    )(big, big, big, g_cols, g_rows, big, big, big, g_cols, g_rows, b_cols, b_rows)


def _conv_kernel(u_ref, prev_ref, next_ref, w_ref, b_ref, o_ref, *, nct, nt, width, halo):
    i = pl.program_id(0)
    first = jnp.logical_or(i == 0, i == nct)
    last = jnp.logical_or(i == nct - 1, i == nt - 1)
    r = u_ref.shape[0]
    prev = jnp.where(first, 0.0, prev_ref[...].astype(F32))
    nxt = jnp.where(last, 0.0, next_ref[...].astype(F32))
    ext = jnp.concatenate([prev, u_ref[...].astype(F32), nxt], axis=0)
    w = w_ref[...]
    acc = b_ref[...] + w[0:1, :] * ext[halo - width // 2: halo - width // 2 + r]
    for j in range(1, width):
        off = halo - width // 2 + j
        acc = acc + w[j:j + 1, :] * ext[off: off + r]
    o_ref[...] = _silu(acc).astype(o_ref.dtype)


def _dwconv_silu(u, conv_w, conv_b, n_ctx):
    s, ch = u.shape
    width = conv_w.shape[0]
    halo = 16
    r = _pick(math.gcd(n_ctx, s - n_ctx), ROW_TILE, halo)
    tc = _pick(ch, 1024, LANES)
    nt, nct, rb = s // r, n_ctx // r, r // halo
    w8 = jnp.zeros((8, ch), F32).at[:width].set(conv_w)
    return pl.pallas_call(
        functools.partial(_conv_kernel, nct=nct, nt=nt, width=width, halo=halo),
        grid=(nt, ch // tc),
        in_specs=[pl.BlockSpec((r, tc), lambda i, j: (i, j)),
                  pl.BlockSpec((halo, tc), lambda i, j: (jnp.maximum(i * rb - 1, 0), j)),
                  pl.BlockSpec((halo, tc), lambda i, j: (jnp.minimum((i + 1) * rb, s // halo - 1), j)),
                  pl.BlockSpec((8, tc), lambda i, j: (0, j)),
                  pl.BlockSpec((1, tc), lambda i, j: (0, j))],
        out_specs=pl.BlockSpec((r, tc), lambda i, j: (i, j)),
        out_shape=jax.ShapeDtypeStruct((s, ch), BF16),
        compiler_params=_cparams("parallel", "parallel"), name="dwconv_silu",
    )(u, u, u, w8, conv_b.reshape(1, ch))


def _ssd_direction(x, bm, cm, dtv, da, da_t, ex, st_ref, *, fwd, hg, p):
    l = x.shape[0]
    off = 0 if fwd else hg
    low = _tri(l, True).astype(F32)
    up = _tri(l, False).astype(F32)
    acum_col = jnp.dot(low if fwd else up, da, preferred_element_type=F32, precision=HIGHEST)
    acum_row = jnp.dot(da_t, up if fwd else low, preferred_element_type=F32, precision=HIGHEST)
    acx = jnp.dot(acum_col, ex, preferred_element_type=F32, precision=HIGHEST)
    dtx = jnp.dot(dtv, ex, preferred_element_type=F32, precision=HIGHEST)
    xdt = x.astype(F32) * dtx
    xdt_b = xdt.astype(BF16)
    causal = _tri(l, fwd)
    cb = lax.dot_general(cm, bm, (((1,), (1,)), ((), ())), preferred_element_type=F32)
    lane = lax.broadcasted_iota(jnp.int32, xdt_b.shape, 1)
    y = None
    for h in range(hg):
        seg = acum_col[:, off + h:off + h + 1] - acum_row[off + h:off + h + 1, :]
        mix = (cb * jnp.exp(jnp.where(causal, seg, -jnp.inf))).astype(BF16)
        own = jnp.logical_and(lane >= h * p, lane < (h + 1) * p)
        part = jnp.dot(mix, jnp.where(own, xdt_b, jnp.zeros_like(xdt_b)), preferred_element_type=F32)
        y = part if y is None else y + part
    st = st_ref[...]
    y = y + jnp.dot(cm, st.astype(BF16), preferred_element_type=F32) * jnp.exp(acx)
    total = acx[l - 1:l, :] if fwd else acx[0:1, :]
    xw = (xdt * jnp.exp(total - acx)).astype(BF16)
    st_ref[...] = jnp.exp(total) * st + lax.dot_general(bm, xw, (((0,), (0,)), ((), ())),
                                                        preferred_element_type=F32)
    return y


def _ssd_kernel(xf, bf, cf, dtf, dttf, xb, bb, cb, dtb, dttb, a_c, a_r, db_c, db_r, ex_f, ex_b, dsk,
                yf, yb, stf, stb, *, hg, p):
    @pl.when(pl.program_id(1) == 0)
    def _():
        stf[...] = jnp.zeros_like(stf)
        stb[...] = jnp.zeros_like(stb)

    def steps(dt_ref, dtt_ref):
        dtv = _softplus(dt_ref[...] + db_c[...])
        da_t = _softplus(dtt_ref[...] + db_r[...]) * a_r[...]
        return dtv, dtv * a_c[...], da_t

    x = xf[...]
    y = _ssd_direction(x, bf[...], cf[...], *steps(dtf, dttf), ex_f[...], stf, fwd=True, hg=hg, p=p)
    yf[...] = (y + dsk[...] * x.astype(F32)).astype(yf.dtype)
    y = _ssd_direction(xb[...], bb[...], cb[...], *steps(dtb, dttb), ex_b[...], stb, fwd=False, hg=hg, p=p)
    yb[...] = y.astype(yb.dtype)


def _ssd(xbc, dt_raw, a_log, dt_bias, d_skip, n_ctx, sw, n_state):
    s = xbc.shape[0]
    h = a_log.shape[1]
    g = SSM_GROUPS
    hg, p = h // g, sw // h
    gw = hg * p
    l = _pick(math.gcd(n_ctx, s - n_ctx), SCAN_CHUNK, 8)
    nct, nlt = n_ctx // l, (s - n_ctx) // l
    fwd, bwd = _chunk_maps(nct, nlt)
    dtg = dt_raw.reshape(s, 2, g, hg).transpose(2, 0, 1, 3).reshape(g, s, 2 * hg)
    dt_cols = jnp.zeros((g, s, LANES), F32).at[:, :, :2 * hg].set(dtg)
    assert 2 * hg <= 8 and sw % n_state == 0
    dt_rows = jnp.zeros((g, 8, s), F32).at[:, :2 * hg, :].set(dtg.transpose(0, 2, 1))
    a = (-jnp.exp(a_log.astype(F32))).reshape(2, g, hg).transpose(1, 0, 2).reshape(g, 2 * hg)
    db = dt_bias.reshape(2, g, hg).transpose(1, 0, 2).reshape(g, 2 * hg)
    a_c = jnp.zeros((g, 1, LANES), F32).at[:, 0, :2 * hg].set(a)
    a_r = jnp.zeros((g, 8, 1), F32).at[:, :2 * hg, 0].set(a)
    db_c = jnp.zeros((g, 1, LANES), F32).at[:, 0, :2 * hg].set(db)
    db_r = jnp.zeros((g, 8, 1), F32).at[:, :2 * hg, 0].set(db)
    head_of = jnp.arange(gw) // p
    ex_f = (jnp.arange(LANES)[:, None] == head_of[None, :]).astype(F32)
    ex_b = (jnp.arange(LANES)[:, None] == (head_of[None, :] + hg)).astype(F32)
    dsk = jnp.repeat(d_skip, p).reshape(g, 1, gw)
    xo, bo, co = 0, sw // n_state, (sw + g * n_state) // n_state

    def specs(cm):
        return [pl.BlockSpec((l, gw), lambda gi, t: (cm(t), gi)),
                pl.BlockSpec((l, n_state), lambda gi, t: (cm(t), bo + gi)),
                pl.BlockSpec((l, n_state), lambda gi, t: (cm(t), co + gi)),
                pl.BlockSpec((None, l, LANES), lambda gi, t: (gi, cm(t), 0)),
                pl.BlockSpec((None, 8, l), lambda gi, t: (gi, 0, cm(t)))]

    par_c = pl.BlockSpec((None, 1, LANES), lambda gi, t: (gi, 0, 0))
    par_r = pl.BlockSpec((None, 8, 1), lambda gi, t: (gi, 0, 0))
    exs = pl.BlockSpec((LANES, gw), lambda gi, t: (0, 0))
    return pl.pallas_call(
        functools.partial(_ssd_kernel, hg=hg, p=p),
        grid=(g, nct + nlt),
        in_specs=specs(fwd) + specs(bwd) + [par_c, par_r, par_c, par_r, exs, exs,
                                             pl.BlockSpec((None, 1, gw), lambda gi, t: (gi, 0, 0))],
        out_specs=[pl.BlockSpec((l, gw), lambda gi, t: (fwd(t), gi)),
                   pl.BlockSpec((l, gw), lambda gi, t: (bwd(t), gi))],
        out_shape=[jax.ShapeDtypeStruct((s, sw), F32)] * 2,
        scratch_shapes=[pltpu.VMEM((n_state, gw), F32)] * 2,
        compiler_params=_cparams("parallel", "arbitrary"), name="ssd_scan",
    )(xbc, xbc, xbc, dt_cols, dt_rows, xbc, xbc, xbc, dt_cols, dt_rows, a_c, a_r, db_c, db_r, ex_f, ex_b, dsk)


def _finish_kernel(hf, hb, o, yf, yb, z, mg, sg, out, *, nh, dv):
    mw = nh * dv
    h = hf[...] + hb[...]
    gate = _sigmoid(o[...].astype(F32))
    mgv = mg[...]
    for k in range(nh):
        hk = h[:, k * dv:(k + 1) * dv]
        r = lax.rsqrt(jnp.mean(hk * hk, axis=-1, keepdims=True) + EPS)
        out[:, k * dv:(k + 1) * dv] = (hk * r * mgv[:, k * dv:(k + 1) * dv]
                                       * gate[:, k * dv:(k + 1) * dv]).astype(out.dtype)
    y = (yf[...] + yb[...]) * _silu(z[...].astype(F32))
    r = lax.rsqrt(jnp.mean(y * y, axis=-1, keepdims=True) + EPS)
    out[:, mw:] = (y * r * sg[...]).astype(out.dtype)


def _finish(hf, hb, big, yf, yb, mlstm_g, ssm_g, nh, dv, col_o, col_z):
    s, mw = hf.shape
    sw = yf.shape[1]
    r = _pick(s, ROW_TILE, 8)
    return pl.pallas_call(
        functools.partial(_finish_kernel, nh=nh, dv=dv),
        grid=(s // r,),
        in_specs=[pl.BlockSpec((r, mw), lambda i: (i, 0)), pl.BlockSpec((r, mw), lambda i: (i, 0)),
                  pl.BlockSpec((r, mw), lambda i: (i, col_o // mw)),
                  pl.BlockSpec((r, sw), lambda i: (i, 0)), pl.BlockSpec((r, sw), lambda i: (i, 0)),
                  pl.BlockSpec((r, sw), lambda i: (i, col_z // sw)),
                  pl.BlockSpec((1, mw), lambda i: (0, 0)), pl.BlockSpec((1, sw), lambda i: (0, 0))],
        out_specs=pl.BlockSpec((r, mw + sw), lambda i: (i, 0)),
        out_shape=jax.ShapeDtypeStruct((s, mw + sw), BF16),
        compiler_params=_cparams("parallel"), name="mixer_finish",
    )(hf, hb, big, yf, yb, big, mlstm_g.reshape(1, mw), ssm_g.reshape(1, sw))


def _gather_kernel(idx_ref, src_ref, o_ref, sem):
    n = o_ref.shape[0]

    def start(r, carry):
        pltpu.make_async_copy(src_ref.at[pl.ds(idx_ref[0, 0, r], 1)], o_ref.at[pl.ds(r, 1)], sem).start()
        return carry

    lax.fori_loop(0, n, start, 0)

    def wait(r, carry):
        pltpu.make_async_copy(src_ref.at[pl.ds(0, 1)], o_ref.at[pl.ds(r, 1)], sem).wait()
        return carry

    lax.fori_loop(0, n, wait, 0)


def _gather_rows(src, idx):
    n = idx.shape[0]
    w = src.shape[1]
    r = _pick(n, GATHER_ROWS, 8)
    return pl.pallas_call(
        _gather_kernel, grid=(n // r,),
        in_specs=[pl.BlockSpec((1, 1, r), lambda i: (i, 0, 0), memory_space=pltpu.SMEM),
                  pl.BlockSpec(memory_space=pl.ANY)],
        out_specs=pl.BlockSpec((r, w), lambda i: (i, 0)),
        out_shape=jax.ShapeDtypeStruct((n, w), src.dtype),
        scratch_shapes=[pltpu.SemaphoreType.DMA(())],
        compiler_params=_cparams("arbitrary"), name="moe_gather",
    )(idx.reshape(n // r, 1, r), src)


def _combine_kernel(p0_ref, p1_ref, yr_ref, x_ref, gk_ref, gate_ref, o_ref, buf, sem):
    n = x_ref.shape[0]

    def start(r, carry):
        pltpu.make_async_copy(yr_ref.at[pl.ds(p0_ref[0, 0, r], 1)], buf.at[0, pl.ds(r, 1)], sem).start()
        pltpu.make_async_copy(yr_ref.at[pl.ds(p1_ref[0, 0, r], 1)], buf.at[1, pl.ds(r, 1)], sem).start()
        return carry

    lax.fori_loop(0, n, start, 0)

    def wait(r, carry):
        pltpu.make_async_copy(yr_ref.at[pl.ds(0, 1)], buf.at[0, pl.ds(r, 1)], sem).wait()
        pltpu.make_async_copy(yr_ref.at[pl.ds(0, 1)], buf.at[1, pl.ds(r, 1)], sem).wait()
        return carry

    lax.fori_loop(0, n, wait, 0)
    gk = gk_ref[...]
    moe = gk[:, 0:1] * buf[0] + gk[:, 1:2] * buf[1]
    o_ref[...] = x_ref[...] + gate_ref[...] * moe


def _moe_combine(xs, yr, pos, gates, gate_mod, n_ctx):
    s, d = xs.shape
    t = s - n_ctx
    r = _pick(math.gcd(n_ctx, t), COMBINE_ROWS, 8)
    nct = n_ctx // r
    gk = jnp.zeros((t, LANES), F32).at[:, :TOP_K].set(gates)
    p0 = pos[:, 0].reshape(t // r, 1, r)
    p1 = pos[:, 1].reshape(t // r, 1, r)
    out = pl.pallas_call(
        _combine_kernel, grid=(t // r,),
        in_specs=[pl.BlockSpec((1, 1, r), lambda i: (i, 0, 0), memory_space=pltpu.SMEM),
                  pl.BlockSpec((1, 1, r), lambda i: (i, 0, 0), memory_space=pltpu.SMEM),
                  pl.BlockSpec(memory_space=pl.ANY),
                  pl.BlockSpec((r, d), lambda i: (i + nct, 0)),
                  pl.BlockSpec((r, LANES), lambda i: (i, 0)),
                  pl.BlockSpec((1, d), lambda i: (0, 0))],
        out_specs=pl.BlockSpec((r, d), lambda i: (i, 0)),
        out_shape=jax.ShapeDtypeStruct((t, d), F32),
        scratch_shapes=[pltpu.VMEM((2, r, d), F32), pltpu.SemaphoreType.DMA(())],
        compiler_params=_cparams("arbitrary"), name="moe_combine",
    )(p0, p1, yr, xs, gk, gate_mod)
    return jnp.concatenate([xs[:n_ctx], out], axis=0)


def _moe_plan(logits, ne, tm):
    t = logits.shape[0]
    top_val, top_idx = lax.top_k(logits[:, :ne], TOP_K)
    gates = jax.nn.softmax(top_val, axis=-1)
    n_assign = t * TOP_K
    e_flat = top_idx.reshape(-1)
    onehot = (e_flat[:, None] == jnp.arange(ne)[None, :]).astype(jnp.int32)
    rank = jnp.take_along_axis(jnp.cumsum(onehot, axis=0) - onehot, e_flat[:, None], axis=1)[:, 0]
    counts = jnp.sum(onehot, axis=0)
    padded = (counts + tm - 1) // tm * tm
    pend = jnp.cumsum(padded)
    pstart = pend - padded
    pos = (pstart[e_flat] + rank).astype(jnp.int32)
    n_tiles = (n_assign + ne * (tm - 1)) // tm
    n_rows = n_tiles * tm
    tok = jnp.repeat(jnp.arange(t, dtype=jnp.int32), TOP_K)
    row_tok = jnp.zeros((n_rows,), jnp.int32).at[pos].set(tok)
    tile_start = jnp.arange(n_tiles, dtype=jnp.int32) * tm
    tile_expert = jnp.minimum(jnp.searchsorted(pend, tile_start, side='right'), ne - 1).astype(jnp.int32)
    tile_valid = (tile_start < pend[-1]).astype(jnp.int32)
    tile_last = (pend[-1] // tm - 1).astype(jnp.int32).reshape(1)
    return gates, pos.reshape(t, TOP_K), row_tok, tile_expert, tile_valid, tile_last


def _final_norm_kernel(x_ref, g_ref, o_ref):
    x = x_ref[...]
    o_ref[...] = x * lax.rsqrt(jnp.mean(x * x, axis=-1, keepdims=True) + EPS) * g_ref[...]


def _final_norm(xs, g, n_ctx):
    s, d = xs.shape
    t = s - n_ctx
    r = _pick(math.gcd(n_ctx, t), ROW_TILE, 8)
    nct = n_ctx // r
    return pl.pallas_call(
        _final_norm_kernel, grid=(t // r,),
        in_specs=[pl.BlockSpec((r, d), lambda i: (i + nct, 0)), pl.BlockSpec((1, d), lambda i: (0, 0))],
        out_specs=pl.BlockSpec((r, d), lambda i: (i, 0)),
        out_shape=jax.ShapeDtypeStruct((t, d), F32),
        compiler_params=_cparams("parallel"), name="final_rmsnorm",
    )(xs, g.reshape(1, d))


def _to_colmajor(a, n_ctx):
    t, ch = a.shape[0] - n_ctx, a.shape[1]
    lat = a[n_ctx:].reshape(t // GRID_W, GRID_W, ch).transpose(1, 0, 2).reshape(t, ch)
    return jnp.concatenate([a[:n_ctx], lat], axis=0)


def _from_colmajor(a, n_ctx):
    t, ch = a.shape[0] - n_ctx, a.shape[1]
    lat = a[n_ctx:].reshape(GRID_W, t // GRID_W, ch).transpose(1, 0, 2).reshape(t, ch)
    return jnp.concatenate([a[:n_ctx], lat], axis=0)


def kernel(x, c, ctx, c_ctx, w_mod, b_mod, norm1_g, norm2_g, w_in, gate_b, conv_w, conv_b, a_log, dt_bias, d_skip, mlstm_g, ssm_g, w_out, ffn_w1, ffn_w3, ffn_w2, router_w, moe_w1, moe_w3, moe_w2, final_g):
    bsz, t, d = x.shape
    assert bsz == 1, "the kernels treat the single batch element's tokens as rows"
    n_ctx = ctx.shape[1]
    depth = w_mod.shape[0]
    nh = gate_b.shape[1] // 4
    mw = mlstm_g.shape[1]
    dv = mw // nh
    dqk = dv // 2
    sw = ssm_g.shape[1]
    heads = a_log.shape[2]
    conv_ch = conv_w.shape[2]
    n_state = (conv_ch - sw) // (2 * SSM_GROUPS)
    wq = nh * dqk
    splits = (wq, wq, mw, mw, 4 * nh, sw, conv_ch, 2 * heads)
    offs = [0]
    for w_ in splits:
        offs.append(offs[-1] + w_)
    assert offs[-1] == w_in.shape[2]
    ne = router_w.shape[2]

    xs = jnp.concatenate([ctx[0], x[0]], axis=0).astype(F32)
    cvec = jnp.zeros((8, d), F32).at[0].set(c_ctx.astype(F32)).at[1].set(c[0].astype(F32))
    mod = _modulation(cvec, w_mod.astype(F32), b_mod.astype(F32))

    one_tile = jnp.zeros((1,), jnp.int32)
    col_q, col_k, col_v, col_o, col_z, col_x = 0, wq, 2 * wq, 2 * wq + mw, 2 * wq + 2 * mw, 2 * wq + 2 * mw + sw
    n_small = -(-(4 * nh + 2 * heads) // LANES) * LANES

    for layer in range(depth):
        need_ctx = layer < depth - 1
        m6 = mod[layer, :2].reshape(2, 6, d)
        shift1, scale1, gate1, shift2, scale2, gate2 = (m6[:, i][:, None, :] for i in range(6))

        wl = w_in[layer]
        w_big = jnp.concatenate([wl[:, offs[0]:offs[4]], wl[:, offs[5]:offs[7]]], axis=1).astype(BF16)
        w_small = jnp.zeros((d, n_small), BF16)
        w_small = w_small.at[:, :4 * nh].set(wl[:, offs[4]:offs[5]].astype(BF16))
        w_small = w_small.at[:, 4 * nh:4 * nh + 2 * heads].set(wl[:, offs[7]:offs[8]].astype(BF16))
        xn = _norm_mod(xs, norm1_g[layer], scale1, shift1, n_ctx)
        big = _matmul(xn, w_big, BF16)
        small = _matmul(xn, w_small, F32)

        hf, hb = _mlstm(big, small[:, :4 * nh], gate_b[layer], nh, dqk, dv, n_ctx, col_q, col_k, col_v)

        xbc = _dwconv_silu(_to_colmajor(big[:, col_x:], n_ctx), conv_w[layer], conv_b[layer], n_ctx)
        dt_raw = _to_colmajor(small[:, 4 * nh:4 * nh + 2 * heads], n_ctx)
        yf, yb = _ssd(xbc, dt_raw, a_log[layer], dt_bias[layer], d_skip[layer], n_ctx, sw, n_state)
        yf, yb = _from_colmajor(yf, n_ctx), _from_colmajor(yb, n_ctx)

        mix = _finish(hf, hb, big, yf, yb, mlstm_g[layer], ssm_g[layer], nh, dv, col_o, col_z)
        xs = _matmul_residual(mix, w_out[layer].astype(BF16), xs, gate1, n_ctx, d)

        i = layer // 2
        if layer % 2 == 0:
            f = ffn_w1.shape[2]
            fp = -(-f // FFN_PAD) * FFN_PAD
            w1 = jnp.zeros((1, d, fp), BF16).at[0, :, :f].set(ffn_w1[i].astype(BF16))
            w3 = jnp.zeros((1, d, fp), BF16).at[0, :, :f].set(ffn_w3[i].astype(BF16))
            w2 = jnp.zeros((fp, d), BF16).at[:f].set(ffn_w2[i].astype(BF16))
            hn = _norm_mod(xs, norm2_g[layer], scale2, shift2, n_ctx)
            tm = _pick(hn.shape[0], MM_TM, 8)
            nt = hn.shape[0] // tm
            act = _glu(hn, w1, w3, tm, jnp.zeros((nt,), jnp.int32), jnp.ones((nt,), jnp.int32),
                       one_tile + (nt - 1))
            xs = _matmul_residual(act, w2, xs, gate2, n_ctx, fp // 4)
        else:
            assert not need_ctx, "an MoE layer that still feeds context is not needed by this block"
            hn, logits = _norm_mod(xs, norm2_g[layer], scale2, shift2, n_ctx, router_w=router_w[i])
            tm = min(MOE_TM, t)
            gates, pos, row_tok, tile_expert, tile_valid, tile_last = _moe_plan(logits[n_ctx:], ne, tm)
            hn32 = lax.bitcast_convert_type(hn.reshape(hn.shape[0], d // 2, 2), jnp.uint32)
            xr32 = _gather_rows(hn32, row_tok + n_ctx)
            xr = lax.bitcast_convert_type(xr32, BF16).reshape(xr32.shape[0], d)
            act = _glu(xr, moe_w1[i].astype(BF16), moe_w3[i].astype(BF16), tm, tile_expert, tile_valid, tile_last)
            yr = _matmul_expert(act, moe_w2[i].astype(BF16), tm, tile_expert, tile_valid, tile_last, F32)
            xs = _moe_combine(xs, yr, pos, gates, gate2[1], n_ctx)

    return _final_norm(xs, final_g, n_ctx)[None]
```

```python
import functools
import math

import jax
import jax.numpy as jnp
from jax import lax
from jax.experimental import pallas as pl
from jax.experimental.pallas import tpu as pltpu

GRID_W = 64
SSM_GROUPS = 8
TOP_K = 2
EPS = 1e-6

V7X_VMEM_LIMIT_BYTES = 56 * 1024 * 1024
LANES = 128
SCAN_CHUNK = 256
ROW_TILE = 256
MM_TM = 768
MM_TN = 1024
GLU_TN = 512
FFN_TM = 384
MOE_TM = 512
GATHER_ROWS = 256
COMBINE_ROWS = 128

F32 = jnp.float32
BF16 = jnp.bfloat16
HIGHEST = lax.Precision.HIGHEST


def _pick(dim, pref, mult):
    if dim <= pref:
        return dim
    t = (pref // mult) * mult
    while t >= mult:
        if dim % t == 0:
            return t
        t -= mult
    return dim


def _cparams(*sem):
    return pltpu.CompilerParams(dimension_semantics=sem, vmem_limit_bytes=V7X_VMEM_LIMIT_BYTES)


def _sigmoid(v):
    return 1.0 / (1.0 + jnp.exp(-v))


def _silu(v):
    return v * _sigmoid(v)


def _softplus(v):
    return jnp.maximum(v, 0.0) + jnp.log1p(jnp.exp(-jnp.abs(v)))


def _log_sigmoid(v):
    return -_softplus(-v)


def _split3(v):
    hi = v.astype(BF16)
    r = v - hi.astype(F32)
    mid = r.astype(BF16)
    lo = (r - mid.astype(F32)).astype(BF16)
    return hi, mid, lo


def _dot_exact(mat01, v, left):
    m = mat01.astype(BF16)
    out = None
    for piece in _split3(v):
        part = (jnp.dot(m, piece, preferred_element_type=F32) if left
                else jnp.dot(piece, m, preferred_element_type=F32))
        out = part if out is None else out + part
    return out


def _mod_kernel(c_ref, w_ref, b_ref, o_ref):
    s_ctx = _silu(c_ref[0])
    s_lat = _silu(c_ref[1])
    o_ref[...] = jnp.zeros_like(o_ref)
    for n0 in range(0, o_ref.shape[1], LANES):
        w = w_ref[:, n0:n0 + LANES]
        b = b_ref[:, n0:n0 + LANES]
        o_ref[0:1, n0:n0 + LANES] = jnp.sum(w * s_ctx, axis=0, keepdims=True) + b
        o_ref[1:2, n0:n0 + LANES] = jnp.sum(w * s_lat, axis=0, keepdims=True) + b


def _modulation(c_ctx, c_lat, w_mod, b_mod):
    depth, d, n = w_mod.shape
    tn = _pick(n, 1024, LANES)
    cb = jnp.broadcast_to(jnp.stack([c_ctx, c_lat])[:, :, None], (2, d, LANES))
    return pl.pallas_call(
        _mod_kernel,
        grid=(depth, n // tn),
        in_specs=[pl.BlockSpec((2, d, LANES), lambda l, j: (0, 0, 0)),
                  pl.BlockSpec((None, d, tn), lambda l, j: (l, 0, j)),
                  pl.BlockSpec((None, 1, tn), lambda l, j: (l, 0, j))],
        out_specs=pl.BlockSpec((None, 8, tn), lambda l, j: (l, 0, j)),
        out_shape=jax.ShapeDtypeStruct((depth, 8, n), F32),
        compiler_params=_cparams("parallel", "parallel"),
        name="adaln_modulation",
    )(cb, w_mod, b_mod.reshape(depth, 1, n))


def _norm_kernel(x_ref, g_ref, sc_ref, sh_ref, o_ref):
    x = x_ref[...]
    r = lax.rsqrt(jnp.mean(x * x, axis=-1, keepdims=True) + EPS)
    o_ref[...] = (x * r * g_ref[...] * (1.0 + sc_ref[...]) + sh_ref[...]).astype(o_ref.dtype)


def _norm_router_kernel(x_ref, g_ref, sc_ref, sh_ref, rw_ref, o_ref, lg_ref):
    x = x_ref[...]
    r = lax.rsqrt(jnp.mean(x * x, axis=-1, keepdims=True) + EPS)
    h = x * r * g_ref[...] * (1.0 + sc_ref[...]) + sh_ref[...]
    o_ref[...] = h.astype(o_ref.dtype)
    lg_ref[...] = jnp.dot(h, rw_ref[...], preferred_element_type=F32, precision=HIGHEST)


def _norm_mod(xs, g, scale, shift, n_ctx, router_w=None):
    s, d = xs.shape
    r = _pick(math.gcd(n_ctx, s - n_ctx), ROW_TILE, 8)
    nct = n_ctx // r
    row_spec = pl.BlockSpec((r, d), lambda i: (i, 0))
    mod_spec = pl.BlockSpec((None, 1, d), lambda i: (jnp.where(i < nct, 0, 1), 0, 0))
    in_specs = [row_spec, pl.BlockSpec((1, d), lambda i: (0, 0)), mod_spec, mod_spec]
    if router_w is None:
        return pl.pallas_call(
            _norm_kernel, grid=(s // r,), in_specs=in_specs, out_specs=row_spec,
            out_shape=jax.ShapeDtypeStruct((s, d), BF16),
            compiler_params=_cparams("parallel"), name="rmsnorm_modulate",
        )(xs, g.reshape(1, d), scale, shift)
    ne = router_w.shape[1]
    rw = jnp.zeros((d, LANES), F32).at[:, :ne].set(router_w)
    return pl.pallas_call(
        _norm_router_kernel, grid=(s // r,),
        in_specs=in_specs + [pl.BlockSpec((d, LANES), lambda i: (0, 0))],
        out_specs=[row_spec, pl.BlockSpec((r, LANES), lambda i: (i, 0))],
        out_shape=[jax.ShapeDtypeStruct((s, d), F32), jax.ShapeDtypeStruct((s, LANES), F32)],
        compiler_params=_cparams("parallel"), name="rmsnorm_modulate_router",
    )(xs, g.reshape(1, d), scale, shift, rw)


def _mm_kernel(a_ref, w_ref, o_ref):
    o_ref[...] = jnp.dot(a_ref[...], w_ref[...], preferred_element_type=F32).astype(o_ref.dtype)


def _matmul(a, w, out_dtype):
    m, k = a.shape
    n = w.shape[1]
    tm = _pick(m, MM_TM, 8)
    tn = _pick(n, MM_TN, LANES)
    return pl.pallas_call(
        _mm_kernel, grid=(m // tm, n // tn),
        in_specs=[pl.BlockSpec((tm, k), lambda i, j: (i, 0)), pl.BlockSpec((k, tn), lambda i, j: (0, j))],
        out_specs=pl.BlockSpec((tm, tn), lambda i, j: (i, j)),
        out_shape=jax.ShapeDtypeStruct((m, n), out_dtype),
        compiler_params=_cparams("parallel", "parallel"), name="matmul",
    )(a, w)


def _mm_res_kernel(a_ref, w_ref, res_ref, gate_ref, o_ref, acc_ref, *, nk, tm, n_ctx):
    k = pl.program_id(2)

    @pl.when(k == 0)
    def _():
        acc_ref[...] = jnp.zeros_like(acc_ref)

    acc_ref[...] += jnp.dot(a_ref[...], w_ref[...], preferred_element_type=F32)

    @pl.when(k == nk - 1)
    def _():
        rows = pl.program_id(0) * tm + lax.broadcasted_iota(jnp.int32, (tm, 1), 0)
        gate = jnp.where(rows < n_ctx, gate_ref[0], gate_ref[1])
        o_ref[...] = res_ref[...] + gate * acc_ref[...]


def _matmul_residual(a, w, res, gate, n_ctx, tk_pref):
    m, kdim = a.shape
    n = w.shape[1]
    tm = _pick(m, MM_TM, 8)
    tn = _pick(n, MM_TN, LANES)
    tk = _pick(kdim, tk_pref, LANES)
    nk = kdim // tk
    return pl.pallas_call(
        functools.partial(_mm_res_kernel, nk=nk, tm=tm, n_ctx=n_ctx),
        grid=(m // tm, n // tn, nk),
        in_specs=[pl.BlockSpec((tm, tk), lambda i, j, k: (i, k)),
                  pl.BlockSpec((tk, tn), lambda i, j, k: (k, j)),
                  pl.BlockSpec((tm, tn), lambda i, j, k: (i, j)),
                  pl.BlockSpec((2, 1, tn), lambda i, j, k: (0, 0, j))],
        out_specs=pl.BlockSpec((tm, tn), lambda i, j, k: (i, j)),
        out_shape=jax.ShapeDtypeStruct((m, n), F32),
        scratch_shapes=[pltpu.VMEM((tm, tn), F32)],
        compiler_params=_cparams("parallel", "parallel", "arbitrary"), name="matmul_residual",
    )(a, w, res, gate)


def _stationary_plan(tile_expert, n_used, n_experts, nj):
    n_tiles = tile_expert.shape[0]
    tile_ids = jnp.arange(n_tiles, dtype=jnp.int32)
    cnt = jnp.sum(((tile_expert[:, None] == jnp.arange(n_experts)[None, :])
                   & (tile_ids[:, None] < n_used)).astype(jnp.int32), axis=0)
    seg_end = jnp.cumsum(cnt)
    seg_start = seg_end - cnt
    step = jnp.arange(n_tiles * nj, dtype=jnp.int32)
    used_steps = n_used * nj
    e = jnp.minimum(jnp.searchsorted(seg_end * nj, step, side='right'), n_experts - 1).astype(jnp.int32)
    seg_len = jnp.maximum(cnt[e], 1)
    local = jnp.maximum(step - nj * seg_start[e], 0)
    j = local // seg_len
    i = seg_start[e] + local % seg_len
    used = step < used_steps
    last = used_steps - 1
    fill = step - used_steps
    pick = lambda arr: jnp.where(used, arr, arr[last]).astype(jnp.int32)
    out_i = jnp.where(used, i, n_used + fill // nj).astype(jnp.int32)
    out_j = jnp.where(used, j, fill % nj).astype(jnp.int32)
    return pick(e), pick(j), pick(i), out_i, out_j, used.astype(jnp.int32)


def _stationary_flags(e_ref, jw_ref, used_ref):
    s = pl.program_id(0)
    prev = jnp.maximum(s - 1, 0)
    fresh = jnp.logical_or(s == 0, jnp.logical_or(e_ref[s] != e_ref[prev], jw_ref[s] != jw_ref[prev]))
    used = used_ref[s] > 0
    return used, jnp.logical_and(used, fresh)


def _glu_kernel(e_ref, jw_ref, ia_ref, io_ref, jo_ref, used_ref, a_ref, w1_ref, w3_ref, o_ref, w1b, w3b,
                *, n_valid, ragged):
    used, fresh = _stationary_flags(e_ref, jw_ref, used_ref)

    @pl.when(fresh)
    def _():
        w1 = w1_ref[...]
        w3 = w3_ref[...]
        if ragged:
            col = jw_ref[pl.program_id(0)] * w1.shape[1] + lax.broadcasted_iota(jnp.int32, w1.shape, 1)
            w1 = jnp.where(col < n_valid, w1, 0.0)
            w3 = jnp.where(col < n_valid, w3, 0.0)
        w1b[...] = w1.astype(BF16)
        w3b[...] = w3.astype(BF16)

    @pl.when(used)
    def _():
        a = a_ref[...]
        h1 = jnp.dot(a, w1b[...], preferred_element_type=F32)
        h3 = jnp.dot(a, w3b[...], preferred_element_type=F32)
        o_ref[...] = (_silu(h1) * h3).astype(o_ref.dtype)

    @pl.when(jnp.logical_not(used))
    def _():
        o_ref[...] = jnp.zeros_like(o_ref)


def _stationary_specs(tm, k, tn):
    a_spec = pl.BlockSpec((tm, k), lambda s, e, jw, ia, io, jo, u: (ia[s], 0))
    w_spec = pl.BlockSpec((None, k, tn), lambda s, e, jw, ia, io, jo, u: (e[s], 0, jw[s]))
    o_spec = pl.BlockSpec((tm, tn), lambda s, e, jw, ia, io, jo, u: (io[s], jo[s]))
    return a_spec, w_spec, o_spec


def _glu(a, w1, w3, tm, tile_expert, n_used):
    m, k = a.shape
    ne, _, n = w1.shape
    tn = GLU_TN if n % LANES == 0 and n > GLU_TN else n
    nj = pl.cdiv(n, tn)
    plan = _stationary_plan(tile_expert, n_used, ne, nj)
    a_spec, w_spec, o_spec = _stationary_specs(tm, k, tn)
    return pl.pallas_call(
        functools.partial(_glu_kernel, n_valid=n, ragged=n % tn != 0),
        grid_spec=pltpu.PrefetchScalarGridSpec(
            num_scalar_prefetch=6, grid=(plan[0].shape[0],),
            in_specs=[a_spec, w_spec, w_spec], out_specs=o_spec,
            scratch_shapes=[pltpu.VMEM((k, tn), BF16)] * 2),
        out_shape=jax.ShapeDtypeStruct((m, nj * tn), BF16),
        compiler_params=_cparams("arbitrary"), name="swiglu_up",
    )(*plan, a, w1, w3)


def _mm_expert_kernel(e_ref, jw_ref, ia_ref, io_ref, jo_ref, used_ref, a_ref, w_ref, o_ref, wb):
    used, fresh = _stationary_flags(e_ref, jw_ref, used_ref)

    @pl.when(fresh)
    def _():
        wb[...] = w_ref[...].astype(BF16)

    @pl.when(used)
    def _():
        o_ref[...] = jnp.dot(a_ref[...], wb[...], preferred_element_type=F32).astype(o_ref.dtype)

    @pl.when(jnp.logical_not(used))
    def _():
        o_ref[...] = jnp.zeros_like(o_ref)


def _matmul_expert(a, w, tm, tile_expert, n_used, out_dtype):
    m, k = a.shape
    ne, _, n = w.shape
    tn = _pick(n, GLU_TN, LANES)
    nj = n // tn
    plan = _stationary_plan(tile_expert, n_used, ne, nj)
    a_spec, w_spec, o_spec = _stationary_specs(tm, k, tn)
    return pl.pallas_call(
        _mm_expert_kernel,
        grid_spec=pltpu.PrefetchScalarGridSpec(
            num_scalar_prefetch=6, grid=(plan[0].shape[0],),
            in_specs=[a_spec, w_spec], out_specs=o_spec,
            scratch_shapes=[pltpu.VMEM((k, tn), BF16)]),
        out_shape=jax.ShapeDtypeStruct((m, n), out_dtype),
        compiler_params=_cparams("arbitrary"), name="expert_down",
    )(*plan, a, w)


def _chunk_maps(nct, nlt):
    def fwd(s):
        return s

    def bwd(s):
        return jnp.where(s < nct, nct - 1 - s, 2 * nct + nlt - 1 - s)

    return fwd, bwd


def _tri(l, lower):
    r = lax.broadcasted_iota(jnp.int32, (l, l), 0)
    c = lax.broadcasted_iota(jnp.int32, (l, l), 1)
    return (r >= c) if lower else (r <= c)


def _mlstm_direction(q, k, v, g, gt, c_ref, n_ref, m_ref, o_ref, *, fwd, scale):
    l = q.shape[0]
    ci, cf = (0, 2) if fwd else (1, 3)
    low = _tri(l, True).astype(F32)
    up = _tri(l, False).astype(F32)
    b_col = _dot_exact(low if fwd else up, _log_sigmoid(g), True)[:, cf:cf + 1]
    b_row = _dot_exact(up if fwd else low, _log_sigmoid(gt), False)[cf:cf + 1, :]
    i_col = g[:, ci:ci + 1]
    i_row = gt[ci:ci + 1, :]
    causal = _tri(l, fwd)
    logw = jnp.where(causal, b_col - b_row + i_row, -jnp.inf)
    m_intra = jnp.max(logw, axis=-1, keepdims=True)
    m_st = m_ref[...]
    inter_log = b_col + m_st
    m_t = jnp.maximum(inter_log, m_intra)
    a_inter = jnp.exp(inter_log - m_t)
    ks = (k.astype(F32) * scale)
    ksb = ks.astype(BF16)
    qk = lax.dot_general(q, ksb, (((1,), (1,)), ((), ())), preferred_element_type=F32)
    s = qk * jnp.exp(logw - m_t)
    c_st = c_ref[...]
    n_st = n_ref[...]
    num = a_inter * jnp.dot(q, c_st.astype(BF16), preferred_element_type=F32) \
        + jnp.dot(s.astype(BF16), v, preferred_element_type=F32)
    den = a_inter * jnp.sum(q.astype(F32) * n_st, axis=-1, keepdims=True) + jnp.sum(s, axis=-1, keepdims=True)
    o_ref[...] = (num / jnp.maximum(jnp.abs(den), jnp.exp(-m_t))).astype(o_ref.dtype)
    b_last = b_col[l - 1:l, :] if fwd else b_col[0:1, :]
    g_col = b_last - b_col + i_col
    m_new = jnp.maximum(b_last + m_st, jnp.max(g_col, axis=0, keepdims=True))
    kw = ks * jnp.exp(g_col - m_new)
    dec = jnp.exp(b_last + m_st - m_new)
    c_ref[...] = dec * c_st + lax.dot_general(kw.astype(BF16), v, (((0,), (0,)), ((), ())),
                                              preferred_element_type=F32)
    n_ref[...] = dec * n_st + jnp.sum(kw, axis=0, keepdims=True)
    m_ref[...] = m_new


def _mlstm_kernel(qf, kf, vf, gf, gtf, qb, kb, vb, gb, gtb, bias, bias_t, hf, hb,
                  cf, nf, mf, cb, nb, mb, *, scale):
    @pl.when(pl.program_id(1) == 0)
    def _():
        for ref in (cf, nf, mf, cb, nb, mb):
            ref[...] = jnp.zeros_like(ref)

    _mlstm_direction(qf[...], kf[...], vf[...], gf[...] + bias[...], gtf[...] + bias_t[...],
                     cf, nf, mf, hf, fwd=True, scale=scale)
    _mlstm_direction(qb[...], kb[...], vb[...], gb[...] + bias[...], gtb[...] + bias_t[...],
                     cb, nb, mb, hb, fwd=False, scale=scale)


def _mlstm(big, gates, gate_b, nh, dqk, dv, n_ctx, col_q, col_k, col_v):
    s = big.shape[0]
    l = _pick(math.gcd(n_ctx, s - n_ctx), SCAN_CHUNK, 8)
    nct, nlt = n_ctx // l, (s - n_ctx) // l
    fwd, bwd = _chunk_maps(nct, nlt)
    gh = gates.reshape(s, 4, nh).transpose(2, 0, 1)
    g_cols = jnp.zeros((nh, s, LANES), F32).at[:, :, :4].set(gh)
    g_rows = jnp.zeros((nh, 8, s), F32).at[:, :4, :].set(gh.transpose(0, 2, 1))
    bh = gate_b.reshape(4, nh).T
    b_cols = jnp.zeros((nh, 1, LANES), F32).at[:, 0, :4].set(bh)
    b_rows = jnp.zeros((nh, 8, 1), F32).at[:, :4, 0].set(bh)
    qo, ko, vo = col_q // dqk, col_k // dqk, col_v // dv

    def specs(cm):
        return [pl.BlockSpec((l, dqk), lambda h, t: (cm(t), qo + h)),
                pl.BlockSpec((l, dqk), lambda h, t: (cm(t), ko + h)),
                pl.BlockSpec((l, dv), lambda h, t: (cm(t), vo + h)),
                pl.BlockSpec((None, l, LANES), lambda h, t: (h, cm(t), 0)),
                pl.BlockSpec((None, 8, l), lambda h, t: (h, 0, cm(t)))]

    out_f = pl.BlockSpec((l, dv), lambda h, t: (fwd(t), h))
    out_b = pl.BlockSpec((l, dv), lambda h, t: (bwd(t), h))
    state = [pltpu.VMEM((dqk, dv), F32), pltpu.VMEM((1, dqk), F32), pltpu.VMEM((1, 1), F32)]
    return pl.pallas_call(
        functools.partial(_mlstm_kernel, scale=dqk ** -0.5),
        grid=(nh, nct + nlt),
        in_specs=specs(fwd) + specs(bwd) + [pl.BlockSpec((None, 1, LANES), lambda h, t: (h, 0, 0)),
                                             pl.BlockSpec((None, 8, 1), lambda h, t: (h, 0, 0))],
        out_specs=[out_f, out_b],
        out_shape=[jax.ShapeDtypeStruct((s, nh * dv), BF16)] * 2,
        scratch_shapes=state + state,
        compiler_params=_cparams("parallel", "arbitrary"), name="mlstm_scan",
    )(big, big, big, g_cols, g_rows, big, big, big, g_cols, g_rows, b_cols, b_rows)


def _conv_kernel(u_ref, prev_ref, next_ref, w_ref, b_ref, o_ref, *, nct, nt, width, halo):
    i = pl.program_id(0)
    first = jnp.logical_or(i == 0, i == nct)
    last = jnp.logical_or(i == nct - 1, i == nt - 1)
    r = u_ref.shape[0]
    prev = jnp.where(first, 0.0, prev_ref[...].astype(F32))
    nxt = jnp.where(last, 0.0, next_ref[...].astype(F32))
    ext = jnp.concatenate([prev, u_ref[...].astype(F32), nxt], axis=0)
    w = w_ref[...]
    acc = b_ref[...] + w[0:1, :] * ext[halo - width // 2: halo - width // 2 + r]
    for j in range(1, width):
        off = halo - width // 2 + j
        acc = acc + w[j:j + 1, :] * ext[off: off + r]
    o_ref[...] = _silu(acc).astype(o_ref.dtype)


def _dwconv_silu(u, conv_w, conv_b, n_ctx):
    s, ch = u.shape
    width = conv_w.shape[0]
    halo = 16
    r = _pick(math.gcd(n_ctx, s - n_ctx), ROW_TILE, halo)
    tc = _pick(ch, 1024, LANES)
    nt, nct, rb = s // r, n_ctx // r, r // halo
    w8 = jnp.zeros((8, ch), F32).at[:width].set(conv_w)
    return pl.pallas_call(
        functools.partial(_conv_kernel, nct=nct, nt=nt, width=width, halo=halo),
        grid=(nt, ch // tc),
        in_specs=[pl.BlockSpec((r, tc), lambda i, j: (i, j)),
                  pl.BlockSpec((halo, tc), lambda i, j: (jnp.maximum(i * rb - 1, 0), j)),
                  pl.BlockSpec((halo, tc), lambda i, j: (jnp.minimum((i + 1) * rb, s // halo - 1), j)),
                  pl.BlockSpec((8, tc), lambda i, j: (0, j)),
                  pl.BlockSpec((1, tc), lambda i, j: (0, j))],
        out_specs=pl.BlockSpec((r, tc), lambda i, j: (i, j)),
        out_shape=jax.ShapeDtypeStruct((s, ch), BF16),
        compiler_params=_cparams("parallel", "parallel"), name="dwconv_silu",
    )(u, u, u, w8, conv_b.reshape(1, ch))


def _per_channel(cols, off, hg, p, lane):
    out = jnp.broadcast_to(cols[:, off + hg - 1:off + hg], lane.shape)
    for h in range(hg - 2, -1, -1):
        out = jnp.where(lane < (h + 1) * p, jnp.broadcast_to(cols[:, off + h:off + h + 1], lane.shape), out)
    return out


def _ssd_direction(x, bm, cm, dtv, da, da_t, st_ref, *, fwd, hg, p):
    l = x.shape[0]
    off = 0 if fwd else hg
    low = _tri(l, True)
    up = _tri(l, False)
    acum_col = _dot_exact(low if fwd else up, da, True)
    acum_row = _dot_exact(up if fwd else low, da_t, False)
    lane = lax.broadcasted_iota(jnp.int32, x.shape, 1)
    acx = _per_channel(acum_col, off, hg, p, lane)
    xdt = x.astype(F32) * _per_channel(dtv, off, hg, p, lane)
    xdt_b = xdt.astype(BF16)
    causal = _tri(l, fwd)
    cb = lax.dot_general(cm, bm, (((1,), (1,)), ((), ())), preferred_element_type=F32)
    y = None
    for h in range(hg):
        seg = acum_col[:, off + h:off + h + 1] - acum_row[off + h:off + h + 1, :]
        mix = (cb * jnp.exp(jnp.where(causal, seg, -jnp.inf))).astype(BF16)
        own = jnp.logical_and(lane >= h * p, lane < (h + 1) * p)
        part = jnp.dot(mix, jnp.where(own, xdt_b, jnp.zeros_like(xdt_b)), preferred_element_type=F32)
        y = part if y is None else y + part
    st = st_ref[...]
    y = y + jnp.dot(cm, st.astype(BF16), preferred_element_type=F32) * jnp.exp(acx)
    total = acx[l - 1:l, :] if fwd else acx[0:1, :]
    xw = (xdt * jnp.exp(total - acx)).astype(BF16)
    st_ref[...] = jnp.exp(total) * st + lax.dot_general(bm, xw, (((0,), (0,)), ((), ())),
                                                        preferred_element_type=F32)
    return y


def _ssd_kernel(xf, bf, cf, dtf, dttf, xb, bb, cb, dtb, dttb, a_c, a_r, db_c, db_r, dsk,
                yf, yb, stf, stb, *, hg, p):
    @pl.when(pl.program_id(1) == 0)
    def _():
        stf[...] = jnp.zeros_like(stf)
        stb[...] = jnp.zeros_like(stb)

    def steps(dt_ref, dtt_ref):
        dtv = _softplus(dt_ref[...] + db_c[...])
        da_t = _softplus(dtt_ref[...] + db_r[...]) * a_r[...]
        return dtv, dtv * a_c[...], da_t

    x = xf[...]
    y = _ssd_direction(x, bf[...], cf[...], *steps(dtf, dttf), stf, fwd=True, hg=hg, p=p)
    yf[...] = (y + dsk[...] * x.astype(F32)).astype(yf.dtype)
    y = _ssd_direction(xb[...], bb[...], cb[...], *steps(dtb, dttb), stb, fwd=False, hg=hg, p=p)
    yb[...] = y.astype(yb.dtype)


def _ssd(xbc, dt_raw, a_log, dt_bias, d_skip, n_ctx, sw, n_state):
    s = xbc.shape[0]
    h = a_log.shape[1]
    g = SSM_GROUPS
    hg, p = h // g, sw // h
    gw = hg * p
    l = _pick(math.gcd(n_ctx, s - n_ctx), SCAN_CHUNK, 8)
    nct, nlt = n_ctx // l, (s - n_ctx) // l
    fwd, bwd = _chunk_maps(nct, nlt)
    dtg = dt_raw.reshape(s, 2, g, hg).transpose(2, 0, 1, 3).reshape(g, s, 2 * hg)
    dt_cols = jnp.zeros((g, s, LANES), F32).at[:, :, :2 * hg].set(dtg)
    assert 2 * hg <= 8 and sw % n_state == 0
    dt_rows = jnp.zeros((g, 8, s), F32).at[:, :2 * hg, :].set(dtg.transpose(0, 2, 1))
    a = (-jnp.exp(a_log.astype(F32))).reshape(2, g, hg).transpose(1, 0, 2).reshape(g, 2 * hg)
    db = dt_bias.reshape(2, g, hg).transpose(1, 0, 2).reshape(g, 2 * hg)
    a_c = jnp.zeros((g, 1, LANES), F32).at[:, 0, :2 * hg].set(a)
    a_r = jnp.zeros((g, 8, 1), F32).at[:, :2 * hg, 0].set(a)
    db_c = jnp.zeros((g, 1, LANES), F32).at[:, 0, :2 * hg].set(db)
    db_r = jnp.zeros((g, 8, 1), F32).at[:, :2 * hg, 0].set(db)
    dsk = jnp.repeat(d_skip, p).reshape(g, 1, gw)
    xo, bo, co = 0, sw // n_state, (sw + g * n_state) // n_state

    def specs(cm):
        return [pl.BlockSpec((l, gw), lambda gi, t: (cm(t), gi)),
                pl.BlockSpec((l, n_state), lambda gi, t: (cm(t), bo + gi)),
                pl.BlockSpec((l, n_state), lambda gi, t: (cm(t), co + gi)),
                pl.BlockSpec((None, l, LANES), lambda gi, t: (gi, cm(t), 0)),
                pl.BlockSpec((None, 8, l), lambda gi, t: (gi, 0, cm(t)))]

    par_c = pl.BlockSpec((None, 1, LANES), lambda gi, t: (gi, 0, 0))
    par_r = pl.BlockSpec((None, 8, 1), lambda gi, t: (gi, 0, 0))
    return pl.pallas_call(
        functools.partial(_ssd_kernel, hg=hg, p=p),
        grid=(g, nct + nlt),
        in_specs=specs(fwd) + specs(bwd) + [par_c, par_r, par_c, par_r,
                                             pl.BlockSpec((None, 1, gw), lambda gi, t: (gi, 0, 0))],
        out_specs=[pl.BlockSpec((l, gw), lambda gi, t: (fwd(t), gi)),
                   pl.BlockSpec((l, gw), lambda gi, t: (bwd(t), gi))],
        out_shape=[jax.ShapeDtypeStruct((s, sw), BF16)] * 2,
        scratch_shapes=[pltpu.VMEM((n_state, gw), F32)] * 2,
        compiler_params=_cparams("parallel", "arbitrary"), name="ssd_scan",
    )(xbc, xbc, xbc, dt_cols, dt_rows, xbc, xbc, xbc, dt_cols, dt_rows, a_c, a_r, db_c, db_r, dsk)


def _finish_kernel(hf, hb, o, yf, yb, z, mg, sg, out, *, nh, dv):
    mw = nh * dv
    h = hf[...].astype(F32) + hb[...].astype(F32)
    gate = _sigmoid(o[...].astype(F32))
    mgv = mg[...]
    for k in range(nh):
        hk = h[:, k * dv:(k + 1) * dv]
        r = lax.rsqrt(jnp.mean(hk * hk, axis=-1, keepdims=True) + EPS)
        out[:, k * dv:(k + 1) * dv] = (hk * r * mgv[:, k * dv:(k + 1) * dv]
                                       * gate[:, k * dv:(k + 1) * dv]).astype(out.dtype)
    y = (yf[...].astype(F32) + yb[...].astype(F32)) * _silu(z[...].astype(F32))
    r = lax.rsqrt(jnp.mean(y * y, axis=-1, keepdims=True) + EPS)
    out[:, mw:] = (y * r * sg[...]).astype(out.dtype)


def _finish(hf, hb, big, yf, yb, mlstm_g, ssm_g, nh, dv, col_o, col_z):
    s, mw = hf.shape
    sw = yf.shape[1]
    r = _pick(s, ROW_TILE, 8)
    return pl.pallas_call(
        functools.partial(_finish_kernel, nh=nh, dv=dv),
        grid=(s // r,),
        in_specs=[pl.BlockSpec((r, mw), lambda i: (i, 0)), pl.BlockSpec((r, mw), lambda i: (i, 0)),
                  pl.BlockSpec((r, mw), lambda i: (i, col_o // mw)),
                  pl.BlockSpec((r, sw), lambda i: (i, 0)), pl.BlockSpec((r, sw), lambda i: (i, 0)),
                  pl.BlockSpec((r, sw), lambda i: (i, col_z // sw)),
                  pl.BlockSpec((1, mw), lambda i: (0, 0)), pl.BlockSpec((1, sw), lambda i: (0, 0))],
        out_specs=pl.BlockSpec((r, mw + sw), lambda i: (i, 0)),
        out_shape=jax.ShapeDtypeStruct((s, mw + sw), BF16),
        compiler_params=_cparams("parallel"), name="mixer_finish",
    )(hf, hb, big, yf, yb, big, mlstm_g.reshape(1, mw), ssm_g.reshape(1, sw))


def _row_copy(src_ref, row, dst_ref, k, sem):
    return pltpu.make_async_copy(src_ref.at[pl.ds(row, 1)], dst_ref.at[pl.ds(k, 1)], sem)


def _gather_kernel(idx_ref, nxt_ref, src_ref, o_ref, buf, sem):
    i = pl.program_id(0)
    n_rows = o_ref.shape[0]
    slot = lax.rem(i, 2)

    def issue(ids_ref, s):
        def body(k, carry):
            _row_copy(src_ref, ids_ref[0, 0, k], buf.at[s], k, sem.at[s]).start()
            return carry
        lax.fori_loop(0, n_rows, body, 0)

    @pl.when(i == 0)
    def _():
        issue(idx_ref, 0)

    @pl.when(i + 1 < pl.num_programs(0))
    def _():
        issue(nxt_ref, 1 - slot)

    def wait(k, carry):
        _row_copy(src_ref, 0, buf.at[slot], k, sem.at[slot]).wait()
        return carry

    lax.fori_loop(0, n_rows, wait, 0)
    o_ref[...] = buf[slot].astype(o_ref.dtype)


def _gather_rows(src, idx, out_dtype):
    n = idx.shape[0]
    w = src.shape[1]
    r = _pick(n, GATHER_ROWS, 8)
    nt = n // r
    ids = idx.reshape(nt, 1, r)
    return pl.pallas_call(
        _gather_kernel, grid=(nt,),
        in_specs=[pl.BlockSpec((1, 1, r), lambda i: (i, 0, 0), memory_space=pltpu.SMEM),
                  pl.BlockSpec((1, 1, r), lambda i: (jnp.minimum(i + 1, nt - 1), 0, 0), memory_space=pltpu.SMEM),
                  pl.BlockSpec(memory_space=pl.ANY)],
        out_specs=pl.BlockSpec((r, w), lambda i: (i, 0)),
        out_shape=jax.ShapeDtypeStruct((n, w), out_dtype),
        scratch_shapes=[pltpu.VMEM((2, r, w), src.dtype), pltpu.SemaphoreType.DMA((2,))],
        compiler_params=_cparams("arbitrary"), name="moe_gather",
    )(ids, ids, src)


def _combine_kernel(p0_ref, p1_ref, q0_ref, q1_ref, yr_ref, x_ref, gk_ref, gate_ref, fg_ref, o_ref, buf, sem,
                    *, final_norm):
    i = pl.program_id(0)
    n_rows = x_ref.shape[0]
    slot = lax.rem(i, 2)

    def issue(a_ref, b_ref, s):
        def body(k, carry):
            _row_copy(yr_ref, a_ref[0, 0, k], buf.at[s, 0], k, sem.at[s]).start()
            _row_copy(yr_ref, b_ref[0, 0, k], buf.at[s, 1], k, sem.at[s]).start()
            return carry
        lax.fori_loop(0, n_rows, body, 0)

    @pl.when(i == 0)
    def _():
        issue(p0_ref, p1_ref, 0)

    @pl.when(i + 1 < pl.num_programs(0))
    def _():
        issue(q0_ref, q1_ref, 1 - slot)

    def wait(k, carry):
        _row_copy(yr_ref, 0, buf.at[slot, 0], k, sem.at[slot]).wait()
        _row_copy(yr_ref, 0, buf.at[slot, 1], k, sem.at[slot]).wait()
        return carry

    lax.fori_loop(0, n_rows, wait, 0)
    gk = gk_ref[...]
    x = x_ref[...] + gate_ref[...] * (gk[:, 0:1] * buf[slot, 0] + gk[:, 1:2] * buf[slot, 1])
    if final_norm:
        x = x * lax.rsqrt(jnp.mean(x * x, axis=-1, keepdims=True) + EPS) * fg_ref[...]
    o_ref[...] = x


def _moe_combine(xs, yr, pos, gates, gate_mod, n_ctx, final_g):
    s, d = xs.shape
    t = s - n_ctx
    r = _pick(math.gcd(n_ctx, t), COMBINE_ROWS, 8)
    nct, nt = n_ctx // r, t // r
    gk = jnp.zeros((t, LANES), F32).at[:, :TOP_K].set(gates)
    p0 = pos[:, 0].reshape(nt, 1, r)
    p1 = pos[:, 1].reshape(nt, 1, r)
    cur = pl.BlockSpec((1, 1, r), lambda i: (i, 0, 0), memory_space=pltpu.SMEM)
    nxt = pl.BlockSpec((1, 1, r), lambda i: (jnp.minimum(i + 1, nt - 1), 0, 0), memory_space=pltpu.SMEM)
    fg = jnp.ones((1, d), F32) if final_g is None else final_g.reshape(1, d).astype(F32)
    return pl.pallas_call(
        functools.partial(_combine_kernel, final_norm=final_g is not None), grid=(nt,),
        in_specs=[cur, cur, nxt, nxt,
                  pl.BlockSpec(memory_space=pl.ANY),
                  pl.BlockSpec((r, d), lambda i: (i + nct, 0)),
                  pl.BlockSpec((r, LANES), lambda i: (i, 0)),
                  pl.BlockSpec((1, d), lambda i: (0, 0)),
                  pl.BlockSpec((1, d), lambda i: (0, 0))],
        out_specs=pl.BlockSpec((r, d), lambda i: (i, 0)),
        out_shape=jax.ShapeDtypeStruct((t, d), F32),
        scratch_shapes=[pltpu.VMEM((2, 2, r, d), F32), pltpu.SemaphoreType.DMA((2,))],
        compiler_params=_cparams("arbitrary"), name="moe_combine",
    )(p0, p1, p0, p1, yr, xs, gk, gate_mod, fg)


def _moe_plan(logits, ne, tm):
    t = logits.shape[0]
    top_val, top_idx = lax.top_k(logits[:, :ne], TOP_K)
    gates = jax.nn.softmax(top_val, axis=-1)
    n_assign = t * TOP_K
    e_flat = top_idx.reshape(-1)
    onehot = (e_flat[:, None] == jnp.arange(ne)[None, :]).astype(jnp.int32)
    rank = jnp.take_along_axis(jnp.cumsum(onehot, axis=0) - onehot, e_flat[:, None], axis=1)[:, 0]
    counts = jnp.sum(onehot, axis=0)
    padded = (counts + tm - 1) // tm * tm
    pend = jnp.cumsum(padded)
    pstart = pend - padded
    pos = (pstart[e_flat] + rank).astype(jnp.int32)
    n_tiles = (n_assign + ne * (tm - 1)) // tm
    n_rows = n_tiles * tm
    tok = jnp.repeat(jnp.arange(t, dtype=jnp.int32), TOP_K)
    row_tok = jnp.zeros((n_rows,), jnp.int32).at[pos].set(tok)
    tile_start = jnp.arange(n_tiles, dtype=jnp.int32) * tm
    tile_expert = jnp.minimum(jnp.searchsorted(pend, tile_start, side='right'), ne - 1).astype(jnp.int32)
    n_used = (pend[-1] // tm).astype(jnp.int32)
    return gates, pos.reshape(t, TOP_K), row_tok, tile_expert, n_used


def _final_norm_kernel(x_ref, g_ref, o_ref):
    x = x_ref[...]
    o_ref[...] = x * lax.rsqrt(jnp.mean(x * x, axis=-1, keepdims=True) + EPS) * g_ref[...]


def _final_norm(xs, g, n_ctx):
    s, d = xs.shape
    t = s - n_ctx
    r = _pick(math.gcd(n_ctx, t), ROW_TILE, 8)
    nct = n_ctx // r
    return pl.pallas_call(
        _final_norm_kernel, grid=(t // r,),
        in_specs=[pl.BlockSpec((r, d), lambda i: (i + nct, 0)), pl.BlockSpec((1, d), lambda i: (0, 0))],
        out_specs=pl.BlockSpec((r, d), lambda i: (i, 0)),
        out_shape=jax.ShapeDtypeStruct((t, d), F32),
        compiler_params=_cparams("parallel"), name="final_rmsnorm",
    )(xs, g.reshape(1, d))


def _to_colmajor(a, n_ctx):
    t, ch = a.shape[0] - n_ctx, a.shape[1]
    lat = a[n_ctx:].reshape(t // GRID_W, GRID_W, ch).transpose(1, 0, 2).reshape(t, ch)
    return jnp.concatenate([a[:n_ctx], lat], axis=0)


def _from_colmajor(a, n_ctx):
    t, ch = a.shape[0] - n_ctx, a.shape[1]
    lat = a[n_ctx:].reshape(GRID_W, t // GRID_W, ch).transpose(1, 0, 2).reshape(t, ch)
    return jnp.concatenate([a[:n_ctx], lat], axis=0)


def kernel(x, c, ctx, c_ctx, w_mod, b_mod, norm1_g, norm2_g, w_in, gate_b, conv_w, conv_b, a_log, dt_bias, d_skip, mlstm_g, ssm_g, w_out, ffn_w1, ffn_w3, ffn_w2, router_w, moe_w1, moe_w3, moe_w2, final_g):
    bsz, t, d = x.shape
    assert bsz == 1, "the kernels treat the single batch element's tokens as rows"
    n_ctx = ctx.shape[1]
    depth = w_mod.shape[0]
    nh = gate_b.shape[1] // 4
    mw = mlstm_g.shape[1]
    dv = mw // nh
    dqk = dv // 2
    sw = ssm_g.shape[1]
    heads = a_log.shape[2]
    conv_ch = conv_w.shape[2]
    n_state = (conv_ch - sw) // (2 * SSM_GROUPS)
    wq = nh * dqk
    splits = (wq, wq, mw, mw, 4 * nh, sw, conv_ch, 2 * heads)
    offs = [0]
    for w_ in splits:
        offs.append(offs[-1] + w_)
    assert offs[-1] == w_in.shape[2]
    ne = router_w.shape[2]

    xs = jnp.concatenate([ctx[0], x[0]], axis=0).astype(F32)
    mod = _modulation(c_ctx.astype(F32), c[0].astype(F32), w_mod.astype(F32), b_mod.astype(F32))

    col_q, col_k, col_v, col_o, col_z, col_x = 0, wq, 2 * wq, 2 * wq + mw, 2 * wq + 2 * mw, 2 * wq + 2 * mw + sw
    n_small = -(-(4 * nh + 2 * heads) // LANES) * LANES

    for layer in range(depth):
        need_ctx = layer < depth - 1
        m6 = mod[layer, :2].reshape(2, 6, d)
        shift1, scale1, gate1, shift2, scale2, gate2 = (m6[:, i][:, None, :] for i in range(6))

        wl = w_in[layer]
        w_big = jnp.concatenate([wl[:, offs[0]:offs[4]], wl[:, offs[5]:offs[7]]], axis=1).astype(BF16)
        w_small = jnp.zeros((d, n_small), BF16)
        w_small = w_small.at[:, :4 * nh].set(wl[:, offs[4]:offs[5]].astype(BF16))
        w_small = w_small.at[:, 4 * nh:4 * nh + 2 * heads].set(wl[:, offs[7]:offs[8]].astype(BF16))
        xn = _norm_mod(xs, norm1_g[layer], scale1, shift1, n_ctx)
        big = _matmul(xn, w_big, BF16)
        small = _matmul(xn, w_small, F32)

        hf, hb = _mlstm(big, small[:, :4 * nh], gate_b[layer], nh, dqk, dv, n_ctx, col_q, col_k, col_v)

        xbc = _dwconv_silu(_to_colmajor(big[:, col_x:], n_ctx), conv_w[layer], conv_b[layer], n_ctx)
        dt_raw = _to_colmajor(small[:, 4 * nh:4 * nh + 2 * heads], n_ctx)
        yf, yb = _ssd(xbc, dt_raw, a_log[layer], dt_bias[layer], d_skip[layer], n_ctx, sw, n_state)
        yf, yb = _from_colmajor(yf, n_ctx), _from_colmajor(yb, n_ctx)

        mix = _finish(hf, hb, big, yf, yb, mlstm_g[layer], ssm_g[layer], nh, dv, col_o, col_z)
        xs = _matmul_residual(mix, w_out[layer].astype(BF16), xs, gate1, n_ctx, d)

        i = layer // 2
        if layer % 2 == 0:
            hn = _norm_mod(xs, norm2_g[layer], scale2, shift2, n_ctx)
            tm = _pick(hn.shape[0], FFN_TM, 8)
            nt = hn.shape[0] // tm
            act = _glu(hn, ffn_w1[i:i + 1], ffn_w3[i:i + 1], tm, jnp.zeros((nt,), jnp.int32), nt)
            f, fp = ffn_w2.shape[1], act.shape[1]
            w2 = jnp.concatenate([ffn_w2[i].astype(BF16), jnp.zeros((fp - f, d), BF16)], axis=0)
            xs = _matmul_residual(act, w2, xs, gate2, n_ctx, fp // 4)
        else:
            assert not need_ctx, "an MoE layer that still feeds context is not needed by this block"
            hn, logits = _norm_mod(xs, norm2_g[layer], scale2, shift2, n_ctx, router_w=router_w[i])
            tm = min(MOE_TM, t)
            gates, pos, row_tok, tile_expert, n_used = _moe_plan(logits[n_ctx:], ne, tm)
            xr = _gather_rows(hn, row_tok + n_ctx, BF16)
            act = _glu(xr, moe_w1[i], moe_w3[i], tm, tile_expert, n_used)
            yr = _matmul_expert(act, moe_w2[i], tm, tile_expert, n_used, F32)
            return _moe_combine(xs, yr, pos, gates, gate2[1], n_ctx, final_g)[None]

    return _final_norm(xs, final_g, n_ctx)[None]
```

```python
import functools
import math

import jax
import jax.numpy as jnp
from jax import lax
from jax.experimental import pallas as pl
from jax.experimental.pallas import tpu as pltpu

GRID_W = 64
SSM_GROUPS = 8
TOP_K = 2
EPS = 1e-6

V7X_VMEM_LIMIT_BYTES = 56 * 1024 * 1024
V7X_VMEM_LIMIT_BIG_BYTES = 60 * 1024 * 1024
LANES = 128
SCAN_CHUNK = 256
ROW_TILE = 256
MOD_TK = 128
MM_TM = 768
MM_TN = 1024
GLU_TN = 512
FFN_TM = 528
DOWN_TN = 1024
MOE_TM = 512
GATHER_ROWS = 256
COMBINE_ROWS = 128

F32 = jnp.float32
BF16 = jnp.bfloat16
HIGHEST = lax.Precision.HIGHEST


def _pick(dim, pref, mult):
    if dim <= pref:
        return dim
    t = (pref // mult) * mult
    while t >= mult:
        if dim % t == 0:
            return t
        t -= mult
    return dim


def _cparams(*sem, vmem=V7X_VMEM_LIMIT_BYTES):
    return pltpu.CompilerParams(dimension_semantics=sem, vmem_limit_bytes=vmem)


def _sigmoid(v):
    return 1.0 / (1.0 + jnp.exp(-v))


def _silu(v):
    return v * _sigmoid(v)


def _softplus(v):
    return jnp.maximum(v, 0.0) + jnp.log1p(jnp.exp(-jnp.abs(v)))


def _log_sigmoid(v):
    return -_softplus(-v)


def _split3(v):
    hi = v.astype(BF16)
    r = v - hi.astype(F32)
    mid = r.astype(BF16)
    lo = (r - mid.astype(F32)).astype(BF16)
    return hi, mid, lo


def _dot_exact(mat01, v, left):
    m = mat01.astype(BF16)
    out = None
    for piece in _split3(v):
        part = (jnp.dot(m, piece, preferred_element_type=F32) if left
                else jnp.dot(piece, m, preferred_element_type=F32))
        out = part if out is None else out + part
    return out


def _mod_kernel(c_ref, w_ref, b_ref, o_ref, acc, *, nk):
    k = pl.program_id(1)

    @pl.when(k == 0)
    def _():
        acc[...] = jnp.zeros_like(acc)

    s_ctx = _silu(c_ref[0])
    s_lat = _silu(c_ref[1])
    tk = w_ref.shape[0]
    for n0 in range(0, w_ref.shape[1], LANES):
        w = w_ref[:, n0:n0 + LANES]
        acc[0, :, n0:n0 + LANES] += jnp.sum((w * s_ctx).reshape(tk // 8, 8, LANES), axis=0)
        acc[1, :, n0:n0 + LANES] += jnp.sum((w * s_lat).reshape(tk // 8, 8, LANES), axis=0)

    @pl.when(k == nk - 1)
    def _():
        o_ref[...] = jnp.zeros_like(o_ref)
        o_ref[0:1, :] = jnp.sum(acc[0], axis=0, keepdims=True) + b_ref[...]
        o_ref[1:2, :] = jnp.sum(acc[1], axis=0, keepdims=True) + b_ref[...]


def _modulation(c_ctx, c_lat, w_mod, b_mod):
    depth, d, n = w_mod.shape
    tk = _pick(d, MOD_TK, 8)
    nk = d // tk
    cb = jnp.broadcast_to(jnp.stack([c_ctx, c_lat])[:, :, None], (2, d, LANES))
    return pl.pallas_call(
        functools.partial(_mod_kernel, nk=nk),
        grid=(depth, nk),
        in_specs=[pl.BlockSpec((2, tk, LANES), lambda l, k: (0, k, 0)),
                  pl.BlockSpec((None, tk, n), lambda l, k: (l, k, 0)),
                  pl.BlockSpec((None, 1, n), lambda l, k: (l, 0, 0))],
        out_specs=pl.BlockSpec((None, 8, n), lambda l, k: (l, 0, 0)),
        out_shape=jax.ShapeDtypeStruct((depth, 8, n), F32),
        scratch_shapes=[pltpu.VMEM((2, 8, n), F32)],
        compiler_params=_cparams("parallel", "arbitrary"),
        name="adaln_modulation",
    )(cb, w_mod, b_mod.reshape(depth, 1, n))


def _norm_kernel(x_ref, g_ref, sc_ref, sh_ref, o_ref):
    x = x_ref[...]
    r = lax.rsqrt(jnp.mean(x * x, axis=-1, keepdims=True) + EPS)
    o_ref[...] = (x * r * g_ref[...] * (1.0 + sc_ref[...]) + sh_ref[...]).astype(o_ref.dtype)


def _norm_router_kernel(x_ref, g_ref, sc_ref, sh_ref, rw_ref, o_ref, lg_ref):
    x = x_ref[...]
    r = lax.rsqrt(jnp.mean(x * x, axis=-1, keepdims=True) + EPS)
    h = x * r * g_ref[...] * (1.0 + sc_ref[...]) + sh_ref[...]
    n_rows, d = h.shape
    rpt = d // LANES
    for j in range(rpt):
        o_ref[pl.ds(j, n_rows, stride=rpt), :] = h[:, j * LANES:(j + 1) * LANES]
    lg_ref[...] = jnp.dot(h, rw_ref[...], preferred_element_type=F32, precision=HIGHEST)


def _norm_mod(xs, g, scale, shift, n_ctx, router_w=None):
    s, d = xs.shape
    r = _pick(math.gcd(n_ctx, s - n_ctx), ROW_TILE, 8)
    nct = n_ctx // r
    row_spec = pl.BlockSpec((r, d), lambda i: (i, 0))
    mod_spec = pl.BlockSpec((None, 1, d), lambda i: (jnp.where(i < nct, 0, 1), 0, 0))
    in_specs = [row_spec, pl.BlockSpec((1, d), lambda i: (0, 0)), mod_spec, mod_spec]
    if router_w is None:
        return pl.pallas_call(
            _norm_kernel, grid=(s // r,), in_specs=in_specs, out_specs=row_spec,
            out_shape=jax.ShapeDtypeStruct((s, d), BF16),
            compiler_params=_cparams("parallel"), name="rmsnorm_modulate",
        )(xs, g.reshape(1, d), scale, shift)
    ne = router_w.shape[1]
    rw = jnp.zeros((d, LANES), F32).at[:, :ne].set(router_w)
    return pl.pallas_call(
        _norm_router_kernel, grid=(s // r,),
        in_specs=in_specs + [pl.BlockSpec((d, LANES), lambda i: (0, 0))],
        out_specs=[pl.BlockSpec((r * (d // LANES), LANES), lambda i: (i, 0)),
                   pl.BlockSpec((r, LANES), lambda i: (i, 0))],
        out_shape=[jax.ShapeDtypeStruct((s * (d // LANES), LANES), F32), jax.ShapeDtypeStruct((s, LANES), F32)],
        compiler_params=_cparams("parallel"), name="rmsnorm_modulate_router",
    )(xs, g.reshape(1, d), scale, shift, rw)


def _mm_kernel(a_ref, w_ref, o_ref):
    o_ref[...] = jnp.dot(a_ref[...], w_ref[...], preferred_element_type=F32).astype(o_ref.dtype)


def _matmul(a, w, out_dtype):
    m, k = a.shape
    n = w.shape[1]
    tm = _pick(m, MM_TM, 8)
    tn = _pick(n, MM_TN, LANES)
    return pl.pallas_call(
        _mm_kernel, grid=(m // tm, n // tn),
        in_specs=[pl.BlockSpec((tm, k), lambda i, j: (i, 0)), pl.BlockSpec((k, tn), lambda i, j: (0, j))],
        out_specs=pl.BlockSpec((tm, tn), lambda i, j: (i, j)),
        out_shape=jax.ShapeDtypeStruct((m, n), out_dtype),
        compiler_params=_cparams("parallel", "parallel"), name="matmul",
    )(a, w)


def _mm_res_kernel(a_ref, w_ref, res_ref, gate_ref, o_ref, acc_ref, *, nk, tm, n_ctx):
    k = pl.program_id(2)

    @pl.when(k == 0)
    def _():
        acc_ref[...] = jnp.zeros_like(acc_ref)

    acc_ref[...] += jnp.dot(a_ref[...], w_ref[...], preferred_element_type=F32)

    @pl.when(k == nk - 1)
    def _():
        rows = pl.program_id(0) * tm + lax.broadcasted_iota(jnp.int32, (tm, 1), 0)
        gate = jnp.where(rows < n_ctx, gate_ref[0], gate_ref[1])
        o_ref[...] = res_ref[...] + gate * acc_ref[...]


def _matmul_residual(a, w, res, gate, n_ctx, tk_pref):
    m, kdim = a.shape
    n = w.shape[1]
    tm = _pick(m, MM_TM, 8)
    tn = _pick(n, MM_TN, LANES)
    tk = _pick(kdim, tk_pref, LANES)
    nk = kdim // tk
    return pl.pallas_call(
        functools.partial(_mm_res_kernel, nk=nk, tm=tm, n_ctx=n_ctx),
        grid=(m // tm, n // tn, nk),
        in_specs=[pl.BlockSpec((tm, tk), lambda i, j, k: (i, k)),
                  pl.BlockSpec((tk, tn), lambda i, j, k: (k, j)),
                  pl.BlockSpec((tm, tn), lambda i, j, k: (i, j)),
                  pl.BlockSpec((2, 1, tn), lambda i, j, k: (0, 0, j))],
        out_specs=pl.BlockSpec((tm, tn), lambda i, j, k: (i, j)),
        out_shape=jax.ShapeDtypeStruct((m, n), F32),
        scratch_shapes=[pltpu.VMEM((tm, tn), F32)],
        compiler_params=_cparams("parallel", "parallel", "arbitrary"), name="matmul_residual",
    )(a, w, res, gate)


def _stationary_plan(tile_expert, n_used, n_experts, nj):
    n_tiles = tile_expert.shape[0]
    tile_ids = jnp.arange(n_tiles, dtype=jnp.int32)
    cnt = jnp.sum(((tile_expert[:, None] == jnp.arange(n_experts)[None, :])
                   & (tile_ids[:, None] < n_used)).astype(jnp.int32), axis=0)
    seg_end = jnp.cumsum(cnt)
    seg_start = seg_end - cnt
    step = jnp.arange(n_tiles * nj, dtype=jnp.int32)
    used_steps = n_used * nj
    e = jnp.minimum(jnp.searchsorted(seg_end * nj, step, side='right'), n_experts - 1).astype(jnp.int32)
    seg_len = jnp.maximum(cnt[e], 1)
    local = jnp.maximum(step - nj * seg_start[e], 0)
    j = local // seg_len
    i = seg_start[e] + local % seg_len
    used = step < used_steps
    last = used_steps - 1
    fill = step - used_steps
    pick = lambda arr: jnp.where(used, arr, arr[last]).astype(jnp.int32)
    out_i = jnp.where(used, i, n_used + fill // nj).astype(jnp.int32)
    out_j = jnp.where(used, j, fill % nj).astype(jnp.int32)
    return pick(e), pick(j), pick(i), out_i, out_j, used.astype(jnp.int32)


def _stationary_flags(e_ref, jw_ref, used_ref):
    s = pl.program_id(0)
    prev = jnp.maximum(s - 1, 0)
    fresh = jnp.logical_or(s == 0, jnp.logical_or(e_ref[s] != e_ref[prev], jw_ref[s] != jw_ref[prev]))
    used = used_ref[s] > 0
    return used, jnp.logical_and(used, fresh)


def _glu_kernel(e_ref, jw_ref, ia_ref, io_ref, jo_ref, used_ref, a_ref, w1_ref, w3_ref, o_ref, w1b, w3b,
                *, n_valid, ragged):
    used, fresh = _stationary_flags(e_ref, jw_ref, used_ref)

    @pl.when(fresh)
    def _():
        w1 = w1_ref[...]
        w3 = w3_ref[...]
        if ragged:
            col = jw_ref[pl.program_id(0)] * w1.shape[1] + lax.broadcasted_iota(jnp.int32, w1.shape, 1)
            w1 = jnp.where(col < n_valid, w1, 0.0)
            w3 = jnp.where(col < n_valid, w3, 0.0)
        w1b[...] = w1.astype(BF16)
        w3b[...] = w3.astype(BF16)

    @pl.when(used)
    def _():
        a = a_ref[...]
        h1 = jnp.dot(a, w1b[...], preferred_element_type=F32)
        h3 = jnp.dot(a, w3b[...], preferred_element_type=F32)
        o_ref[...] = (_silu(h1) * h3).astype(o_ref.dtype)

    @pl.when(jnp.logical_not(used))
    def _():
        o_ref[...] = jnp.zeros_like(o_ref)


def _stationary_specs(tm, k, tn):
    a_spec = pl.BlockSpec((tm, k), lambda s, e, jw, ia, io, jo, u: (ia[s], 0))
    w_spec = pl.BlockSpec((None, k, tn), lambda s, e, jw, ia, io, jo, u: (e[s], 0, jw[s]))
    o_spec = pl.BlockSpec((tm, tn), lambda s, e, jw, ia, io, jo, u: (io[s], jo[s]))
    return a_spec, w_spec, o_spec


def _glu(a, w1, w3, tm, tile_expert, n_used):
    m, k = a.shape
    ne, _, n = w1.shape
    tn = GLU_TN if n % LANES == 0 and n > GLU_TN else n
    nj = pl.cdiv(n, tn)
    plan = _stationary_plan(tile_expert, n_used, ne, nj)
    a_spec, w_spec, o_spec = _stationary_specs(tm, k, tn)
    return pl.pallas_call(
        functools.partial(_glu_kernel, n_valid=n, ragged=n % tn != 0),
        grid_spec=pltpu.PrefetchScalarGridSpec(
            num_scalar_prefetch=6, grid=(plan[0].shape[0],),
            in_specs=[a_spec, w_spec, w_spec], out_specs=o_spec,
            scratch_shapes=[pltpu.VMEM((k, tn), BF16)] * 2),
        out_shape=jax.ShapeDtypeStruct((m, nj * tn), BF16),
        compiler_params=_cparams("arbitrary", vmem=V7X_VMEM_LIMIT_BIG_BYTES), name="swiglu_up",
    )(*plan, a, w1, w3)


def _mm_expert_kernel(e_ref, jw_ref, ia_ref, io_ref, jo_ref, used_ref, a_ref, w_ref, o_ref, wb):
    used, fresh = _stationary_flags(e_ref, jw_ref, used_ref)

    @pl.when(fresh)
    def _():
        wb[...] = w_ref[...].astype(BF16)

    @pl.when(used)
    def _():
        res = jnp.dot(a_ref[...], wb[...], preferred_element_type=F32)
        for s in range(res.shape[1] // LANES):
            o_ref[:, s, :] = res[:, s * LANES:(s + 1) * LANES]

    @pl.when(jnp.logical_not(used))
    def _():
        o_ref[...] = jnp.zeros_like(o_ref)


def _matmul_expert(a, w, tm, tile_expert, n_used):
    m, k = a.shape
    ne, _, n = w.shape
    tn = _pick(n, DOWN_TN, 8 * LANES)
    assert tn % LANES == 0 and n % tn == 0
    nj = n // tn
    plan = _stationary_plan(tile_expert, n_used, ne, nj)
    a_spec, w_spec, _ = _stationary_specs(tm, k, tn)
    o_spec = pl.BlockSpec((tm, tn // LANES, LANES), lambda s, e, jw, ia, io, jo, u: (io[s], jo[s], 0))
    return pl.pallas_call(
        _mm_expert_kernel,
        grid_spec=pltpu.PrefetchScalarGridSpec(
            num_scalar_prefetch=6, grid=(plan[0].shape[0],),
            in_specs=[a_spec, w_spec], out_specs=o_spec,
            scratch_shapes=[pltpu.VMEM((k, tn), BF16)]),
        out_shape=jax.ShapeDtypeStruct((m, n // LANES, LANES), F32),
        compiler_params=_cparams("arbitrary", vmem=V7X_VMEM_LIMIT_BIG_BYTES), name="expert_down",
    )(*plan, a, w)


def _chunk_maps(nct, nlt):
    def fwd(s):
        return s

    def bwd(s):
        return jnp.where(s < nct, nct - 1 - s, 2 * nct + nlt - 1 - s)

    return fwd, bwd


def _tri(l, lower):
    r = lax.broadcasted_iota(jnp.int32, (l, l), 0)
    c = lax.broadcasted_iota(jnp.int32, (l, l), 1)
    return (r >= c) if lower else (r <= c)


def _mlstm_direction(q, k, v, g, gt, c_ref, n_ref, m_ref, o_ref, *, fwd, scale):
    l = q.shape[0]
    ci, cf = (0, 2) if fwd else (1, 3)
    low = _tri(l, True).astype(F32)
    up = _tri(l, False).astype(F32)
    b_col = _dot_exact(low if fwd else up, _log_sigmoid(g), True)[:, cf:cf + 1]
    b_row = _dot_exact(up if fwd else low, _log_sigmoid(gt), False)[cf:cf + 1, :]
    i_col = g[:, ci:ci + 1]
    i_row = gt[ci:ci + 1, :]
    causal = _tri(l, fwd)
    logw = jnp.where(causal, b_col - b_row + i_row, -jnp.inf)
    m_intra = jnp.max(logw, axis=-1, keepdims=True)
    m_st = m_ref[...]
    inter_log = b_col + m_st
    m_t = jnp.maximum(inter_log, m_intra)
    a_inter = jnp.exp(inter_log - m_t)
    ks = (k.astype(F32) * scale)
    ksb = ks.astype(BF16)
    qk = lax.dot_general(q, ksb, (((1,), (1,)), ((), ())), preferred_element_type=F32)
    s = qk * jnp.exp(logw - m_t)
    c_st = c_ref[...]
    n_st = n_ref[...]
    num = a_inter * jnp.dot(q, c_st.astype(BF16), preferred_element_type=F32) \
        + jnp.dot(s.astype(BF16), v, preferred_element_type=F32)
    den = a_inter * jnp.sum(q.astype(F32) * n_st, axis=-1, keepdims=True) + jnp.sum(s, axis=-1, keepdims=True)
    o_ref[...] = (num / jnp.maximum(jnp.abs(den), jnp.exp(-m_t))).astype(o_ref.dtype)
    b_last = b_col[l - 1:l, :] if fwd else b_col[0:1, :]
    g_col = b_last - b_col + i_col
    m_new = jnp.maximum(b_last + m_st, jnp.max(g_col, axis=0, keepdims=True))
    kw = ks * jnp.exp(g_col - m_new)
    dec = jnp.exp(b_last + m_st - m_new)
    c_ref[...] = dec * c_st + lax.dot_general(kw.astype(BF16), v, (((0,), (0,)), ((), ())),
                                              preferred_element_type=F32)
    n_ref[...] = dec * n_st + jnp.sum(kw, axis=0, keepdims=True)
    m_ref[...] = m_new


def _mlstm_kernel(qf, kf, vf, gf, gtf, qb, kb, vb, gb, gtb, bias, bias_t, hf, hb,
                  cf, nf, mf, cb, nb, mb, *, scale):
    @pl.when(pl.program_id(1) == 0)
    def _():
        for ref in (cf, nf, mf, cb, nb, mb):
            ref[...] = jnp.zeros_like(ref)

    _mlstm_direction(qf[...], kf[...], vf[...], gf[...] + bias[...], gtf[...] + bias_t[...],
                     cf, nf, mf, hf, fwd=True, scale=scale)
    _mlstm_direction(qb[...], kb[...], vb[...], gb[...] + bias[...], gtb[...] + bias_t[...],
                     cb, nb, mb, hb, fwd=False, scale=scale)


def _mlstm(big, gates, gate_b, nh, dqk, dv, n_ctx, col_q, col_k, col_v):
    s = big.shape[0]
    l = _pick(math.gcd(n_ctx, s - n_ctx), SCAN_CHUNK, 8)
    nct, nlt = n_ctx // l, (s - n_ctx) // l
    fwd, bwd = _chunk_maps(nct, nlt)
    gh = gates.reshape(s, 4, nh).transpose(2, 0, 1)
    g_cols = jnp.zeros((nh, s, LANES), F32).at[:, :, :4].set(gh)
    g_rows = jnp.zeros((nh, 8, s), F32).at[:, :4, :].set(gh.transpose(0, 2, 1))
    bh = gate_b.reshape(4, nh).T
    b_cols = jnp.zeros((nh, 1, LANES), F32).at[:, 0, :4].set(bh)
    b_rows = jnp.zeros((nh, 8, 1), F32).at[:, :4, 0].set(bh)
    qo, ko, vo = col_q // dqk, col_k // dqk, col_v // dv

    def specs(cm):
        return [pl.BlockSpec((l, dqk), lambda h, t: (cm(t), qo + h)),
                pl.BlockSpec((l, dqk), lambda h, t: (cm(t), ko + h)),
                pl.BlockSpec((l, dv), lambda h, t: (cm(t), vo + h)),
                pl.BlockSpec((None, l, LANES), lambda h, t: (h, cm(t), 0)),
                pl.BlockSpec((None, 8, l), lambda h, t: (h, 0, cm(t)))]

    out_f = pl.BlockSpec((l, dv), lambda h, t: (fwd(t), h))
    out_b = pl.BlockSpec((l, dv), lambda h, t: (bwd(t), h))
    state = [pltpu.VMEM((dqk, dv), F32), pltpu.VMEM((1, dqk), F32), pltpu.VMEM((1, 1), F32)]
    return pl.pallas_call(
        functools.partial(_mlstm_kernel, scale=dqk ** -0.5),
        grid=(nh, nct + nlt),
        in_specs=specs(fwd) + specs(bwd) + [pl.BlockSpec((None, 1, LANES), lambda h, t: (h, 0, 0)),
                                             pl.BlockSpec((None, 8, 1), lambda h, t: (h, 0, 0))],
        out_specs=[out_f, out_b],
        out_shape=[jax.ShapeDtypeStruct((s, nh * dv), BF16)] * 2,
        scratch_shapes=state + state,
        compiler_params=_cparams("parallel", "arbitrary"), name="mlstm_scan",
    )(big, big, big, g_cols, g_rows, big, big, big, g_cols, g_rows, b_cols, b_rows)


def _conv_kernel(u_ref, prev_ref, next_ref, w_ref, b_ref, o_ref, *, nct, nt, width, halo):
    i = pl.program_id(0)
    first = jnp.logical_or(i == 0, i == nct)
    last = jnp.logical_or(i == nct - 1, i == nt - 1)
    r = u_ref.shape[0]
    prev = jnp.where(first, 0.0, prev_ref[...].astype(F32))
    nxt = jnp.where(last, 0.0, next_ref[...].astype(F32))
    ext = jnp.concatenate([prev, u_ref[...].astype(F32), nxt], axis=0)
    w = w_ref[...]
    acc = b_ref[...] + w[0:1, :] * ext[halo - width // 2: halo - width // 2 + r]
    for j in range(1, width):
        off = halo - width // 2 + j
        acc = acc + w[j:j + 1, :] * ext[off: off + r]
    o_ref[...] = _silu(acc).astype(o_ref.dtype)


def _dwconv_silu(u, conv_w, conv_b, n_ctx):
    s, ch = u.shape
    width = conv_w.shape[0]
    halo = 16
    r = _pick(math.gcd(n_ctx, s - n_ctx), ROW_TILE, halo)
    tc = _pick(ch, 1024, LANES)
    nt, nct, rb = s // r, n_ctx // r, r // halo
    w8 = jnp.zeros((8, ch), F32).at[:width].set(conv_w)
    return pl.pallas_call(
        functools.partial(_conv_kernel, nct=nct, nt=nt, width=width, halo=halo),
        grid=(nt, ch // tc),
        in_specs=[pl.BlockSpec((r, tc), lambda i, j: (i, j)),
                  pl.BlockSpec((halo, tc), lambda i, j: (jnp.maximum(i * rb - 1, 0), j)),
                  pl.BlockSpec((halo, tc), lambda i, j: (jnp.minimum((i + 1) * rb, s // halo - 1), j)),
                  pl.BlockSpec((8, tc), lambda i, j: (0, j)),
                  pl.BlockSpec((1, tc), lambda i, j: (0, j))],
        out_specs=pl.BlockSpec((r, tc), lambda i, j: (i, j)),
        out_shape=jax.ShapeDtypeStruct((s, ch), BF16),
        compiler_params=_cparams("parallel", "parallel"), name="dwconv_silu",
    )(u, u, u, w8, conv_b.reshape(1, ch))


def _per_channel(cols, off, hg, p, lane):
    out = jnp.broadcast_to(cols[:, off + hg - 1:off + hg], lane.shape)
    for h in range(hg - 2, -1, -1):
        out = jnp.where(lane < (h + 1) * p, jnp.broadcast_to(cols[:, off + h:off + h + 1], lane.shape), out)
    return out


def _ssd_direction(x, bm, cm, dtv, da, da_t, st_ref, *, fwd, hg, p):
    l = x.shape[0]
    off = 0 if fwd else hg
    low = _tri(l, True)
    up = _tri(l, False)
    acum_col = _dot_exact(low if fwd else up, da, True)
    acum_row = _dot_exact(up if fwd else low, da_t, False)
    lane = lax.broadcasted_iota(jnp.int32, x.shape, 1)
    acx = _per_channel(acum_col, off, hg, p, lane)
    xdt = x.astype(F32) * _per_channel(dtv, off, hg, p, lane)
    xdt_b = xdt.astype(BF16)
    causal = _tri(l, fwd)
    cb = lax.dot_general(cm, bm, (((1,), (1,)), ((), ())), preferred_element_type=F32)
    y = None
    for h in range(hg):
        seg = acum_col[:, off + h:off + h + 1] - acum_row[off + h:off + h + 1, :]
        mix = (cb * jnp.exp(jnp.where(causal, seg, -jnp.inf))).astype(BF16)
        own = jnp.logical_and(lane >= h * p, lane < (h + 1) * p)
        part = jnp.dot(mix, jnp.where(own, xdt_b, jnp.zeros_like(xdt_b)), preferred_element_type=F32)
        y = part if y is None else y + part
    st = st_ref[...]
    y = y + jnp.dot(cm, st.astype(BF16), preferred_element_type=F32) * jnp.exp(acx)
    total = acx[l - 1:l, :] if fwd else acx[0:1, :]
    xw = (xdt * jnp.exp(total - acx)).astype(BF16)
    st_ref[...] = jnp.exp(total) * st + lax.dot_general(bm, xw, (((0,), (0,)), ((), ())),
                                                        preferred_element_type=F32)
    return y


def _ssd_kernel(xf, bf, cf, dtf, dttf, xb, bb, cb, dtb, dttb, a_c, a_r, db_c, db_r, dsk,
                yf, yb, stf, stb, *, hg, p):
    @pl.when(pl.program_id(1) == 0)
    def _():
        stf[...] = jnp.zeros_like(stf)
        stb[...] = jnp.zeros_like(stb)

    def steps(dt_ref, dtt_ref):
        dtv = _softplus(dt_ref[...] + db_c[...])
        da_t = _softplus(dtt_ref[...] + db_r[...]) * a_r[...]
        return dtv, dtv * a_c[...], da_t

    x = xf[...]
    y = _ssd_direction(x, bf[...], cf[...], *steps(dtf, dttf), stf, fwd=True, hg=hg, p=p)
    yf[...] = (y + dsk[...] * x.astype(F32)).astype(yf.dtype)
    y = _ssd_direction(xb[...], bb[...], cb[...], *steps(dtb, dttb), stb, fwd=False, hg=hg, p=p)
    yb[...] = y.astype(yb.dtype)


def _ssd(xbc, dt_raw, a_log, dt_bias, d_skip, n_ctx, sw, n_state):
    s = xbc.shape[0]
    h = a_log.shape[1]
    g = SSM_GROUPS
    hg, p = h // g, sw // h
    gw = hg * p
    l = _pick(math.gcd(n_ctx, s - n_ctx), SCAN_CHUNK, 8)
    nct, nlt = n_ctx // l, (s - n_ctx) // l
    fwd, bwd = _chunk_maps(nct, nlt)
    dtg = dt_raw.reshape(s, 2, g, hg).transpose(2, 0, 1, 3).reshape(g, s, 2 * hg)
    dt_cols = jnp.zeros((g, s, LANES), F32).at[:, :, :2 * hg].set(dtg)
    assert 2 * hg <= 8 and sw % n_state == 0
    dt_rows = jnp.zeros((g, 8, s), F32).at[:, :2 * hg, :].set(dtg.transpose(0, 2, 1))
    a = (-jnp.exp(a_log.astype(F32))).reshape(2, g, hg).transpose(1, 0, 2).reshape(g, 2 * hg)
    db = dt_bias.reshape(2, g, hg).transpose(1, 0, 2).reshape(g, 2 * hg)
    a_c = jnp.zeros((g, 1, LANES), F32).at[:, 0, :2 * hg].set(a)
    a_r = jnp.zeros((g, 8, 1), F32).at[:, :2 * hg, 0].set(a)
    db_c = jnp.zeros((g, 1, LANES), F32).at[:, 0, :2 * hg].set(db)
    db_r = jnp.zeros((g, 8, 1), F32).at[:, :2 * hg, 0].set(db)
    dsk = jnp.repeat(d_skip, p).reshape(g, 1, gw)
    xo, bo, co = 0, sw // n_state, (sw + g * n_state) // n_state

    def specs(cm):
        return [pl.BlockSpec((l, gw), lambda gi, t: (cm(t), gi)),
                pl.BlockSpec((l, n_state), lambda gi, t: (cm(t), bo + gi)),
                pl.BlockSpec((l, n_state), lambda gi, t: (cm(t), co + gi)),
                pl.BlockSpec((None, l, LANES), lambda gi, t: (gi, cm(t), 0)),
                pl.BlockSpec((None, 8, l), lambda gi, t: (gi, 0, cm(t)))]

    par_c = pl.BlockSpec((None, 1, LANES), lambda gi, t: (gi, 0, 0))
    par_r = pl.BlockSpec((None, 8, 1), lambda gi, t: (gi, 0, 0))
    return pl.pallas_call(
        functools.partial(_ssd_kernel, hg=hg, p=p),
        grid=(g, nct + nlt),
        in_specs=specs(fwd) + specs(bwd) + [par_c, par_r, par_c, par_r,
                                             pl.BlockSpec((None, 1, gw), lambda gi, t: (gi, 0, 0))],
        out_specs=[pl.BlockSpec((l, gw), lambda gi, t: (fwd(t), gi)),
                   pl.BlockSpec((l, gw), lambda gi, t: (bwd(t), gi))],
        out_shape=[jax.ShapeDtypeStruct((s, sw), BF16)] * 2,
        scratch_shapes=[pltpu.VMEM((n_state, gw), F32)] * 2,
        compiler_params=_cparams("parallel", "arbitrary"), name="ssd_scan",
    )(xbc, xbc, xbc, dt_cols, dt_rows, xbc, xbc, xbc, dt_cols, dt_rows, a_c, a_r, db_c, db_r, dsk)


def _finish_kernel(hf, hb, o, yf, yb, z, mg, sg, out, *, nh, dv):
    mw = nh * dv
    h = hf[...].astype(F32) + hb[...].astype(F32)
    gate = _sigmoid(o[...].astype(F32))
    mgv = mg[...]
    for k in range(nh):
        hk = h[:, k * dv:(k + 1) * dv]
        r = lax.rsqrt(jnp.mean(hk * hk, axis=-1, keepdims=True) + EPS)
        out[:, k * dv:(k + 1) * dv] = (hk * r * mgv[:, k * dv:(k + 1) * dv]
                                       * gate[:, k * dv:(k + 1) * dv]).astype(out.dtype)
    y = (yf[...].astype(F32) + yb[...].astype(F32)) * _silu(z[...].astype(F32))
    r = lax.rsqrt(jnp.mean(y * y, axis=-1, keepdims=True) + EPS)
    out[:, mw:] = (y * r * sg[...]).astype(out.dtype)


def _finish(hf, hb, big, yf, yb, mlstm_g, ssm_g, nh, dv, col_o, col_z):
    s, mw = hf.shape
    sw = yf.shape[1]
    r = _pick(s, ROW_TILE, 8)
    return pl.pallas_call(
        functools.partial(_finish_kernel, nh=nh, dv=dv),
        grid=(s // r,),
        in_specs=[pl.BlockSpec((r, mw), lambda i: (i, 0)), pl.BlockSpec((r, mw), lambda i: (i, 0)),
                  pl.BlockSpec((r, mw), lambda i: (i, col_o // mw)),
                  pl.BlockSpec((r, sw), lambda i: (i, 0)), pl.BlockSpec((r, sw), lambda i: (i, 0)),
                  pl.BlockSpec((r, sw), lambda i: (i, col_z // sw)),
                  pl.BlockSpec((1, mw), lambda i: (0, 0)), pl.BlockSpec((1, sw), lambda i: (0, 0))],
        out_specs=pl.BlockSpec((r, mw + sw), lambda i: (i, 0)),
        out_shape=jax.ShapeDtypeStruct((s, mw + sw), BF16),
        compiler_params=_cparams("parallel"), name="mixer_finish",
    )(hf, hb, big, yf, yb, big, mlstm_g.reshape(1, mw), ssm_g.reshape(1, sw))


def _token_copy(src_ref, tok, dst_ref, k, sem, rpt):
    src = pl.multiple_of(tok * rpt, rpt)
    dst = pl.multiple_of(k * rpt, rpt)
    return pltpu.make_async_copy(src_ref.at[pl.ds(src, rpt)], dst_ref.at[pl.ds(dst, rpt)], sem)


def _gather_kernel(idx_ref, nxt_ref, src_ref, o_ref, buf, sem):
    i = pl.program_id(0)
    n_tok, d = o_ref.shape
    rpt = d // LANES
    slot = lax.rem(i, 2)

    def issue(ids_ref, s):
        def body(k, carry):
            _token_copy(src_ref, ids_ref[0, 0, k], buf.at[s], k, sem.at[s], rpt).start()
            return carry
        lax.fori_loop(0, n_tok, body, 0)

    @pl.when(i == 0)
    def _():
        issue(idx_ref, 0)

    @pl.when(i + 1 < pl.num_programs(0))
    def _():
        issue(nxt_ref, 1 - slot)

    def wait(k, carry):
        _token_copy(src_ref, 0, buf.at[slot], k, sem.at[slot], rpt).wait()
        return carry

    lax.fori_loop(0, n_tok, wait, 0)
    for j in range(rpt):
        o_ref[:, j * LANES:(j + 1) * LANES] = buf[slot, pl.ds(j, n_tok, stride=rpt), :].astype(o_ref.dtype)


def _gather_rows(src, idx, d, out_dtype):
    n = idx.shape[0]
    rpt = d // LANES
    r = _pick(n, GATHER_ROWS, 8)
    nt = n // r
    ids = idx.reshape(nt, 1, r)
    return pl.pallas_call(
        _gather_kernel, grid=(nt,),
        in_specs=[pl.BlockSpec((1, 1, r), lambda i: (i, 0, 0), memory_space=pltpu.SMEM),
                  pl.BlockSpec((1, 1, r), lambda i: (jnp.minimum(i + 1, nt - 1), 0, 0), memory_space=pltpu.SMEM),
                  pl.BlockSpec(memory_space=pl.ANY)],
        out_specs=pl.BlockSpec((r, d), lambda i: (i, 0)),
        out_shape=jax.ShapeDtypeStruct((n, d), out_dtype),
        scratch_shapes=[pltpu.VMEM((2, r * rpt, LANES), src.dtype), pltpu.SemaphoreType.DMA((2,))],
        compiler_params=_cparams("arbitrary"), name="moe_gather",
    )(ids, ids, src)


def _combine_kernel(p0_ref, p1_ref, q0_ref, q1_ref, yr_ref, x_ref, gk_ref, gate_ref, fg_ref, o_ref, buf, sem,
                    *, final_norm):
    i = pl.program_id(0)
    n_rows, d = x_ref.shape
    rpt = d // LANES
    slot = lax.rem(i, 2)

    def issue(a_ref, b_ref, s):
        def body(k, carry):
            _token_copy(yr_ref, a_ref[0, 0, k], buf.at[s, 0], k, sem.at[s], rpt).start()
            _token_copy(yr_ref, b_ref[0, 0, k], buf.at[s, 1], k, sem.at[s], rpt).start()
            return carry
        lax.fori_loop(0, n_rows, body, 0)

    @pl.when(i == 0)
    def _():
        issue(p0_ref, p1_ref, 0)

    @pl.when(i + 1 < pl.num_programs(0))
    def _():
        issue(q0_ref, q1_ref, 1 - slot)

    def wait(k, carry):
        _token_copy(yr_ref, 0, buf.at[slot, 0], k, sem.at[slot], rpt).wait()
        _token_copy(yr_ref, 0, buf.at[slot, 1], k, sem.at[slot], rpt).wait()
        return carry

    lax.fori_loop(0, n_rows, wait, 0)
    gk = gk_ref[...]
    g0, g1 = gk[:, 0:1], gk[:, 1:2]
    for j in range(rpt):
        rows = pl.ds(j, n_rows, stride=rpt)
        cols = slice(j * LANES, (j + 1) * LANES)
        moe = g0 * buf[slot, 0, rows, :] + g1 * buf[slot, 1, rows, :]
        o_ref[:, cols] = x_ref[:, cols] + gate_ref[:, cols] * moe
    if final_norm:
        x = o_ref[...]
        o_ref[...] = x * lax.rsqrt(jnp.mean(x * x, axis=-1, keepdims=True) + EPS) * fg_ref[...]


def _moe_combine(xs, yr, pos, gates, gate_mod, n_ctx, final_g):
    s, d = xs.shape
    t = s - n_ctx
    rpt = d // LANES
    r = _pick(math.gcd(n_ctx, t), COMBINE_ROWS, 8)
    nct, nt = n_ctx // r, t // r
    gk = jnp.zeros((t, LANES), F32).at[:, :TOP_K].set(gates)
    p0 = pos[:, 0].reshape(nt, 1, r)
    p1 = pos[:, 1].reshape(nt, 1, r)
    cur = pl.BlockSpec((1, 1, r), lambda i: (i, 0, 0), memory_space=pltpu.SMEM)
    nxt = pl.BlockSpec((1, 1, r), lambda i: (jnp.minimum(i + 1, nt - 1), 0, 0), memory_space=pltpu.SMEM)
    fg = jnp.ones((1, d), F32) if final_g is None else final_g.reshape(1, d).astype(F32)
    return pl.pallas_call(
        functools.partial(_combine_kernel, final_norm=final_g is not None), grid=(nt,),
        in_specs=[cur, cur, nxt, nxt,
                  pl.BlockSpec(memory_space=pl.ANY),
                  pl.BlockSpec((r, d), lambda i: (i + nct, 0)),
                  pl.BlockSpec((r, LANES), lambda i: (i, 0)),
                  pl.BlockSpec((1, d), lambda i: (0, 0)),
                  pl.BlockSpec((1, d), lambda i: (0, 0))],
        out_specs=pl.BlockSpec((r, d), lambda i: (i, 0)),
        out_shape=jax.ShapeDtypeStruct((t, d), F32),
        scratch_shapes=[pltpu.VMEM((2, 2, r * rpt, LANES), F32), pltpu.SemaphoreType.DMA((2,))],
        compiler_params=_cparams("arbitrary"), name="moe_combine",
    )(p0, p1, p0, p1, yr.reshape(yr.shape[0] * rpt, LANES), xs, gk, gate_mod, fg)


def _moe_plan(logits, ne, tm):
    t = logits.shape[0]
    top_val, top_idx = lax.top_k(logits[:, :ne], TOP_K)
    gates = jax.nn.softmax(top_val, axis=-1)
    n_assign = t * TOP_K
    e_flat = top_idx.reshape(-1)
    onehot = (e_flat[:, None] == jnp.arange(ne)[None, :]).astype(jnp.int32)
    rank = jnp.take_along_axis(jnp.cumsum(onehot, axis=0) - onehot, e_flat[:, None], axis=1)[:, 0]
    counts = jnp.sum(onehot, axis=0)
    padded = (counts + tm - 1) // tm * tm
    pend = jnp.cumsum(padded)
    pstart = pend - padded
    pos = (pstart[e_flat] + rank).astype(jnp.int32)
    n_tiles = (n_assign + ne * (tm - 1)) // tm
    n_rows = n_tiles * tm
    tok = jnp.repeat(jnp.arange(t, dtype=jnp.int32), TOP_K)
    row_tok = jnp.zeros((n_rows,), jnp.int32).at[pos].set(tok)
    tile_start = jnp.arange(n_tiles, dtype=jnp.int32) * tm
    tile_expert = jnp.minimum(jnp.searchsorted(pend, tile_start, side='right'), ne - 1).astype(jnp.int32)
    n_used = (pend[-1] // tm).astype(jnp.int32)
    return gates, pos.reshape(t, TOP_K), row_tok, tile_expert, n_used


def _final_norm_kernel(x_ref, g_ref, o_ref):
    x = x_ref[...]
    o_ref[...] = x * lax.rsqrt(jnp.mean(x * x, axis=-1, keepdims=True) + EPS) * g_ref[...]


def _final_norm(xs, g, n_ctx):
    s, d = xs.shape
    t = s - n_ctx
    r = _pick(math.gcd(n_ctx, t), ROW_TILE, 8)
    nct = n_ctx // r
    return pl.pallas_call(
        _final_norm_kernel, grid=(t // r,),
        in_specs=[pl.BlockSpec((r, d), lambda i: (i + nct, 0)), pl.BlockSpec((1, d), lambda i: (0, 0))],
        out_specs=pl.BlockSpec((r, d), lambda i: (i, 0)),
        out_shape=jax.ShapeDtypeStruct((t, d), F32),
        compiler_params=_cparams("parallel"), name="final_rmsnorm",
    )(xs, g.reshape(1, d))


def _to_colmajor(a, n_ctx):
    t, ch = a.shape[0] - n_ctx, a.shape[1]
    lat = a[n_ctx:].reshape(t // GRID_W, GRID_W, ch).transpose(1, 0, 2).reshape(t, ch)
    return jnp.concatenate([a[:n_ctx], lat], axis=0)


def _from_colmajor(a, n_ctx):
    t, ch = a.shape[0] - n_ctx, a.shape[1]
    lat = a[n_ctx:].reshape(GRID_W, t // GRID_W, ch).transpose(1, 0, 2).reshape(t, ch)
    return jnp.concatenate([a[:n_ctx], lat], axis=0)


def kernel(x, c, ctx, c_ctx, w_mod, b_mod, norm1_g, norm2_g, w_in, gate_b, conv_w, conv_b, a_log, dt_bias, d_skip, mlstm_g, ssm_g, w_out, ffn_w1, ffn_w3, ffn_w2, router_w, moe_w1, moe_w3, moe_w2, final_g):
    bsz, t, d = x.shape
    assert bsz == 1, "the kernels treat the single batch element's tokens as rows"
    n_ctx = ctx.shape[1]
    depth = w_mod.shape[0]
    nh = gate_b.shape[1] // 4
    mw = mlstm_g.shape[1]
    dv = mw // nh
    dqk = dv // 2
    sw = ssm_g.shape[1]
    heads = a_log.shape[2]
    conv_ch = conv_w.shape[2]
    n_state = (conv_ch - sw) // (2 * SSM_GROUPS)
    wq = nh * dqk
    splits = (wq, wq, mw, mw, 4 * nh, sw, conv_ch, 2 * heads)
    offs = [0]
    for w_ in splits:
        offs.append(offs[-1] + w_)
    assert offs[-1] == w_in.shape[2]
    ne = router_w.shape[2]

    xs = jnp.concatenate([ctx[0], x[0]], axis=0).astype(F32)
    mod = _modulation(c_ctx.astype(F32), c[0].astype(F32), w_mod.astype(F32), b_mod.astype(F32))

    col_q, col_k, col_v, col_o, col_z, col_x = 0, wq, 2 * wq, 2 * wq + mw, 2 * wq + 2 * mw, 2 * wq + 2 * mw + sw
    n_small = -(-(4 * nh + 2 * heads) // LANES) * LANES

    for layer in range(depth):
        need_ctx = layer < depth - 1
        m6 = mod[layer, :2].reshape(2, 6, d)
        shift1, scale1, gate1, shift2, scale2, gate2 = (m6[:, i][:, None, :] for i in range(6))

        wl = w_in[layer]
        w_big = jnp.concatenate([wl[:, offs[0]:offs[4]].astype(BF16), wl[:, offs[5]:offs[7]].astype(BF16)], axis=1)
        w_small = jnp.zeros((d, n_small), BF16)
        w_small = w_small.at[:, :4 * nh].set(wl[:, offs[4]:offs[5]].astype(BF16))
        w_small = w_small.at[:, 4 * nh:4 * nh + 2 * heads].set(wl[:, offs[7]:offs[8]].astype(BF16))
        xn = _norm_mod(xs, norm1_g[layer], scale1, shift1, n_ctx)
        big = _matmul(xn, w_big, BF16)
        small = _matmul(xn, w_small, F32)

        hf, hb = _mlstm(big, small[:, :4 * nh], gate_b[layer], nh, dqk, dv, n_ctx, col_q, col_k, col_v)

        xbc = _dwconv_silu(_to_colmajor(big[:, col_x:], n_ctx), conv_w[layer], conv_b[layer], n_ctx)
        dt_raw = _to_colmajor(small[:, 4 * nh:4 * nh + 2 * heads], n_ctx)
        yf, yb = _ssd(xbc, dt_raw, a_log[layer], dt_bias[layer], d_skip[layer], n_ctx, sw, n_state)
        yf, yb = _from_colmajor(yf, n_ctx), _from_colmajor(yb, n_ctx)

        mix = _finish(hf, hb, big, yf, yb, mlstm_g[layer], ssm_g[layer], nh, dv, col_o, col_z)
        xs = _matmul_residual(mix, w_out[layer].astype(BF16), xs, gate1, n_ctx, d)

        i = layer // 2
        if layer % 2 == 0:
            hn = _norm_mod(xs, norm2_g[layer], scale2, shift2, n_ctx)
            tm = _pick(hn.shape[0], FFN_TM, 8)
            nt = hn.shape[0] // tm
            act = _glu(hn, ffn_w1[i:i + 1], ffn_w3[i:i + 1], tm, jnp.zeros((nt,), jnp.int32), nt)
            f, fp = ffn_w2.shape[1], act.shape[1]
            w2 = jnp.concatenate([ffn_w2[i].astype(BF16), jnp.zeros((fp - f, d), BF16)], axis=0)
            xs = _matmul_residual(act, w2, xs, gate2, n_ctx, fp // 4)
        else:
            assert not need_ctx, "an MoE layer that still feeds context is not needed by this block"
            hn, logits = _norm_mod(xs, norm2_g[layer], scale2, shift2, n_ctx, router_w=router_w[i])
            tm = min(MOE_TM, t)
            gates, pos, row_tok, tile_expert, n_used = _moe_plan(logits[n_ctx:], ne, tm)
            xr = _gather_rows(hn, row_tok + n_ctx, d, BF16)
            act = _glu(xr, moe_w1[i], moe_w3[i], tm, tile_expert, n_used)
            yr = _matmul_expert(act, moe_w2[i], tm, tile_expert, n_used)
            return _moe_combine(xs, yr, pos, gates, gate2[1], n_ctx, final_g)[None]

    return _final_norm(xs, final_g, n_ctx)[None]
```

```python
import functools
import math

import jax
import jax.numpy as jnp
from jax import lax
from jax.experimental import pallas as pl
from jax.experimental.pallas import tpu as pltpu

GRID_W = 64
SSM_GROUPS = 8
TOP_K = 2
EPS = 1e-6

V7X_VMEM_LIMIT_BYTES = 56 * 1024 * 1024
V7X_VMEM_LIMIT_BIG_BYTES = 60 * 1024 * 1024
LANES = 128
SCAN_CHUNK = 256
ROW_TILE = 256
MOD_TK = 128
MM_TM = 768
MM_TN = 1024
GLU_TN = 512
FFN_TM = 528
DOWN_TN = 1024
MOE_TM = 512
GATHER_ROWS = 256
COMBINE_ROWS = 128

F32 = jnp.float32
BF16 = jnp.bfloat16
HIGHEST = lax.Precision.HIGHEST


def _pick(dim, pref, mult):
    if dim <= pref:
        return dim
    t = (pref // mult) * mult
    while t >= mult:
        if dim % t == 0:
            return t
        t -= mult
    return dim


def _cparams(*sem, vmem=V7X_VMEM_LIMIT_BYTES):
    return pltpu.CompilerParams(dimension_semantics=sem, vmem_limit_bytes=vmem)


def _sigmoid(v):
    return 1.0 / (1.0 + jnp.exp(-v))


def _silu(v):
    return v * _sigmoid(v)


def _softplus(v):
    return jnp.maximum(v, 0.0) + jnp.log1p(jnp.exp(-jnp.abs(v)))


def _log_sigmoid(v):
    return -_softplus(-v)


def _split3(v):
    hi = v.astype(BF16)
    r = v - hi.astype(F32)
    mid = r.astype(BF16)
    lo = (r - mid.astype(F32)).astype(BF16)
    return hi, mid, lo


def _dot_exact(mat01, v, left):
    m = mat01.astype(BF16)
    out = None
    for piece in _split3(v):
        part = (jnp.dot(m, piece, preferred_element_type=F32) if left
                else jnp.dot(piece, m, preferred_element_type=F32))
        out = part if out is None else out + part
    return out


def _mod_kernel(c_ref, w_ref, b_ref, o_ref, acc, *, nk):
    k = pl.program_id(1)

    @pl.when(k == 0)
    def _():
        acc[...] = jnp.zeros_like(acc)

    s_ctx = _silu(c_ref[0])
    s_lat = _silu(c_ref[1])
    tk = w_ref.shape[0]
    for n0 in range(0, w_ref.shape[1], LANES):
        w = w_ref[:, n0:n0 + LANES]
        acc[0, :, n0:n0 + LANES] += jnp.sum((w * s_ctx).reshape(tk // 8, 8, LANES), axis=0)
        acc[1, :, n0:n0 + LANES] += jnp.sum((w * s_lat).reshape(tk // 8, 8, LANES), axis=0)

    @pl.when(k == nk - 1)
    def _():
        o_ref[...] = jnp.zeros_like(o_ref)
        o_ref[0:1, :] = jnp.sum(acc[0], axis=0, keepdims=True) + b_ref[...]
        o_ref[1:2, :] = jnp.sum(acc[1], axis=0, keepdims=True) + b_ref[...]


def _modulation(c_ctx, c_lat, w_mod, b_mod):
    depth, d, n = w_mod.shape
    tk = _pick(d, MOD_TK, 8)
    nk = d // tk
    cb = jnp.broadcast_to(jnp.stack([c_ctx, c_lat])[:, :, None], (2, d, LANES))
    return pl.pallas_call(
        functools.partial(_mod_kernel, nk=nk),
        grid=(depth, nk),
        in_specs=[pl.BlockSpec((2, tk, LANES), lambda l, k: (0, k, 0)),
                  pl.BlockSpec((None, tk, n), lambda l, k: (l, k, 0)),
                  pl.BlockSpec((None, 1, n), lambda l, k: (l, 0, 0))],
        out_specs=pl.BlockSpec((None, 8, n), lambda l, k: (l, 0, 0)),
        out_shape=jax.ShapeDtypeStruct((depth, 8, n), F32),
        scratch_shapes=[pltpu.VMEM((2, 8, n), F32)],
        compiler_params=_cparams("parallel", "arbitrary"),
        name="adaln_modulation",
    )(cb, w_mod, b_mod.reshape(depth, 1, n))


def _norm_kernel(x_ref, g_ref, sc_ref, sh_ref, o_ref):
    x = x_ref[...]
    r = lax.rsqrt(jnp.mean(x * x, axis=-1, keepdims=True) + EPS)
    o_ref[...] = (x * r * g_ref[...] * (1.0 + sc_ref[...]) + sh_ref[...]).astype(o_ref.dtype)


def _norm_router_kernel(x_ref, g_ref, sc_ref, sh_ref, rw_ref, o_ref, lg_ref):
    x = x_ref[...]
    r = lax.rsqrt(jnp.mean(x * x, axis=-1, keepdims=True) + EPS)
    h = x * r * g_ref[...] * (1.0 + sc_ref[...]) + sh_ref[...]
    o_ref[...] = h
    lg_ref[...] = jnp.dot(h, rw_ref[...], preferred_element_type=F32, precision=HIGHEST)


def _norm_mod(xs, g, scale, shift, n_ctx, router_w=None):
    s, d = xs.shape
    r = _pick(math.gcd(n_ctx, s - n_ctx), ROW_TILE, 8)
    nct = n_ctx // r
    row_spec = pl.BlockSpec((r, d), lambda i: (i, 0))
    mod_spec = pl.BlockSpec((None, 1, d), lambda i: (jnp.where(i < nct, 0, 1), 0, 0))
    in_specs = [row_spec, pl.BlockSpec((1, d), lambda i: (0, 0)), mod_spec, mod_spec]
    if router_w is None:
        return pl.pallas_call(
            _norm_kernel, grid=(s // r,), in_specs=in_specs, out_specs=row_spec,
            out_shape=jax.ShapeDtypeStruct((s, d), BF16),
            compiler_params=_cparams("parallel"), name="rmsnorm_modulate",
        )(xs, g.reshape(1, d), scale, shift)
    ne = router_w.shape[1]
    rw = jnp.zeros((d, LANES), F32).at[:, :ne].set(router_w)
    return pl.pallas_call(
        _norm_router_kernel, grid=(s // r,),
        in_specs=in_specs + [pl.BlockSpec((d, LANES), lambda i: (0, 0))],
        out_specs=[row_spec, pl.BlockSpec((r, LANES), lambda i: (i, 0))],
        out_shape=[jax.ShapeDtypeStruct((s, d), F32), jax.ShapeDtypeStruct((s, LANES), F32)],
        compiler_params=_cparams("parallel"), name="rmsnorm_modulate_router",
    )(xs, g.reshape(1, d), scale, shift, rw)


def _mm_kernel(a_ref, w_ref, o_ref):
    o_ref[...] = jnp.dot(a_ref[...], w_ref[...], preferred_element_type=F32).astype(o_ref.dtype)


def _matmul(a, w, out_dtype):
    m, k = a.shape
    n = w.shape[1]
    tm = _pick(m, MM_TM, 8)
    tn = _pick(n, MM_TN, LANES)
    return pl.pallas_call(
        _mm_kernel, grid=(m // tm, n // tn),
        in_specs=[pl.BlockSpec((tm, k), lambda i, j: (i, 0)), pl.BlockSpec((k, tn), lambda i, j: (0, j))],
        out_specs=pl.BlockSpec((tm, tn), lambda i, j: (i, j)),
        out_shape=jax.ShapeDtypeStruct((m, n), out_dtype),
        compiler_params=_cparams("parallel", "parallel"), name="matmul",
    )(a, w)


def _row_gate(gate_ref, row0, tm, n_ctx):
    rows = row0 + lax.broadcasted_iota(jnp.int32, (tm, 1), 0)
    return jnp.where(rows < n_ctx, gate_ref[0], gate_ref[1])


def _mm_res_kernel(a_ref, w_ref, res_ref, gate_ref, o_ref, acc_ref, *, nk, tm, n_ctx, k_tail):
    k = pl.program_id(2)

    @pl.when(k == 0)
    def _():
        acc_ref[...] = jnp.zeros_like(acc_ref)

    if k_tail:
        @pl.when(k < nk - 1)
        def _():
            acc_ref[...] += jnp.dot(a_ref[...], w_ref[...], preferred_element_type=F32)

        @pl.when(k == nk - 1)
        def _():
            w = w_ref[...]
            w = jnp.where(lax.broadcasted_iota(jnp.int32, w.shape, 0) < k_tail, w, jnp.zeros_like(w))
            acc_ref[...] += jnp.dot(a_ref[...], w, preferred_element_type=F32)
    else:
        acc_ref[...] += jnp.dot(a_ref[...], w_ref[...], preferred_element_type=F32)

    @pl.when(k == nk - 1)
    def _():
        gate = _row_gate(gate_ref, pl.program_id(0) * tm, tm, n_ctx)
        o_ref[...] = res_ref[...] + gate * acc_ref[...]


def _matmul_residual(a, w, res, gate, n_ctx, tk_pref):
    m, kdim = a.shape
    n = w.shape[1]
    tm = _pick(m, MM_TM, 8)
    tn = _pick(n, MM_TN, LANES)
    tk = _pick(kdim, tk_pref, LANES)
    if w.shape[0] <= (kdim // tk - 1) * tk:
        tk = kdim
    nk = kdim // tk
    k_tail = w.shape[0] - (nk - 1) * tk
    assert 0 < k_tail <= tk
    return pl.pallas_call(
        functools.partial(_mm_res_kernel, nk=nk, tm=tm, n_ctx=n_ctx, k_tail=0 if k_tail == tk else k_tail),
        grid=(m // tm, n // tn, nk),
        in_specs=[pl.BlockSpec((tm, tk), lambda i, j, k: (i, k)),
                  pl.BlockSpec((tk, tn), lambda i, j, k: (k, j)),
                  pl.BlockSpec((tm, tn), lambda i, j, k: (i, j)),
                  pl.BlockSpec((2, 1, tn), lambda i, j, k: (0, 0, j))],
        out_specs=pl.BlockSpec((tm, tn), lambda i, j, k: (i, j)),
        out_shape=jax.ShapeDtypeStruct((m, n), F32),
        scratch_shapes=[pltpu.VMEM((tm, tn), F32)],
        compiler_params=_cparams("parallel", "parallel", "arbitrary"), name="matmul_residual",
    )(a, w, res, gate)


def _mm_res_ws_kernel(a_ref, w_ref, res_ref, gate_ref, o_ref, wbuf, *, tm, n_ctx):
    i = pl.program_id(1)

    @pl.when(i == 0)
    def _():
        wbuf[...] = w_ref[...].astype(BF16)

    acc = jnp.dot(a_ref[...], wbuf[...], preferred_element_type=F32)
    o_ref[...] = res_ref[...] + _row_gate(gate_ref, i * tm, tm, n_ctx) * acc


def _matmul_residual_stationary(a, w, layer, res, gate, n_ctx):
    m, k = a.shape
    n = w.shape[2]
    tm = _pick(m, MM_TM, 8)
    tn = _pick(n, GLU_TN, LANES)
    return pl.pallas_call(
        functools.partial(_mm_res_ws_kernel, tm=tm, n_ctx=n_ctx),
        grid=(n // tn, m // tm),
        in_specs=[pl.BlockSpec((tm, k), lambda j, i: (i, 0)),
                  pl.BlockSpec((None, k, tn), lambda j, i: (layer, 0, j)),
                  pl.BlockSpec((tm, tn), lambda j, i: (i, j)),
                  pl.BlockSpec((2, 1, tn), lambda j, i: (0, 0, j))],
        out_specs=pl.BlockSpec((tm, tn), lambda j, i: (i, j)),
        out_shape=jax.ShapeDtypeStruct((m, n), F32),
        scratch_shapes=[pltpu.VMEM((k, tn), BF16)],
        compiler_params=_cparams("arbitrary", "arbitrary"), name="matmul_residual_ws",
    )(a, w, res, gate)


def _inproj_kernel(a_ref, wa_ref, wb_ref, o_ref, wbuf, *, shift_from, shift):
    j, i = pl.program_id(0), pl.program_id(1)

    @pl.when(jnp.logical_and(i == 0, j < shift_from))
    def _():
        wbuf[...] = wa_ref[...].astype(BF16)

    @pl.when(jnp.logical_and(i == 0, j >= shift_from))
    def _():
        wbuf[...] = jnp.concatenate([wa_ref[:, shift:], wb_ref[:, :shift]], axis=1).astype(BF16)

    o_ref[...] = jnp.dot(a_ref[...], wbuf[...], preferred_element_type=F32).astype(o_ref.dtype)


def _inproj(a, w, layer, skip_at, skip, n_out):
    m, k = a.shape
    tm = _pick(m, MM_TM, 8)
    tn = _pick(math.gcd(skip_at, n_out), GLU_TN, LANES)
    assert skip_at % tn == 0 and n_out % tn == 0 and 0 < skip < LANES and tn % LANES == 0
    lb = tn // LANES
    return pl.pallas_call(
        functools.partial(_inproj_kernel, shift_from=skip_at // tn, shift=skip),
        grid=(n_out // tn, m // tm),
        in_specs=[pl.BlockSpec((tm, k), lambda j, i: (i, 0)),
                  pl.BlockSpec((None, k, tn), lambda j, i: (layer, 0, j)),
                  pl.BlockSpec((None, k, LANES), lambda j, i: (layer, 0, (j + 1) * lb))],
        out_specs=pl.BlockSpec((tm, tn), lambda j, i: (i, j)),
        out_shape=jax.ShapeDtypeStruct((m, n_out), BF16),
        scratch_shapes=[pltpu.VMEM((k, tn), BF16)],
        compiler_params=_cparams("arbitrary", "arbitrary"), name="in_projection",
    )(a, w, w)


def _stationary_plan(tile_expert, n_used, n_experts, nj):
    n_tiles = tile_expert.shape[0]
    tile_ids = jnp.arange(n_tiles, dtype=jnp.int32)
    cnt = jnp.sum(((tile_expert[:, None] == jnp.arange(n_experts)[None, :])
                   & (tile_ids[:, None] < n_used)).astype(jnp.int32), axis=0)
    seg_end = jnp.cumsum(cnt)
    seg_start = seg_end - cnt
    step = jnp.arange(n_tiles * nj, dtype=jnp.int32)
    used_steps = n_used * nj
    e = jnp.minimum(jnp.searchsorted(seg_end * nj, step, side='right'), n_experts - 1).astype(jnp.int32)
    seg_len = jnp.maximum(cnt[e], 1)
    local = jnp.maximum(step - nj * seg_start[e], 0)
    j = local // seg_len
    i = seg_start[e] + local % seg_len
    used = step < used_steps
    last = used_steps - 1
    fill = step - used_steps
    pick = lambda arr: jnp.where(used, arr, arr[last]).astype(jnp.int32)
    out_i = jnp.where(used, i, n_used + fill // nj).astype(jnp.int32)
    out_j = jnp.where(used, j, fill % nj).astype(jnp.int32)
    return pick(e), pick(j), pick(i), out_i, out_j, used.astype(jnp.int32)


def _stationary_flags(e_ref, jw_ref, used_ref):
    s = pl.program_id(0)
    prev = jnp.maximum(s - 1, 0)
    fresh = jnp.logical_or(s == 0, jnp.logical_or(e_ref[s] != e_ref[prev], jw_ref[s] != jw_ref[prev]))
    used = used_ref[s] > 0
    return used, jnp.logical_and(used, fresh)


def _glu_kernel(e_ref, jw_ref, ia_ref, io_ref, jo_ref, used_ref, a_ref, w1_ref, w3_ref, o_ref, w1b, w3b,
                *, n_valid, ragged):
    used, fresh = _stationary_flags(e_ref, jw_ref, used_ref)

    @pl.when(fresh)
    def _():
        w1 = w1_ref[...]
        w3 = w3_ref[...]
        if ragged:
            col = jw_ref[pl.program_id(0)] * w1.shape[1] + lax.broadcasted_iota(jnp.int32, w1.shape, 1)
            w1 = jnp.where(col < n_valid, w1, 0.0)
            w3 = jnp.where(col < n_valid, w3, 0.0)
        w1b[...] = w1.astype(BF16)
        w3b[...] = w3.astype(BF16)

    @pl.when(used)
    def _():
        a = a_ref[...]
        h1 = jnp.dot(a, w1b[...], preferred_element_type=F32)
        h3 = jnp.dot(a, w3b[...], preferred_element_type=F32)
        o_ref[...] = (_silu(h1) * h3).astype(o_ref.dtype)

    @pl.when(jnp.logical_not(used))
    def _():
        o_ref[...] = jnp.zeros_like(o_ref)


def _stationary_specs(tm, k, tn):
    a_spec = pl.BlockSpec((tm, k), lambda s, e, jw, ia, io, jo, u: (ia[s], 0))
    w_spec = pl.BlockSpec((None, k, tn), lambda s, e, jw, ia, io, jo, u: (e[s], 0, jw[s]))
    o_spec = pl.BlockSpec((tm, tn), lambda s, e, jw, ia, io, jo, u: (io[s], jo[s]))
    return a_spec, w_spec, o_spec


def _glu(a, w1, w3, tm, tile_expert, n_used):
    m, k = a.shape
    ne, _, n = w1.shape
    tn = GLU_TN if n % LANES == 0 and n > GLU_TN else n
    nj = pl.cdiv(n, tn)
    plan = _stationary_plan(tile_expert, n_used, ne, nj)
    a_spec, w_spec, o_spec = _stationary_specs(tm, k, tn)
    return pl.pallas_call(
        functools.partial(_glu_kernel, n_valid=n, ragged=n % tn != 0),
        grid_spec=pltpu.PrefetchScalarGridSpec(
            num_scalar_prefetch=6, grid=(plan[0].shape[0],),
            in_specs=[a_spec, w_spec, w_spec], out_specs=o_spec,
            scratch_shapes=[pltpu.VMEM((k, tn), BF16)] * 2),
        out_shape=jax.ShapeDtypeStruct((m, nj * tn), BF16),
        compiler_params=_cparams("arbitrary", vmem=V7X_VMEM_LIMIT_BIG_BYTES), name="swiglu_up",
    )(*plan, a, w1, w3)


def _mm_expert_kernel(e_ref, jw_ref, ia_ref, io_ref, jo_ref, used_ref, a_ref, w_ref, o_ref, wb):
    used, fresh = _stationary_flags(e_ref, jw_ref, used_ref)

    @pl.when(fresh)
    def _():
        wb[...] = w_ref[...].astype(BF16)

    @pl.when(used)
    def _():
        o_ref[...] = jnp.dot(a_ref[...], wb[...], preferred_element_type=F32).astype(o_ref.dtype)

    @pl.when(jnp.logical_not(used))
    def _():
        o_ref[...] = jnp.zeros_like(o_ref)


def _matmul_expert(a, w, tm, tile_expert, n_used):
    m, k = a.shape
    ne, _, n = w.shape
    tn = _pick(n, DOWN_TN, LANES)
    nj = n // tn
    plan = _stationary_plan(tile_expert, n_used, ne, nj)
    a_spec, w_spec, o_spec = _stationary_specs(tm, k, tn)
    return pl.pallas_call(
        _mm_expert_kernel,
        grid_spec=pltpu.PrefetchScalarGridSpec(
            num_scalar_prefetch=6, grid=(plan[0].shape[0],),
            in_specs=[a_spec, w_spec], out_specs=o_spec,
            scratch_shapes=[pltpu.VMEM((k, tn), BF16)]),
        out_shape=jax.ShapeDtypeStruct((m, n), F32),
        compiler_params=_cparams("arbitrary", vmem=V7X_VMEM_LIMIT_BIG_BYTES), name="expert_down",
    )(*plan, a, w)


def _chunk_maps(nct, nlt):
    def fwd(s):
        return s

    def bwd(s):
        return jnp.where(s < nct, nct - 1 - s, 2 * nct + nlt - 1 - s)

    return fwd, bwd


def _tri(l, lower):
    r = lax.broadcasted_iota(jnp.int32, (l, l), 0)
    c = lax.broadcasted_iota(jnp.int32, (l, l), 1)
    return (r >= c) if lower else (r <= c)


def _mlstm_direction(q, k, v, g, gt, c_ref, n_ref, m_ref, o_ref, *, fwd, scale):
    l = q.shape[0]
    ci, cf = (0, 2) if fwd else (1, 3)
    low = _tri(l, True).astype(F32)
    up = _tri(l, False).astype(F32)
    b_col = _dot_exact(low if fwd else up, _log_sigmoid(g), True)[:, cf:cf + 1]
    b_row = _dot_exact(up if fwd else low, _log_sigmoid(gt), False)[cf:cf + 1, :]
    i_col = g[:, ci:ci + 1]
    i_row = gt[ci:ci + 1, :]
    causal = _tri(l, fwd)
    logw = jnp.where(causal, b_col - b_row + i_row, -jnp.inf)
    m_intra = jnp.max(logw, axis=-1, keepdims=True)
    m_st = m_ref[...]
    inter_log = b_col + m_st
    m_t = jnp.maximum(inter_log, m_intra)
    a_inter = jnp.exp(inter_log - m_t)
    ks = (k.astype(F32) * scale)
    ksb = ks.astype(BF16)
    qk = lax.dot_general(q, ksb, (((1,), (1,)), ((), ())), preferred_element_type=F32)
    s = qk * jnp.exp(logw - m_t)
    c_st = c_ref[...]
    n_st = n_ref[...]
    num = a_inter * jnp.dot(q, c_st.astype(BF16), preferred_element_type=F32) \
        + jnp.dot(s.astype(BF16), v, preferred_element_type=F32)
    den = a_inter * jnp.sum(q.astype(F32) * n_st, axis=-1, keepdims=True) + jnp.sum(s, axis=-1, keepdims=True)
    o_ref[...] = (num / jnp.maximum(jnp.abs(den), jnp.exp(-m_t))).astype(o_ref.dtype)
    b_last = b_col[l - 1:l, :] if fwd else b_col[0:1, :]
    g_col = b_last - b_col + i_col
    m_new = jnp.maximum(b_last + m_st, jnp.max(g_col, axis=0, keepdims=True))
    kw = ks * jnp.exp(g_col - m_new)
    dec = jnp.exp(b_last + m_st - m_new)
    c_ref[...] = dec * c_st + lax.dot_general(kw.astype(BF16), v, (((0,), (0,)), ((), ())),
                                              preferred_element_type=F32)
    n_ref[...] = dec * n_st + jnp.sum(kw, axis=0, keepdims=True)
    m_ref[...] = m_new


def _mlstm_kernel(qf, kf, vf, gf, gtf, qb, kb, vb, gb, gtb, bias, bias_t, hf, hb,
                  cf, nf, mf, cb, nb, mb, *, scale):
    @pl.when(pl.program_id(1) == 0)
    def _():
        for ref in (cf, nf, mf, cb, nb, mb):
            ref[...] = jnp.zeros_like(ref)

    _mlstm_direction(qf[...], kf[...], vf[...], gf[...] + bias[...], gtf[...] + bias_t[...],
                     cf, nf, mf, hf, fwd=True, scale=scale)
    _mlstm_direction(qb[...], kb[...], vb[...], gb[...] + bias[...], gtb[...] + bias_t[...],
                     cb, nb, mb, hb, fwd=False, scale=scale)


def _mlstm(big, gates, gate_b, nh, dqk, dv, n_ctx, col_q, col_k, col_v):
    s = big.shape[0]
    l = _pick(math.gcd(n_ctx, s - n_ctx), SCAN_CHUNK, 8)
    nct, nlt = n_ctx // l, (s - n_ctx) // l
    fwd, bwd = _chunk_maps(nct, nlt)
    gh = gates.reshape(s, 4, nh).transpose(2, 0, 1)
    g_cols = jnp.zeros((nh, s, LANES), F32).at[:, :, :4].set(gh)
    g_rows = jnp.zeros((nh, 8, s), F32).at[:, :4, :].set(gh.transpose(0, 2, 1))
    bh = gate_b.reshape(4, nh).T
    b_cols = jnp.zeros((nh, 1, LANES), F32).at[:, 0, :4].set(bh)
    b_rows = jnp.zeros((nh, 8, 1), F32).at[:, :4, 0].set(bh)
    qo, ko, vo = col_q // dqk, col_k // dqk, col_v // dv

    def specs(cm):
        return [pl.BlockSpec((l, dqk), lambda h, t: (cm(t), qo + h)),
                pl.BlockSpec((l, dqk), lambda h, t: (cm(t), ko + h)),
                pl.BlockSpec((l, dv), lambda h, t: (cm(t), vo + h)),
                pl.BlockSpec((None, l, LANES), lambda h, t: (h, cm(t), 0)),
                pl.BlockSpec((None, 8, l), lambda h, t: (h, 0, cm(t)))]

    out_f = pl.BlockSpec((l, dv), lambda h, t: (fwd(t), h))
    out_b = pl.BlockSpec((l, dv), lambda h, t: (bwd(t), h))
    state = [pltpu.VMEM((dqk, dv), F32), pltpu.VMEM((1, dqk), F32), pltpu.VMEM((1, 1), F32)]
    return pl.pallas_call(
        functools.partial(_mlstm_kernel, scale=dqk ** -0.5),
        grid=(nh, nct + nlt),
        in_specs=specs(fwd) + specs(bwd) + [pl.BlockSpec((None, 1, LANES), lambda h, t: (h, 0, 0)),
                                             pl.BlockSpec((None, 8, 1), lambda h, t: (h, 0, 0))],
        out_specs=[out_f, out_b],
        out_shape=[jax.ShapeDtypeStruct((s, nh * dv), BF16)] * 2,
        scratch_shapes=state + state,
        compiler_params=_cparams("parallel", "arbitrary"), name="mlstm_scan",
    )(big, big, big, g_cols, g_rows, big, big, big, g_cols, g_rows, b_cols, b_rows)


def _conv_kernel(u_ref, prev_ref, next_ref, w_ref, b_ref, o_ref, *, nct, nt, width, halo):
    i = pl.program_id(0)
    first = jnp.logical_or(i == 0, i == nct)
    last = jnp.logical_or(i == nct - 1, i == nt - 1)
    r = u_ref.shape[0]
    prev = jnp.where(first, 0.0, prev_ref[...].astype(F32))
    nxt = jnp.where(last, 0.0, next_ref[...].astype(F32))
    ext = jnp.concatenate([prev, u_ref[...].astype(F32), nxt], axis=0)
    w = w_ref[...]
    acc = b_ref[...] + w[0:1, :] * ext[halo - width // 2: halo - width // 2 + r]
    for j in range(1, width):
        off = halo - width // 2 + j
        acc = acc + w[j:j + 1, :] * ext[off: off + r]
    o_ref[...] = _silu(acc).astype(o_ref.dtype)


def _dwconv_silu(u, conv_w, conv_b, n_ctx):
    s, ch = u.shape
    width = conv_w.shape[0]
    halo = 16
    r = _pick(math.gcd(n_ctx, s - n_ctx), ROW_TILE, halo)
    tc = _pick(ch, 1024, LANES)
    nt, nct, rb = s // r, n_ctx // r, r // halo
    w8 = jnp.zeros((8, ch), F32).at[:width].set(conv_w)
    return pl.pallas_call(
        functools.partial(_conv_kernel, nct=nct, nt=nt, width=width, halo=halo),
        grid=(nt, ch // tc),
        in_specs=[pl.BlockSpec((r, tc), lambda i, j: (i, j)),
                  pl.BlockSpec((halo, tc), lambda i, j: (jnp.maximum(i * rb - 1, 0), j)),
                  pl.BlockSpec((halo, tc), lambda i, j: (jnp.minimum((i + 1) * rb, s // halo - 1), j)),
                  pl.BlockSpec((8, tc), lambda i, j: (0, j)),
                  pl.BlockSpec((1, tc), lambda i, j: (0, j))],
        out_specs=pl.BlockSpec((r, tc), lambda i, j: (i, j)),
        out_shape=jax.ShapeDtypeStruct((s, ch), BF16),
        compiler_params=_cparams("parallel", "parallel"), name="dwconv_silu",
    )(u, u, u, w8, conv_b.reshape(1, ch))


def _per_channel(cols, off, hg, p, lane):
    out = jnp.broadcast_to(cols[:, off + hg - 1:off + hg], lane.shape)
    for h in range(hg - 2, -1, -1):
        out = jnp.where(lane < (h + 1) * p, jnp.broadcast_to(cols[:, off + h:off + h + 1], lane.shape), out)
    return out


def _ssd_direction(x, bm, cm, dtv, da, da_t, st_ref, *, fwd, hg, p):
    l = x.shape[0]
    off = 0 if fwd else hg
    low = _tri(l, True)
    up = _tri(l, False)
    acum_col = _dot_exact(low if fwd else up, da, True)
    acum_row = _dot_exact(up if fwd else low, da_t, False)
    lane = lax.broadcasted_iota(jnp.int32, x.shape, 1)
    acx = _per_channel(acum_col, off, hg, p, lane)
    xdt = x.astype(F32) * _per_channel(dtv, off, hg, p, lane)
    xdt_b = xdt.astype(BF16)
    causal = _tri(l, fwd)
    cb = lax.dot_general(cm, bm, (((1,), (1,)), ((), ())), preferred_element_type=F32)
    y = None
    for h in range(hg):
        seg = acum_col[:, off + h:off + h + 1] - acum_row[off + h:off + h + 1, :]
        mix = (cb * jnp.exp(jnp.where(causal, seg, -jnp.inf))).astype(BF16)
        own = jnp.logical_and(lane >= h * p, lane < (h + 1) * p)
        part = jnp.dot(mix, jnp.where(own, xdt_b, jnp.zeros_like(xdt_b)), preferred_element_type=F32)
        y = part if y is None else y + part
    st = st_ref[...]
    y = y + jnp.dot(cm, st.astype(BF16), preferred_element_type=F32) * jnp.exp(acx)
    total = acx[l - 1:l, :] if fwd else acx[0:1, :]
    xw = (xdt * jnp.exp(total - acx)).astype(BF16)
    st_ref[...] = jnp.exp(total) * st + lax.dot_general(bm, xw, (((0,), (0,)), ((), ())),
                                                        preferred_element_type=F32)
    return y


def _ssd_kernel(xf, bf, cf, dtf, dttf, xb, bb, cb, dtb, dttb, a_c, a_r, db_c, db_r, dsk,
                yf, yb, stf, stb, *, hg, p):
    @pl.when(pl.program_id(1) == 0)
    def _():
        stf[...] = jnp.zeros_like(stf)
        stb[...] = jnp.zeros_like(stb)

    def steps(dt_ref, dtt_ref):
        dtv = _softplus(dt_ref[...] + db_c[...])
        da_t = _softplus(dtt_ref[...] + db_r[...]) * a_r[...]
        return dtv, dtv * a_c[...], da_t

    x = xf[...]
    y = _ssd_direction(x, bf[...], cf[...], *steps(dtf, dttf), stf, fwd=True, hg=hg, p=p)
    yf[...] = (y + dsk[...] * x.astype(F32)).astype(yf.dtype)
    y = _ssd_direction(xb[...], bb[...], cb[...], *steps(dtb, dttb), stb, fwd=False, hg=hg, p=p)
    yb[...] = y.astype(yb.dtype)


def _ssd(xbc, dt_raw, a_log, dt_bias, d_skip, n_ctx, sw, n_state):
    s = xbc.shape[0]
    h = a_log.shape[1]
    g = SSM_GROUPS
    hg, p = h // g, sw // h
    gw = hg * p
    l = _pick(math.gcd(n_ctx, s - n_ctx), SCAN_CHUNK, 8)
    nct, nlt = n_ctx // l, (s - n_ctx) // l
    fwd, bwd = _chunk_maps(nct, nlt)
    dtg = dt_raw.reshape(s, 2, g, hg).transpose(2, 0, 1, 3).reshape(g, s, 2 * hg)
    dt_cols = jnp.zeros((g, s, LANES), F32).at[:, :, :2 * hg].set(dtg)
    assert 2 * hg <= 8 and sw % n_state == 0
    dt_rows = jnp.zeros((g, 8, s), F32).at[:, :2 * hg, :].set(dtg.transpose(0, 2, 1))
    a = (-jnp.exp(a_log.astype(F32))).reshape(2, g, hg).transpose(1, 0, 2).reshape(g, 2 * hg)
    db = dt_bias.reshape(2, g, hg).transpose(1, 0, 2).reshape(g, 2 * hg)
    a_c = jnp.zeros((g, 1, LANES), F32).at[:, 0, :2 * hg].set(a)
    a_r = jnp.zeros((g, 8, 1), F32).at[:, :2 * hg, 0].set(a)
    db_c = jnp.zeros((g, 1, LANES), F32).at[:, 0, :2 * hg].set(db)
    db_r = jnp.zeros((g, 8, 1), F32).at[:, :2 * hg, 0].set(db)
    dsk = jnp.repeat(d_skip, p).reshape(g, 1, gw)
    xo, bo, co = 0, sw // n_state, (sw + g * n_state) // n_state

    def specs(cm):
        return [pl.BlockSpec((l, gw), lambda gi, t: (cm(t), gi)),
                pl.BlockSpec((l, n_state), lambda gi, t: (cm(t), bo + gi)),
                pl.BlockSpec((l, n_state), lambda gi, t: (cm(t), co + gi)),
                pl.BlockSpec((None, l, LANES), lambda gi, t: (gi, cm(t), 0)),
                pl.BlockSpec((None, 8, l), lambda gi, t: (gi, 0, cm(t)))]

    par_c = pl.BlockSpec((None, 1, LANES), lambda gi, t: (gi, 0, 0))
    par_r = pl.BlockSpec((None, 8, 1), lambda gi, t: (gi, 0, 0))
    return pl.pallas_call(
        functools.partial(_ssd_kernel, hg=hg, p=p),
        grid=(g, nct + nlt),
        in_specs=specs(fwd) + specs(bwd) + [par_c, par_r, par_c, par_r,
                                             pl.BlockSpec((None, 1, gw), lambda gi, t: (gi, 0, 0))],
        out_specs=[pl.BlockSpec((l, gw), lambda gi, t: (fwd(t), gi)),
                   pl.BlockSpec((l, gw), lambda gi, t: (bwd(t), gi))],
        out_shape=[jax.ShapeDtypeStruct((s, sw), BF16)] * 2,
        scratch_shapes=[pltpu.VMEM((n_state, gw), F32)] * 2,
        compiler_params=_cparams("parallel", "arbitrary"), name="ssd_scan",
    )(xbc, xbc, xbc, dt_cols, dt_rows, xbc, xbc, xbc, dt_cols, dt_rows, a_c, a_r, db_c, db_r, dsk)


def _finish_kernel(hf, hb, o, yf, yb, z, mg, sg, out, *, nh, dv):
    mw = nh * dv
    h = hf[...].astype(F32) + hb[...].astype(F32)
    gate = _sigmoid(o[...].astype(F32))
    mgv = mg[...]
    for k in range(nh):
        hk = h[:, k * dv:(k + 1) * dv]
        r = lax.rsqrt(jnp.mean(hk * hk, axis=-1, keepdims=True) + EPS)
        out[:, k * dv:(k + 1) * dv] = (hk * r * mgv[:, k * dv:(k + 1) * dv]
                                       * gate[:, k * dv:(k + 1) * dv]).astype(out.dtype)
    y = (yf[...].astype(F32) + yb[...].astype(F32)) * _silu(z[...].astype(F32))
    r = lax.rsqrt(jnp.mean(y * y, axis=-1, keepdims=True) + EPS)
    out[:, mw:] = (y * r * sg[...]).astype(out.dtype)


def _finish(hf, hb, big, yf, yb, mlstm_g, ssm_g, nh, dv, col_o, col_z):
    s, mw = hf.shape
    sw = yf.shape[1]
    r = _pick(s, ROW_TILE, 8)
    return pl.pallas_call(
        functools.partial(_finish_kernel, nh=nh, dv=dv),
        grid=(s // r,),
        in_specs=[pl.BlockSpec((r, mw), lambda i: (i, 0)), pl.BlockSpec((r, mw), lambda i: (i, 0)),
                  pl.BlockSpec((r, mw), lambda i: (i, col_o // mw)),
                  pl.BlockSpec((r, sw), lambda i: (i, 0)), pl.BlockSpec((r, sw), lambda i: (i, 0)),
                  pl.BlockSpec((r, sw), lambda i: (i, col_z // sw)),
                  pl.BlockSpec((1, mw), lambda i: (0, 0)), pl.BlockSpec((1, sw), lambda i: (0, 0))],
        out_specs=pl.BlockSpec((r, mw + sw), lambda i: (i, 0)),
        out_shape=jax.ShapeDtypeStruct((s, mw + sw), BF16),
        compiler_params=_cparams("parallel"), name="mixer_finish",
    )(hf, hb, big, yf, yb, big, mlstm_g.reshape(1, mw), ssm_g.reshape(1, sw))


def _row_copy(src_ref, row, dst_ref, k, sem):
    return pltpu.make_async_copy(src_ref.at[pl.ds(row, 1)], dst_ref.at[pl.ds(k, 1)], sem)


def _gather_kernel(idx_ref, nxt_ref, src_ref, o_ref, buf, sem):
    i = pl.program_id(0)
    n_rows = o_ref.shape[0]
    slot = lax.rem(i, 2)

    def issue(ids_ref, s):
        def body(k, carry):
            for p in range(2):
                _row_copy(src_ref, ids_ref[0, 0, 2 * k + p], buf.at[s], 2 * k + p, sem.at[s]).start(priority=p)
            return carry
        lax.fori_loop(0, n_rows // 2, body, 0)

    @pl.when(i == 0)
    def _():
        issue(idx_ref, 0)

    @pl.when(i + 1 < pl.num_programs(0))
    def _():
        issue(nxt_ref, 1 - slot)

    def wait(k, carry):
        _row_copy(src_ref, 0, buf.at[slot], k, sem.at[slot]).wait()
        return carry

    lax.fori_loop(0, n_rows, wait, 0)
    o_ref[...] = buf[slot].astype(o_ref.dtype)


def _gather_rows(src, idx, out_dtype):
    n = idx.shape[0]
    d = src.shape[1]
    r = _pick(n, GATHER_ROWS, 8)
    assert r % 2 == 0
    nt = n // r
    ids = idx.reshape(nt, 1, r)
    return pl.pallas_call(
        _gather_kernel, grid=(nt,),
        in_specs=[pl.BlockSpec((1, 1, r), lambda i: (i, 0, 0), memory_space=pltpu.SMEM),
                  pl.BlockSpec((1, 1, r), lambda i: (jnp.minimum(i + 1, nt - 1), 0, 0), memory_space=pltpu.SMEM),
                  pl.BlockSpec(memory_space=pl.ANY)],
        out_specs=pl.BlockSpec((r, d), lambda i: (i, 0)),
        out_shape=jax.ShapeDtypeStruct((n, d), out_dtype),
        scratch_shapes=[pltpu.VMEM((2, r, d), src.dtype), pltpu.SemaphoreType.DMA((2,))],
        compiler_params=_cparams("arbitrary"), name="moe_gather",
    )(ids, ids, src)


def _combine_kernel(p0_ref, p1_ref, q0_ref, q1_ref, yr_ref, x_ref, gk_ref, gate_ref, fg_ref, o_ref, buf, sem,
                    *, final_norm):
    i = pl.program_id(0)
    n_rows = x_ref.shape[0]
    slot = lax.rem(i, 2)

    def issue(a_ref, b_ref, s):
        def body(k, carry):
            _row_copy(yr_ref, a_ref[0, 0, k], buf.at[s, 0], k, sem.at[s]).start(priority=0)
            _row_copy(yr_ref, b_ref[0, 0, k], buf.at[s, 1], k, sem.at[s]).start(priority=1)
            return carry
        lax.fori_loop(0, n_rows, body, 0)

    @pl.when(i == 0)
    def _():
        issue(p0_ref, p1_ref, 0)

    @pl.when(i + 1 < pl.num_programs(0))
    def _():
        issue(q0_ref, q1_ref, 1 - slot)

    def wait(k, carry):
        _row_copy(yr_ref, 0, buf.at[slot, 0], k, sem.at[slot]).wait()
        _row_copy(yr_ref, 0, buf.at[slot, 1], k, sem.at[slot]).wait()
        return carry

    lax.fori_loop(0, n_rows, wait, 0)
    gk = gk_ref[...]
    x = x_ref[...] + gate_ref[...] * (gk[:, 0:1] * buf[slot, 0] + gk[:, 1:2] * buf[slot, 1])
    if final_norm:
        x = x * lax.rsqrt(jnp.mean(x * x, axis=-1, keepdims=True) + EPS) * fg_ref[...]
    o_ref[...] = x


def _moe_combine(xs, yr, pos, gates, gate_mod, n_ctx, final_g):
    s, d = xs.shape
    t = s - n_ctx
    r = _pick(math.gcd(n_ctx, t), COMBINE_ROWS, 8)
    nct, nt = n_ctx // r, t // r
    gk = jnp.zeros((t, LANES), F32).at[:, :TOP_K].set(gates)
    p0 = pos[:, 0].reshape(nt, 1, r)
    p1 = pos[:, 1].reshape(nt, 1, r)
    cur = pl.BlockSpec((1, 1, r), lambda i: (i, 0, 0), memory_space=pltpu.SMEM)
    nxt = pl.BlockSpec((1, 1, r), lambda i: (jnp.minimum(i + 1, nt - 1), 0, 0), memory_space=pltpu.SMEM)
    fg = jnp.ones((1, d), F32) if final_g is None else final_g.reshape(1, d).astype(F32)
    return pl.pallas_call(
        functools.partial(_combine_kernel, final_norm=final_g is not None), grid=(nt,),
        in_specs=[cur, cur, nxt, nxt,
                  pl.BlockSpec(memory_space=pl.ANY),
                  pl.BlockSpec((r, d), lambda i: (i + nct, 0)),
                  pl.BlockSpec((r, LANES), lambda i: (i, 0)),
                  pl.BlockSpec((1, d), lambda i: (0, 0)),
                  pl.BlockSpec((1, d), lambda i: (0, 0))],
        out_specs=pl.BlockSpec((r, d), lambda i: (i, 0)),
        out_shape=jax.ShapeDtypeStruct((t, d), F32),
        scratch_shapes=[pltpu.VMEM((2, 2, r, d), F32), pltpu.SemaphoreType.DMA((2,))],
        compiler_params=_cparams("arbitrary"), name="moe_combine",
    )(p0, p1, p0, p1, yr, xs, gk, gate_mod, fg)


def _moe_plan(logits, ne, tm):
    t = logits.shape[0]
    top_val, top_idx = lax.top_k(logits[:, :ne], TOP_K)
    gates = jax.nn.softmax(top_val, axis=-1)
    n_assign = t * TOP_K
    e_flat = top_idx.reshape(-1)
    onehot = (e_flat[:, None] == jnp.arange(ne)[None, :]).astype(jnp.int32)
    rank = jnp.take_along_axis(jnp.cumsum(onehot, axis=0) - onehot, e_flat[:, None], axis=1)[:, 0]
    counts = jnp.sum(onehot, axis=0)
    padded = (counts + tm - 1) // tm * tm
    pend = jnp.cumsum(padded)
    pstart = pend - padded
    pos = (pstart[e_flat] + rank).astype(jnp.int32)
    n_tiles = (n_assign + ne * (tm - 1)) // tm
    n_rows = n_tiles * tm
    tok = jnp.repeat(jnp.arange(t, dtype=jnp.int32), TOP_K)
    row_tok = jnp.zeros((n_rows,), jnp.int32).at[pos].set(tok)
    tile_start = jnp.arange(n_tiles, dtype=jnp.int32) * tm
    tile_expert = jnp.minimum(jnp.searchsorted(pend, tile_start, side='right'), ne - 1).astype(jnp.int32)
    n_used = (pend[-1] // tm).astype(jnp.int32)
    return gates, pos.reshape(t, TOP_K), row_tok, tile_expert, n_used


def _final_norm_kernel(x_ref, g_ref, o_ref):
    x = x_ref[...]
    o_ref[...] = x * lax.rsqrt(jnp.mean(x * x, axis=-1, keepdims=True) + EPS) * g_ref[...]


def _final_norm(xs, g, n_ctx):
    s, d = xs.shape
    t = s - n_ctx
    r = _pick(math.gcd(n_ctx, t), ROW_TILE, 8)
    nct = n_ctx // r
    return pl.pallas_call(
        _final_norm_kernel, grid=(t // r,),
        in_specs=[pl.BlockSpec((r, d), lambda i: (i + nct, 0)), pl.BlockSpec((1, d), lambda i: (0, 0))],
        out_specs=pl.BlockSpec((r, d), lambda i: (i, 0)),
        out_shape=jax.ShapeDtypeStruct((t, d), F32),
        compiler_params=_cparams("parallel"), name="final_rmsnorm",
    )(xs, g.reshape(1, d))


def _to_colmajor(a, n_ctx):
    t, ch = a.shape[0] - n_ctx, a.shape[1]
    lat = a[n_ctx:].reshape(t // GRID_W, GRID_W, ch).transpose(1, 0, 2).reshape(t, ch)
    return jnp.concatenate([a[:n_ctx], lat], axis=0)


def _from_colmajor(a, n_ctx):
    t, ch = a.shape[0] - n_ctx, a.shape[1]
    lat = a[n_ctx:].reshape(GRID_W, t // GRID_W, ch).transpose(1, 0, 2).reshape(t, ch)
    return jnp.concatenate([a[:n_ctx], lat], axis=0)


def kernel(x, c, ctx, c_ctx, w_mod, b_mod, norm1_g, norm2_g, w_in, gate_b, conv_w, conv_b, a_log, dt_bias, d_skip, mlstm_g, ssm_g, w_out, ffn_w1, ffn_w3, ffn_w2, router_w, moe_w1, moe_w3, moe_w2, final_g):
    bsz, t, d = x.shape
    assert bsz == 1, "the kernels treat the single batch element's tokens as rows"
    n_ctx = ctx.shape[1]
    depth = w_mod.shape[0]
    nh = gate_b.shape[1] // 4
    mw = mlstm_g.shape[1]
    dv = mw // nh
    dqk = dv // 2
    sw = ssm_g.shape[1]
    heads = a_log.shape[2]
    conv_ch = conv_w.shape[2]
    n_state = (conv_ch - sw) // (2 * SSM_GROUPS)
    wq = nh * dqk
    splits = (wq, wq, mw, mw, 4 * nh, sw, conv_ch, 2 * heads)
    offs = [0]
    for w_ in splits:
        offs.append(offs[-1] + w_)
    assert offs[-1] == w_in.shape[2]
    ne = router_w.shape[2]

    xs = jnp.concatenate([ctx[0], x[0]], axis=0).astype(F32)
    mod = _modulation(c_ctx.astype(F32), c[0].astype(F32), w_mod.astype(F32), b_mod.astype(F32))

    col_q, col_k, col_v, col_o, col_z, col_x = 0, wq, 2 * wq, 2 * wq + mw, 2 * wq + 2 * mw, 2 * wq + 2 * mw + sw
    n_small = -(-(4 * nh + 2 * heads) // LANES) * LANES

    for layer in range(depth):
        need_ctx = layer < depth - 1
        m6 = mod[layer, :2].reshape(2, 6, d)
        shift1, scale1, gate1, shift2, scale2, gate2 = (m6[:, i][:, None, :] for i in range(6))

        wl = w_in[layer]
        w_small = jnp.zeros((d, n_small), BF16)
        w_small = w_small.at[:, :4 * nh].set(wl[:, offs[4]:offs[5]].astype(BF16))
        w_small = w_small.at[:, 4 * nh:4 * nh + 2 * heads].set(wl[:, offs[7]:offs[8]].astype(BF16))
        xn = _norm_mod(xs, norm1_g[layer], scale1, shift1, n_ctx)
        big = _inproj(xn, w_in, layer, offs[4], splits[4], offs[4] + offs[7] - offs[5])
        small = _matmul(xn, w_small, F32)

        hf, hb = _mlstm(big, small[:, :4 * nh], gate_b[layer], nh, dqk, dv, n_ctx, col_q, col_k, col_v)

        xbc = _dwconv_silu(_to_colmajor(big[:, col_x:], n_ctx), conv_w[layer], conv_b[layer], n_ctx)
        dt_raw = _to_colmajor(small[:, 4 * nh:4 * nh + 2 * heads], n_ctx)
        yf, yb = _ssd(xbc, dt_raw, a_log[layer], dt_bias[layer], d_skip[layer], n_ctx, sw, n_state)
        yf, yb = _from_colmajor(yf, n_ctx), _from_colmajor(yb, n_ctx)

        mix = _finish(hf, hb, big, yf, yb, mlstm_g[layer], ssm_g[layer], nh, dv, col_o, col_z)
        xs = _matmul_residual_stationary(mix, w_out, layer, xs, gate1, n_ctx)

        i = layer // 2
        if layer % 2 == 0:
            hn = _norm_mod(xs, norm2_g[layer], scale2, shift2, n_ctx)
            tm = _pick(hn.shape[0], FFN_TM, 8)
            nt = hn.shape[0] // tm
            act = _glu(hn, ffn_w1[i:i + 1], ffn_w3[i:i + 1], tm, jnp.zeros((nt,), jnp.int32), nt)
            xs = _matmul_residual(act, ffn_w2[i].astype(BF16), xs, gate2, n_ctx, act.shape[1] // 4)
        else:
            assert not need_ctx, "an MoE layer that still feeds context is not needed by this block"
            hn, logits = _norm_mod(xs, norm2_g[layer], scale2, shift2, n_ctx, router_w=router_w[i])
            tm = min(MOE_TM, t)
            gates, pos, row_tok, tile_expert, n_used = _moe_plan(logits[n_ctx:], ne, tm)
            xr = _gather_rows(hn, row_tok + n_ctx, BF16)
            act = _glu(xr, moe_w1[i], moe_w3[i], tm, tile_expert, n_used)
            yr = _matmul_expert(act, moe_w2[i], tm, tile_expert, n_used)
            return _moe_combine(xs, yr, pos, gates, gate2[1], n_ctx, final_g)[None]

    return _final_norm(xs, final_g, n_ctx)[None]
```

```python
import functools
import math

import jax
import jax.numpy as jnp
from jax import lax
from jax.experimental import pallas as pl
from jax.experimental.pallas import tpu as pltpu

GRID_W = 64
SSM_GROUPS = 8
TOP_K = 2
EPS = 1e-6

V7X_VMEM_LIMIT_BYTES = 56 * 1024 * 1024
V7X_VMEM_LIMIT_BIG_BYTES = 60 * 1024 * 1024
LANES = 128
SCAN_CHUNK = 256
ROW_TILE = 256
MOD_TK = 128
MM_TM = 768
MM_TN = 1024
GLU_TN = 512
FFN_TM = 528
DOWN_TN = 1024
MOE_TM = 512
GATHER_ROWS = 256
COMBINE_ROWS = 128

F32 = jnp.float32
BF16 = jnp.bfloat16
HIGHEST = lax.Precision.HIGHEST


def _pick(dim, pref, mult):
    if dim <= pref:
        return dim
    t = (pref // mult) * mult
    while t >= mult:
        if dim % t == 0:
            return t
        t -= mult
    return dim


def _cparams(*sem, vmem=V7X_VMEM_LIMIT_BYTES):
    return pltpu.CompilerParams(dimension_semantics=sem, vmem_limit_bytes=vmem)


def _sigmoid(v):
    return 1.0 / (1.0 + jnp.exp(-v))


def _silu(v):
    return v * _sigmoid(v)


def _softplus(v):
    return jnp.maximum(v, 0.0) + jnp.log1p(jnp.exp(-jnp.abs(v)))


def _log_sigmoid(v):
    return -_softplus(-v)


def _split3(v):
    hi = v.astype(BF16)
    r = v - hi.astype(F32)
    mid = r.astype(BF16)
    lo = (r - mid.astype(F32)).astype(BF16)
    return hi, mid, lo


def _dot_exact(mat01, v, left):
    m = mat01.astype(BF16)
    out = None
    for piece in _split3(v):
        part = (jnp.dot(m, piece, preferred_element_type=F32) if left
                else jnp.dot(piece, m, preferred_element_type=F32))
        out = part if out is None else out + part
    return out


def _mod_kernel(c_ref, w_ref, b_ref, o_ref, acc, *, nk):
    k = pl.program_id(1)

    @pl.when(k == 0)
    def _():
        acc[...] = jnp.zeros_like(acc)

    s_ctx = _silu(c_ref[0])
    s_lat = _silu(c_ref[1])
    tk = w_ref.shape[0]
    for n0 in range(0, w_ref.shape[1], LANES):
        w = w_ref[:, n0:n0 + LANES]
        acc[0, :, n0:n0 + LANES] += jnp.sum((w * s_ctx).reshape(tk // 8, 8, LANES), axis=0)
        acc[1, :, n0:n0 + LANES] += jnp.sum((w * s_lat).reshape(tk // 8, 8, LANES), axis=0)

    @pl.when(k == nk - 1)
    def _():
        o_ref[...] = jnp.zeros_like(o_ref)
        o_ref[0:1, :] = jnp.sum(acc[0], axis=0, keepdims=True) + b_ref[...]
        o_ref[1:2, :] = jnp.sum(acc[1], axis=0, keepdims=True) + b_ref[...]


def _modulation(c_ctx, c_lat, w_mod, b_mod):
    depth, d, n = w_mod.shape
    tk = _pick(d, MOD_TK, 8)
    nk = d // tk
    cb = jnp.broadcast_to(jnp.stack([c_ctx, c_lat])[:, :, None], (2, d, LANES))
    return pl.pallas_call(
        functools.partial(_mod_kernel, nk=nk),
        grid=(depth, nk),
        in_specs=[pl.BlockSpec((2, tk, LANES), lambda l, k: (0, k, 0)),
                  pl.BlockSpec((None, tk, n), lambda l, k: (l, k, 0)),
                  pl.BlockSpec((None, 1, n), lambda l, k: (l, 0, 0))],
        out_specs=pl.BlockSpec((None, 8, n), lambda l, k: (l, 0, 0)),
        out_shape=jax.ShapeDtypeStruct((depth, 8, n), F32),
        scratch_shapes=[pltpu.VMEM((2, 8, n), F32)],
        compiler_params=_cparams("parallel", "arbitrary"),
        name="adaln_modulation",
    )(cb, w_mod, b_mod.reshape(depth, 1, n))


def _norm_kernel(x_ref, g_ref, sc_ref, sh_ref, o_ref):
    x = x_ref[...]
    r = lax.rsqrt(jnp.mean(x * x, axis=-1, keepdims=True) + EPS)
    o_ref[...] = (x * r * g_ref[...] * (1.0 + sc_ref[...]) + sh_ref[...]).astype(o_ref.dtype)


def _norm_router_kernel(x_ref, g_ref, sc_ref, sh_ref, rw_ref, o_ref, lg_ref):
    x = x_ref[...]
    r = lax.rsqrt(jnp.mean(x * x, axis=-1, keepdims=True) + EPS)
    h = x * r * g_ref[...] * (1.0 + sc_ref[...]) + sh_ref[...]
    o_ref[...] = h
    lg_ref[...] = jnp.dot(h, rw_ref[...], preferred_element_type=F32, precision=HIGHEST)


def _norm_mod(xs, g, scale, shift, n_ctx, router_w=None):
    s, d = xs.shape
    r = _pick(math.gcd(n_ctx, s - n_ctx), ROW_TILE, 8)
    nct = n_ctx // r
    row_spec = pl.BlockSpec((r, d), lambda i: (i, 0))
    mod_spec = pl.BlockSpec((None, 1, d), lambda i: (jnp.where(i < nct, 0, 1), 0, 0))
    in_specs = [row_spec, pl.BlockSpec((1, d), lambda i: (0, 0)), mod_spec, mod_spec]
    if router_w is None:
        return pl.pallas_call(
            _norm_kernel, grid=(s // r,), in_specs=in_specs, out_specs=row_spec,
            out_shape=jax.ShapeDtypeStruct((s, d), BF16),
            compiler_params=_cparams("parallel"), name="rmsnorm_modulate",
        )(xs, g.reshape(1, d), scale, shift)
    ne = router_w.shape[1]
    rw = jnp.zeros((d, LANES), F32).at[:, :ne].set(router_w)
    return pl.pallas_call(
        _norm_router_kernel, grid=(s // r,),
        in_specs=in_specs + [pl.BlockSpec((d, LANES), lambda i: (0, 0))],
        out_specs=[row_spec, pl.BlockSpec((r, LANES), lambda i: (i, 0))],
        out_shape=[jax.ShapeDtypeStruct((s, d), F32), jax.ShapeDtypeStruct((s, LANES), F32)],
        compiler_params=_cparams("parallel"), name="rmsnorm_modulate_router",
    )(xs, g.reshape(1, d), scale, shift, rw)


def _mm_nt_kernel(a_ref, wt_ref, o_ref):
    o_ref[...] = lax.dot_general(a_ref[...], wt_ref[...], (((1,), (1,)), ((), ())),
                                 preferred_element_type=F32).astype(o_ref.dtype)


def _matmul_nt(a, wt, out_dtype):
    m, k = a.shape
    n = wt.shape[0]
    tm = _pick(m, MM_TM, 8)
    tn = _pick(n, MM_TN, LANES)
    return pl.pallas_call(
        _mm_nt_kernel, grid=(m // tm, n // tn),
        in_specs=[pl.BlockSpec((tm, k), lambda i, j: (i, 0)), pl.BlockSpec((tn, k), lambda i, j: (j, 0))],
        out_specs=pl.BlockSpec((tm, tn), lambda i, j: (i, j)),
        out_shape=jax.ShapeDtypeStruct((m, n), out_dtype),
        compiler_params=_cparams("parallel", "parallel"), name="matmul_nt",
    )(a, wt)


def _row_gate(gate_ref, row0, tm, n_ctx):
    rows = row0 + lax.broadcasted_iota(jnp.int32, (tm, 1), 0)
    return jnp.where(rows < n_ctx, gate_ref[0], gate_ref[1])


def _mm_res_kernel(a_ref, w_ref, res_ref, gate_ref, o_ref, acc_ref, *, nk, tm, n_ctx, k_tail):
    k = pl.program_id(2)

    @pl.when(k == 0)
    def _():
        acc_ref[...] = jnp.zeros_like(acc_ref)

    if k_tail:
        @pl.when(k < nk - 1)
        def _():
            acc_ref[...] += jnp.dot(a_ref[...], w_ref[...], preferred_element_type=F32)

        @pl.when(k == nk - 1)
        def _():
            w = w_ref[...]
            w = jnp.where(lax.broadcasted_iota(jnp.int32, w.shape, 0) < k_tail, w, jnp.zeros_like(w))
            acc_ref[...] += jnp.dot(a_ref[...], w, preferred_element_type=F32)
    else:
        acc_ref[...] += jnp.dot(a_ref[...], w_ref[...], preferred_element_type=F32)

    @pl.when(k == nk - 1)
    def _():
        gate = _row_gate(gate_ref, pl.program_id(0) * tm, tm, n_ctx)
        o_ref[...] = res_ref[...] + gate * acc_ref[...]


def _matmul_residual(a, w, res, gate, n_ctx, tk_pref):
    m, kdim = a.shape
    n = w.shape[1]
    tm = _pick(m, MM_TM, 8)
    tn = _pick(n, MM_TN, LANES)
    tk = _pick(kdim, tk_pref, LANES)
    if w.shape[0] <= (kdim // tk - 1) * tk:
        tk = kdim
    nk = kdim // tk
    k_tail = w.shape[0] - (nk - 1) * tk
    assert 0 < k_tail <= tk
    return pl.pallas_call(
        functools.partial(_mm_res_kernel, nk=nk, tm=tm, n_ctx=n_ctx, k_tail=0 if k_tail == tk else k_tail),
        grid=(m // tm, n // tn, nk),
        in_specs=[pl.BlockSpec((tm, tk), lambda i, j, k: (i, k)),
                  pl.BlockSpec((tk, tn), lambda i, j, k: (k, j)),
                  pl.BlockSpec((tm, tn), lambda i, j, k: (i, j)),
                  pl.BlockSpec((2, 1, tn), lambda i, j, k: (0, 0, j))],
        out_specs=pl.BlockSpec((tm, tn), lambda i, j, k: (i, j)),
        out_shape=jax.ShapeDtypeStruct((m, n), F32),
        scratch_shapes=[pltpu.VMEM((tm, tn), F32)],
        compiler_params=_cparams("parallel", "parallel", "arbitrary"), name="matmul_residual",
    )(a, w, res, gate)


def _mm_res_ws_kernel(a_ref, w_ref, res_ref, gate_ref, o_ref, wbuf, *, tm, n_ctx):
    i = pl.program_id(1)

    @pl.when(i == 0)
    def _():
        wbuf[...] = w_ref[...].astype(BF16)

    acc = jnp.dot(a_ref[...], wbuf[...], preferred_element_type=F32)
    o_ref[...] = res_ref[...] + _row_gate(gate_ref, i * tm, tm, n_ctx) * acc


def _matmul_residual_stationary(a, w, layer, res, gate, n_ctx):
    m, k = a.shape
    n = w.shape[2]
    tm = _pick(m, MM_TM, 8)
    tn = _pick(n, GLU_TN, LANES)
    return pl.pallas_call(
        functools.partial(_mm_res_ws_kernel, tm=tm, n_ctx=n_ctx),
        grid=(n // tn, m // tm),
        in_specs=[pl.BlockSpec((tm, k), lambda j, i: (i, 0)),
                  pl.BlockSpec((None, k, tn), lambda j, i: (layer, 0, j)),
                  pl.BlockSpec((tm, tn), lambda j, i: (i, j)),
                  pl.BlockSpec((2, 1, tn), lambda j, i: (0, 0, j))],
        out_specs=pl.BlockSpec((tm, tn), lambda j, i: (i, j)),
        out_shape=jax.ShapeDtypeStruct((m, n), F32),
        scratch_shapes=[pltpu.VMEM((k, tn), BF16)],
        compiler_params=_cparams("arbitrary", "arbitrary"), name="matmul_residual_ws",
    )(a, w, res, gate)


def _stationary_plan(tile_expert, n_used, n_experts, nj):
    n_tiles = tile_expert.shape[0]
    tile_ids = jnp.arange(n_tiles, dtype=jnp.int32)
    cnt = jnp.sum(((tile_expert[:, None] == jnp.arange(n_experts)[None, :])
                   & (tile_ids[:, None] < n_used)).astype(jnp.int32), axis=0)
    seg_end = jnp.cumsum(cnt)
    seg_start = seg_end - cnt
    step = jnp.arange(n_tiles * nj, dtype=jnp.int32)
    used_steps = n_used * nj
    e = jnp.minimum(jnp.searchsorted(seg_end * nj, step, side='right'), n_experts - 1).astype(jnp.int32)
    seg_len = jnp.maximum(cnt[e], 1)
    local = jnp.maximum(step - nj * seg_start[e], 0)
    j = local // seg_len
    i = seg_start[e] + local % seg_len
    used = step < used_steps
    last = used_steps - 1
    fill = step - used_steps
    pick = lambda arr: jnp.where(used, arr, arr[last]).astype(jnp.int32)
    out_i = jnp.where(used, i, n_used + fill // nj).astype(jnp.int32)
    out_j = jnp.where(used, j, fill % nj).astype(jnp.int32)
    return pick(e), pick(j), pick(i), out_i, out_j, used.astype(jnp.int32)


def _stationary_flags(e_ref, jw_ref, used_ref):
    s = pl.program_id(0)
    prev = jnp.maximum(s - 1, 0)
    fresh = jnp.logical_or(s == 0, jnp.logical_or(e_ref[s] != e_ref[prev], jw_ref[s] != jw_ref[prev]))
    used = used_ref[s] > 0
    return used, jnp.logical_and(used, fresh)


def _glu_kernel(e_ref, jw_ref, ia_ref, io_ref, jo_ref, used_ref, a_ref, w1_ref, w3_ref, o_ref, w1b, w3b,
                *, n_valid, ragged):
    used, fresh = _stationary_flags(e_ref, jw_ref, used_ref)

    @pl.when(fresh)
    def _():
        w1 = w1_ref[...]
        w3 = w3_ref[...]
        if ragged:
            col = jw_ref[pl.program_id(0)] * w1.shape[1] + lax.broadcasted_iota(jnp.int32, w1.shape, 1)
            w1 = jnp.where(col < n_valid, w1, 0.0)
            w3 = jnp.where(col < n_valid, w3, 0.0)
        w1b[...] = w1.astype(BF16)
        w3b[...] = w3.astype(BF16)

    @pl.when(used)
    def _():
        a = a_ref[...]
        h1 = jnp.dot(a, w1b[...], preferred_element_type=F32)
        h3 = jnp.dot(a, w3b[...], preferred_element_type=F32)
        o_ref[...] = (_silu(h1) * h3).astype(o_ref.dtype)

    @pl.when(jnp.logical_not(used))
    def _():
        o_ref[...] = jnp.zeros_like(o_ref)


def _stationary_specs(tm, k, tn):
    a_spec = pl.BlockSpec((tm, k), lambda s, e, jw, ia, io, jo, u: (ia[s], 0))
    w_spec = pl.BlockSpec((None, k, tn), lambda s, e, jw, ia, io, jo, u: (e[s], 0, jw[s]))
    o_spec = pl.BlockSpec((tm, tn), lambda s, e, jw, ia, io, jo, u: (io[s], jo[s]))
    return a_spec, w_spec, o_spec


def _glu(a, w1, w3, tm, tile_expert, n_used):
    m, k = a.shape
    ne, _, n = w1.shape
    tn = GLU_TN if n % LANES == 0 and n > GLU_TN else n
    nj = pl.cdiv(n, tn)
    plan = _stationary_plan(tile_expert, n_used, ne, nj)
    a_spec, w_spec, o_spec = _stationary_specs(tm, k, tn)
    return pl.pallas_call(
        functools.partial(_glu_kernel, n_valid=n, ragged=n % tn != 0),
        grid_spec=pltpu.PrefetchScalarGridSpec(
            num_scalar_prefetch=6, grid=(plan[0].shape[0],),
            in_specs=[a_spec, w_spec, w_spec], out_specs=o_spec,
            scratch_shapes=[pltpu.VMEM((k, tn), BF16)] * 2),
        out_shape=jax.ShapeDtypeStruct((m, nj * tn), BF16),
        compiler_params=_cparams("arbitrary", vmem=V7X_VMEM_LIMIT_BIG_BYTES), name="swiglu_up",
    )(*plan, a, w1, w3)


def _mm_expert_kernel(e_ref, jw_ref, ia_ref, io_ref, jo_ref, used_ref, a_ref, w_ref, o_ref, wb):
    used, fresh = _stationary_flags(e_ref, jw_ref, used_ref)

    @pl.when(fresh)
    def _():
        wb[...] = w_ref[...].astype(BF16)

    @pl.when(used)
    def _():
        o_ref[...] = jnp.dot(a_ref[...], wb[...], preferred_element_type=F32).astype(o_ref.dtype)

    @pl.when(jnp.logical_not(used))
    def _():
        o_ref[...] = jnp.zeros_like(o_ref)


def _matmul_expert(a, w, tm, tile_expert, n_used):
    m, k = a.shape
    ne, _, n = w.shape
    tn = _pick(n, DOWN_TN, LANES)
    nj = n // tn
    plan = _stationary_plan(tile_expert, n_used, ne, nj)
    a_spec, w_spec, o_spec = _stationary_specs(tm, k, tn)
    return pl.pallas_call(
        _mm_expert_kernel,
        grid_spec=pltpu.PrefetchScalarGridSpec(
            num_scalar_prefetch=6, grid=(plan[0].shape[0],),
            in_specs=[a_spec, w_spec], out_specs=o_spec,
            scratch_shapes=[pltpu.VMEM((k, tn), BF16)]),
        out_shape=jax.ShapeDtypeStruct((m, n), F32),
        compiler_params=_cparams("arbitrary", vmem=V7X_VMEM_LIMIT_BIG_BYTES), name="expert_down",
    )(*plan, a, w)


def _chunk_maps(nct, nlt):
    def fwd(s):
        return s

    def bwd(s):
        return jnp.where(s < nct, nct - 1 - s, 2 * nct + nlt - 1 - s)

    return fwd, bwd


def _tri(l, lower):
    r = lax.broadcasted_iota(jnp.int32, (l, l), 0)
    c = lax.broadcasted_iota(jnp.int32, (l, l), 1)
    return (r >= c) if lower else (r <= c)


def _mlstm_direction(q, k, v, g, gt, c_ref, n_ref, m_ref, o_ref, *, fwd, scale):
    l = q.shape[0]
    ci, cf = (0, 2) if fwd else (1, 3)
    low = _tri(l, True).astype(F32)
    up = _tri(l, False).astype(F32)
    b_col = _dot_exact(low if fwd else up, _log_sigmoid(g), True)[:, cf:cf + 1]
    b_row = _dot_exact(up if fwd else low, _log_sigmoid(gt), False)[cf:cf + 1, :]
    i_col = g[:, ci:ci + 1]
    i_row = gt[ci:ci + 1, :]
    causal = _tri(l, fwd)
    logw = jnp.where(causal, b_col - b_row + i_row, -jnp.inf)
    m_intra = jnp.max(logw, axis=-1, keepdims=True)
    m_st = m_ref[...]
    inter_log = b_col + m_st
    m_t = jnp.maximum(inter_log, m_intra)
    a_inter = jnp.exp(inter_log - m_t)
    ks = (k.astype(F32) * scale)
    ksb = ks.astype(BF16)
    qk = lax.dot_general(q, ksb, (((1,), (1,)), ((), ())), preferred_element_type=F32)
    s = qk * jnp.exp(logw - m_t)
    c_st = c_ref[...]
    n_st = n_ref[...]
    num = a_inter * jnp.dot(q, c_st.astype(BF16), preferred_element_type=F32) \
        + jnp.dot(s.astype(BF16), v, preferred_element_type=F32)
    den = a_inter * jnp.sum(q.astype(F32) * n_st, axis=-1, keepdims=True) + jnp.sum(s, axis=-1, keepdims=True)
    o_ref[...] = (num / jnp.maximum(jnp.abs(den), jnp.exp(-m_t))).astype(o_ref.dtype)
    b_last = b_col[l - 1:l, :] if fwd else b_col[0:1, :]
    g_col = b_last - b_col + i_col
    m_new = jnp.maximum(b_last + m_st, jnp.max(g_col, axis=0, keepdims=True))
    kw = ks * jnp.exp(g_col - m_new)
    dec = jnp.exp(b_last + m_st - m_new)
    c_ref[...] = dec * c_st + lax.dot_general(kw.astype(BF16), v, (((0,), (0,)), ((), ())),
                                              preferred_element_type=F32)
    n_ref[...] = dec * n_st + jnp.sum(kw, axis=0, keepdims=True)
    m_ref[...] = m_new


def _mlstm_kernel(qf, kf, vf, gf, gtf, qb, kb, vb, gb, gtb, bias, bias_t, hf, hb,
                  cf, nf, mf, cb, nb, mb, *, scale):
    @pl.when(pl.program_id(1) == 0)
    def _():
        for ref in (cf, nf, mf, cb, nb, mb):
            ref[...] = jnp.zeros_like(ref)

    _mlstm_direction(qf[...], kf[...], vf[...], gf[...] + bias[...], gtf[...] + bias_t[...],
                     cf, nf, mf, hf, fwd=True, scale=scale)
    _mlstm_direction(qb[...], kb[...], vb[...], gb[...] + bias[...], gtb[...] + bias_t[...],
                     cb, nb, mb, hb, fwd=False, scale=scale)


def _mlstm(big, gates, gate_b, nh, dqk, dv, n_ctx, col_q, col_k, col_v):
    s = big.shape[0]
    l = _pick(math.gcd(n_ctx, s - n_ctx), SCAN_CHUNK, 8)
    nct, nlt = n_ctx // l, (s - n_ctx) // l
    fwd, bwd = _chunk_maps(nct, nlt)
    gh = gates.reshape(s, 4, nh).transpose(2, 0, 1)
    g_cols = jnp.zeros((nh, s, LANES), F32).at[:, :, :4].set(gh)
    g_rows = jnp.zeros((nh, 8, s), F32).at[:, :4, :].set(gh.transpose(0, 2, 1))
    bh = gate_b.reshape(4, nh).T
    b_cols = jnp.zeros((nh, 1, LANES), F32).at[:, 0, :4].set(bh)
    b_rows = jnp.zeros((nh, 8, 1), F32).at[:, :4, 0].set(bh)
    qo, ko, vo = col_q // dqk, col_k // dqk, col_v // dv

    def specs(cm):
        return [pl.BlockSpec((l, dqk), lambda h, t: (cm(t), qo + h)),
                pl.BlockSpec((l, dqk), lambda h, t: (cm(t), ko + h)),
                pl.BlockSpec((l, dv), lambda h, t: (cm(t), vo + h)),
                pl.BlockSpec((None, l, LANES), lambda h, t: (h, cm(t), 0)),
                pl.BlockSpec((None, 8, l), lambda h, t: (h, 0, cm(t)))]

    out_f = pl.BlockSpec((l, dv), lambda h, t: (fwd(t), h))
    out_b = pl.BlockSpec((l, dv), lambda h, t: (bwd(t), h))
    state = [pltpu.VMEM((dqk, dv), F32), pltpu.VMEM((1, dqk), F32), pltpu.VMEM((1, 1), F32)]
    return pl.pallas_call(
        functools.partial(_mlstm_kernel, scale=dqk ** -0.5),
        grid=(nh, nct + nlt),
        in_specs=specs(fwd) + specs(bwd) + [pl.BlockSpec((None, 1, LANES), lambda h, t: (h, 0, 0)),
                                             pl.BlockSpec((None, 8, 1), lambda h, t: (h, 0, 0))],
        out_specs=[out_f, out_b],
        out_shape=[jax.ShapeDtypeStruct((s, nh * dv), BF16)] * 2,
        scratch_shapes=state + state,
        compiler_params=_cparams("parallel", "arbitrary"), name="mlstm_scan",
    )(big, big, big, g_cols, g_rows, big, big, big, g_cols, g_rows, b_cols, b_rows)


def _conv_kernel(u_ref, prev_ref, next_ref, w_ref, b_ref, o_ref, *, nct, nt, width, halo):
    i = pl.program_id(0)
    first = jnp.logical_or(i == 0, i == nct)
    last = jnp.logical_or(i == nct - 1, i == nt - 1)
    r = u_ref.shape[0]
    prev = jnp.where(first, 0.0, prev_ref[...].astype(F32))
    nxt = jnp.where(last, 0.0, next_ref[...].astype(F32))
    ext = jnp.concatenate([prev, u_ref[...].astype(F32), nxt], axis=0)
    w = w_ref[...]
    acc = b_ref[...] + w[0:1, :] * ext[halo - width // 2: halo - width // 2 + r]
    for j in range(1, width):
        off = halo - width // 2 + j
        acc = acc + w[j:j + 1, :] * ext[off: off + r]
    o_ref[...] = _silu(acc).astype(o_ref.dtype)


def _dwconv_silu(u, conv_w, conv_b, n_ctx):
    s, ch = u.shape
    width = conv_w.shape[0]
    halo = 16
    r = _pick(math.gcd(n_ctx, s - n_ctx), ROW_TILE, halo)
    tc = _pick(ch, 1024, LANES)
    nt, nct, rb = s // r, n_ctx // r, r // halo
    w8 = jnp.zeros((8, ch), F32).at[:width].set(conv_w)
    return pl.pallas_call(
        functools.partial(_conv_kernel, nct=nct, nt=nt, width=width, halo=halo),
        grid=(nt, ch // tc),
        in_specs=[pl.BlockSpec((r, tc), lambda i, j: (i, j)),
                  pl.BlockSpec((halo, tc), lambda i, j: (jnp.maximum(i * rb - 1, 0), j)),
                  pl.BlockSpec((halo, tc), lambda i, j: (jnp.minimum((i + 1) * rb, s // halo - 1), j)),
                  pl.BlockSpec((8, tc), lambda i, j: (0, j)),
                  pl.BlockSpec((1, tc), lambda i, j: (0, j))],
        out_specs=pl.BlockSpec((r, tc), lambda i, j: (i, j)),
        out_shape=jax.ShapeDtypeStruct((s, ch), BF16),
        compiler_params=_cparams("parallel", "parallel"), name="dwconv_silu",
    )(u, u, u, w8, conv_b.reshape(1, ch))


def _per_channel(cols, off, hg, p, lane):
    out = jnp.broadcast_to(cols[:, off + hg - 1:off + hg], lane.shape)
    for h in range(hg - 2, -1, -1):
        out = jnp.where(lane < (h + 1) * p, jnp.broadcast_to(cols[:, off + h:off + h + 1], lane.shape), out)
    return out


def _ssd_direction(x, bm, cm, dtv, da, da_t, st_ref, *, fwd, hg, p):
    l = x.shape[0]
    off = 0 if fwd else hg
    low = _tri(l, True)
    up = _tri(l, False)
    acum_col = _dot_exact(low if fwd else up, da, True)
    acum_row = _dot_exact(up if fwd else low, da_t, False)
    lane = lax.broadcasted_iota(jnp.int32, x.shape, 1)
    acx = _per_channel(acum_col, off, hg, p, lane)
    xdt = x.astype(F32) * _per_channel(dtv, off, hg, p, lane)
    xdt_b = xdt.astype(BF16)
    causal = _tri(l, fwd)
    cb = lax.dot_general(cm, bm, (((1,), (1,)), ((), ())), preferred_element_type=F32)
    y = None
    for h in range(hg):
        seg = acum_col[:, off + h:off + h + 1] - acum_row[off + h:off + h + 1, :]
        mix = (cb * jnp.exp(jnp.where(causal, seg, -jnp.inf))).astype(BF16)
        own = jnp.logical_and(lane >= h * p, lane < (h + 1) * p)
        part = jnp.dot(mix, jnp.where(own, xdt_b, jnp.zeros_like(xdt_b)), preferred_element_type=F32)
        y = part if y is None else y + part
    st = st_ref[...]
    y = y + jnp.dot(cm, st.astype(BF16), preferred_element_type=F32) * jnp.exp(acx)
    total = acx[l - 1:l, :] if fwd else acx[0:1, :]
    xw = (xdt * jnp.exp(total - acx)).astype(BF16)
    st_ref[...] = jnp.exp(total) * st + lax.dot_general(bm, xw, (((0,), (0,)), ((), ())),
                                                        preferred_element_type=F32)
    return y


def _ssd_kernel(xf, bf, cf, dtf, dttf, xb, bb, cb, dtb, dttb, a_c, a_r, db_c, db_r, dsk,
                yf, yb, stf, stb, *, hg, p):
    @pl.when(pl.program_id(1) == 0)
    def _():
        stf[...] = jnp.zeros_like(stf)
        stb[...] = jnp.zeros_like(stb)

    def steps(dt_ref, dtt_ref):
        dtv = _softplus(dt_ref[...] + db_c[...])
        da_t = _softplus(dtt_ref[...] + db_r[...]) * a_r[...]
        return dtv, dtv * a_c[...], da_t

    x = xf[...]
    y = _ssd_direction(x, bf[...], cf[...], *steps(dtf, dttf), stf, fwd=True, hg=hg, p=p)
    yf[...] = (y + dsk[...] * x.astype(F32)).astype(yf.dtype)
    y = _ssd_direction(xb[...], bb[...], cb[...], *steps(dtb, dttb), stb, fwd=False, hg=hg, p=p)
    yb[...] = y.astype(yb.dtype)


def _ssd(xbc, dt_raw, a_log, dt_bias, d_skip, n_ctx, sw, n_state):
    s = xbc.shape[0]
    h = a_log.shape[1]
    g = SSM_GROUPS
    hg, p = h // g, sw // h
    gw = hg * p
    l = _pick(math.gcd(n_ctx, s - n_ctx), SCAN_CHUNK, 8)
    nct, nlt = n_ctx // l, (s - n_ctx) // l
    fwd, bwd = _chunk_maps(nct, nlt)
    dtg = dt_raw.reshape(s, 2, g, hg).transpose(2, 0, 1, 3).reshape(g, s, 2 * hg)
    dt_cols = jnp.zeros((g, s, LANES), F32).at[:, :, :2 * hg].set(dtg)
    assert 2 * hg <= 8 and sw % n_state == 0
    dt_rows = jnp.zeros((g, 8, s), F32).at[:, :2 * hg, :].set(dtg.transpose(0, 2, 1))
    a = (-jnp.exp(a_log.astype(F32))).reshape(2, g, hg).transpose(1, 0, 2).reshape(g, 2 * hg)
    db = dt_bias.reshape(2, g, hg).transpose(1, 0, 2).reshape(g, 2 * hg)
    a_c = jnp.zeros((g, 1, LANES), F32).at[:, 0, :2 * hg].set(a)
    a_r = jnp.zeros((g, 8, 1), F32).at[:, :2 * hg, 0].set(a)
    db_c = jnp.zeros((g, 1, LANES), F32).at[:, 0, :2 * hg].set(db)
    db_r = jnp.zeros((g, 8, 1), F32).at[:, :2 * hg, 0].set(db)
    dsk = jnp.repeat(d_skip, p).reshape(g, 1, gw)
    xo, bo, co = 0, sw // n_state, (sw + g * n_state) // n_state

    def specs(cm):
        return [pl.BlockSpec((l, gw), lambda gi, t: (cm(t), gi)),
                pl.BlockSpec((l, n_state), lambda gi, t: (cm(t), bo + gi)),
                pl.BlockSpec((l, n_state), lambda gi, t: (cm(t), co + gi)),
                pl.BlockSpec((None, l, LANES), lambda gi, t: (gi, cm(t), 0)),
                pl.BlockSpec((None, 8, l), lambda gi, t: (gi, 0, cm(t)))]

    par_c = pl.BlockSpec((None, 1, LANES), lambda gi, t: (gi, 0, 0))
    par_r = pl.BlockSpec((None, 8, 1), lambda gi, t: (gi, 0, 0))
    return pl.pallas_call(
        functools.partial(_ssd_kernel, hg=hg, p=p),
        grid=(g, nct + nlt),
        in_specs=specs(fwd) + specs(bwd) + [par_c, par_r, par_c, par_r,
                                             pl.BlockSpec((None, 1, gw), lambda gi, t: (gi, 0, 0))],
        out_specs=[pl.BlockSpec((l, gw), lambda gi, t: (fwd(t), gi)),
                   pl.BlockSpec((l, gw), lambda gi, t: (bwd(t), gi))],
        out_shape=[jax.ShapeDtypeStruct((s, sw), BF16)] * 2,
        scratch_shapes=[pltpu.VMEM((n_state, gw), F32)] * 2,
        compiler_params=_cparams("parallel", "arbitrary"), name="ssd_scan",
    )(xbc, xbc, xbc, dt_cols, dt_rows, xbc, xbc, xbc, dt_cols, dt_rows, a_c, a_r, db_c, db_r, dsk)


def _finish_kernel(hf, hb, o, yf, yb, z, mg, sg, out, *, nh, dv):
    mw = nh * dv
    h = hf[...].astype(F32) + hb[...].astype(F32)
    gate = _sigmoid(o[...].astype(F32))
    mgv = mg[...]
    for k in range(nh):
        hk = h[:, k * dv:(k + 1) * dv]
        r = lax.rsqrt(jnp.mean(hk * hk, axis=-1, keepdims=True) + EPS)
        out[:, k * dv:(k + 1) * dv] = (hk * r * mgv[:, k * dv:(k + 1) * dv]
                                       * gate[:, k * dv:(k + 1) * dv]).astype(out.dtype)
    y = (yf[...].astype(F32) + yb[...].astype(F32)) * _silu(z[...].astype(F32))
    r = lax.rsqrt(jnp.mean(y * y, axis=-1, keepdims=True) + EPS)
    out[:, mw:] = (y * r * sg[...]).astype(out.dtype)


def _finish(hf, hb, big, yf, yb, mlstm_g, ssm_g, nh, dv, col_o, col_z):
    s, mw = hf.shape
    sw = yf.shape[1]
    r = _pick(s, ROW_TILE, 8)
    return pl.pallas_call(
        functools.partial(_finish_kernel, nh=nh, dv=dv),
        grid=(s // r,),
        in_specs=[pl.BlockSpec((r, mw), lambda i: (i, 0)), pl.BlockSpec((r, mw), lambda i: (i, 0)),
                  pl.BlockSpec((r, mw), lambda i: (i, col_o // mw)),
                  pl.BlockSpec((r, sw), lambda i: (i, 0)), pl.BlockSpec((r, sw), lambda i: (i, 0)),
                  pl.BlockSpec((r, sw), lambda i: (i, col_z // sw)),
                  pl.BlockSpec((1, mw), lambda i: (0, 0)), pl.BlockSpec((1, sw), lambda i: (0, 0))],
        out_specs=pl.BlockSpec((r, mw + sw), lambda i: (i, 0)),
        out_shape=jax.ShapeDtypeStruct((s, mw + sw), BF16),
        compiler_params=_cparams("parallel"), name="mixer_finish",
    )(hf, hb, big, yf, yb, big, mlstm_g.reshape(1, mw), ssm_g.reshape(1, sw))


def _row_copy(src_ref, row, dst_ref, k, sem):
    return pltpu.make_async_copy(src_ref.at[pl.ds(row, 1)], dst_ref.at[pl.ds(k, 1)], sem)


def _gather_kernel(idx_ref, nxt_ref, src_ref, o_ref, buf, sem):
    i = pl.program_id(0)
    n_rows = o_ref.shape[0]
    slot = lax.rem(i, 2)

    def issue(ids_ref, s):
        def body(k, carry):
            row = ids_ref[0, 0, k]

            @pl.when(row >= 0)
            def _():
                _row_copy(src_ref, row, buf.at[s], k, sem.at[s]).start()
            return carry
        lax.fori_loop(0, n_rows, body, 0)

    @pl.when(i == 0)
    def _():
        buf[...] = jnp.zeros_like(buf)
        issue(idx_ref, 0)

    @pl.when(i + 1 < pl.num_programs(0))
    def _():
        issue(nxt_ref, 1 - slot)

    def wait(k, carry):
        @pl.when(idx_ref[0, 0, k] >= 0)
        def _():
            _row_copy(src_ref, 0, buf.at[slot], k, sem.at[slot]).wait()
        return carry

    lax.fori_loop(0, n_rows, wait, 0)
    o_ref[...] = buf[slot].astype(o_ref.dtype)


def _gather_rows(src, idx, out_dtype):
    n = idx.shape[0]
    d = src.shape[1]
    r = _pick(n, GATHER_ROWS, 8)
    nt = n // r
    ids = idx.reshape(nt, 1, r)
    return pl.pallas_call(
        _gather_kernel, grid=(nt,),
        in_specs=[pl.BlockSpec((1, 1, r), lambda i: (i, 0, 0), memory_space=pltpu.SMEM),
                  pl.BlockSpec((1, 1, r), lambda i: (jnp.minimum(i + 1, nt - 1), 0, 0), memory_space=pltpu.SMEM),
                  pl.BlockSpec(memory_space=pl.ANY)],
        out_specs=pl.BlockSpec((r, d), lambda i: (i, 0)),
        out_shape=jax.ShapeDtypeStruct((n, d), out_dtype),
        scratch_shapes=[pltpu.VMEM((2, r, d), src.dtype), pltpu.SemaphoreType.DMA((2,))],
        compiler_params=_cparams("arbitrary"), name="moe_gather",
    )(ids, ids, src)


def _combine_kernel(p0_ref, p1_ref, q0_ref, q1_ref, yr_ref, x_ref, gk_ref, gate_ref, fg_ref, o_ref, buf, sem,
                    *, final_norm):
    i = pl.program_id(0)
    n_rows = x_ref.shape[0]
    slot = lax.rem(i, 2)

    def issue(a_ref, b_ref, s):
        def body(k, carry):
            _row_copy(yr_ref, a_ref[0, 0, k], buf.at[s, 0], k, sem.at[s]).start()
            _row_copy(yr_ref, b_ref[0, 0, k], buf.at[s, 1], k, sem.at[s]).start()
            return carry
        lax.fori_loop(0, n_rows, body, 0)

    @pl.when(i == 0)
    def _():
        issue(p0_ref, p1_ref, 0)

    @pl.when(i + 1 < pl.num_programs(0))
    def _():
        issue(q0_ref, q1_ref, 1 - slot)

    def wait(k, carry):
        _row_copy(yr_ref, 0, buf.at[slot, 0], k, sem.at[slot]).wait()
        _row_copy(yr_ref, 0, buf.at[slot, 1], k, sem.at[slot]).wait()
        return carry

    lax.fori_loop(0, n_rows, wait, 0)
    gk = gk_ref[...]
    x = x_ref[...] + gate_ref[...] * (gk[:, 0:1] * buf[slot, 0] + gk[:, 1:2] * buf[slot, 1])
    if final_norm:
        x = x * lax.rsqrt(jnp.mean(x * x, axis=-1, keepdims=True) + EPS) * fg_ref[...]
    o_ref[...] = x


def _moe_combine(xs, yr, pos, gates, gate_mod, n_ctx, final_g):
    s, d = xs.shape
    t = s - n_ctx
    r = _pick(math.gcd(n_ctx, t), COMBINE_ROWS, 8)
    nct, nt = n_ctx // r, t // r
    gk = jnp.zeros((t, LANES), F32).at[:, :TOP_K].set(gates)
    p0 = pos[:, 0].reshape(nt, 1, r)
    p1 = pos[:, 1].reshape(nt, 1, r)
    cur = pl.BlockSpec((1, 1, r), lambda i: (i, 0, 0), memory_space=pltpu.SMEM)
    nxt = pl.BlockSpec((1, 1, r), lambda i: (jnp.minimum(i + 1, nt - 1), 0, 0), memory_space=pltpu.SMEM)
    fg = jnp.ones((1, d), F32) if final_g is None else final_g.reshape(1, d).astype(F32)
    return pl.pallas_call(
        functools.partial(_combine_kernel, final_norm=final_g is not None), grid=(nt,),
        in_specs=[cur, cur, nxt, nxt,
                  pl.BlockSpec(memory_space=pl.ANY),
                  pl.BlockSpec((r, d), lambda i: (i + nct, 0)),
                  pl.BlockSpec((r, LANES), lambda i: (i, 0)),
                  pl.BlockSpec((1, d), lambda i: (0, 0)),
                  pl.BlockSpec((1, d), lambda i: (0, 0))],
        out_specs=pl.BlockSpec((r, d), lambda i: (i, 0)),
        out_shape=jax.ShapeDtypeStruct((t, d), F32),
        scratch_shapes=[pltpu.VMEM((2, 2, r, d), F32), pltpu.SemaphoreType.DMA((2,))],
        compiler_params=_cparams("arbitrary"), name="moe_combine",
    )(p0, p1, p0, p1, yr, xs, gk, gate_mod, fg)


def _moe_plan(logits, ne, tm):
    t = logits.shape[0]
    top_val, top_idx = lax.top_k(logits[:, :ne], TOP_K)
    gates = jax.nn.softmax(top_val, axis=-1)
    n_assign = t * TOP_K
    e_flat = top_idx.reshape(-1)
    onehot = (e_flat[:, None] == jnp.arange(ne)[None, :]).astype(jnp.int32)
    rank = jnp.take_along_axis(jnp.cumsum(onehot, axis=0) - onehot, e_flat[:, None], axis=1)[:, 0]
    counts = jnp.sum(onehot, axis=0)
    padded = (counts + tm - 1) // tm * tm
    pend = jnp.cumsum(padded)
    pstart = pend - padded
    pos = (pstart[e_flat] + rank).astype(jnp.int32)
    n_tiles = (n_assign + ne * (tm - 1)) // tm
    n_rows = n_tiles * tm
    tok = jnp.repeat(jnp.arange(t, dtype=jnp.int32), TOP_K)
    row_tok = jnp.full((n_rows,), -1, jnp.int32).at[pos].set(tok)
    tile_start = jnp.arange(n_tiles, dtype=jnp.int32) * tm
    tile_expert = jnp.minimum(jnp.searchsorted(pend, tile_start, side='right'), ne - 1).astype(jnp.int32)
    n_used = (pend[-1] // tm).astype(jnp.int32)
    return gates, pos.reshape(t, TOP_K), row_tok, tile_expert, n_used


def _final_norm_kernel(x_ref, g_ref, o_ref):
    x = x_ref[...]
    o_ref[...] = x * lax.rsqrt(jnp.mean(x * x, axis=-1, keepdims=True) + EPS) * g_ref[...]


def _final_norm(xs, g, n_ctx):
    s, d = xs.shape
    t = s - n_ctx
    r = _pick(math.gcd(n_ctx, t), ROW_TILE, 8)
    nct = n_ctx // r
    return pl.pallas_call(
        _final_norm_kernel, grid=(t // r,),
        in_specs=[pl.BlockSpec((r, d), lambda i: (i + nct, 0)), pl.BlockSpec((1, d), lambda i: (0, 0))],
        out_specs=pl.BlockSpec((r, d), lambda i: (i, 0)),
        out_shape=jax.ShapeDtypeStruct((t, d), F32),
        compiler_params=_cparams("parallel"), name="final_rmsnorm",
    )(xs, g.reshape(1, d))


def _to_colmajor(a, n_ctx):
    t, ch = a.shape[0] - n_ctx, a.shape[1]
    lat = a[n_ctx:].reshape(t // GRID_W, GRID_W, ch).transpose(1, 0, 2).reshape(t, ch)
    return jnp.concatenate([a[:n_ctx], lat], axis=0)


def _from_colmajor(a, n_ctx):
    t, ch = a.shape[0] - n_ctx, a.shape[1]
    lat = a[n_ctx:].reshape(GRID_W, t // GRID_W, ch).transpose(1, 0, 2).reshape(t, ch)
    return jnp.concatenate([a[:n_ctx], lat], axis=0)


def kernel(x, c, ctx, c_ctx, w_mod, b_mod, norm1_g, norm2_g, w_in, gate_b, conv_w, conv_b, a_log, dt_bias, d_skip, mlstm_g, ssm_g, w_out, ffn_w1, ffn_w3, ffn_w2, router_w, moe_w1, moe_w3, moe_w2, final_g):
    bsz, t, d = x.shape
    assert bsz == 1, "the kernels treat the single batch element's tokens as rows"
    n_ctx = ctx.shape[1]
    depth = w_mod.shape[0]
    nh = gate_b.shape[1] // 4
    mw = mlstm_g.shape[1]
    dv = mw // nh
    dqk = dv // 2
    sw = ssm_g.shape[1]
    heads = a_log.shape[2]
    conv_ch = conv_w.shape[2]
    n_state = (conv_ch - sw) // (2 * SSM_GROUPS)
    wq = nh * dqk
    splits = (wq, wq, mw, mw, 4 * nh, sw, conv_ch, 2 * heads)
    offs = [0]
    for w_ in splits:
        offs.append(offs[-1] + w_)
    assert offs[-1] == w_in.shape[2]
    ne = router_w.shape[2]

    xs = jnp.concatenate([ctx[0], x[0]], axis=0).astype(F32)
    mod = _modulation(c_ctx.astype(F32), c[0].astype(F32), w_mod.astype(F32), b_mod.astype(F32))

    col_q, col_k, col_v, col_o, col_z, col_x = 0, wq, 2 * wq, 2 * wq + mw, 2 * wq + 2 * mw, 2 * wq + 2 * mw + sw
    n_small = -(-(4 * nh + 2 * heads) // LANES) * LANES

    for layer in range(depth):
        need_ctx = layer < depth - 1
        m6 = mod[layer, :2].reshape(2, 6, d)
        shift1, scale1, gate1, shift2, scale2, gate2 = (m6[:, i][:, None, :] for i in range(6))

        wt = jnp.swapaxes(w_in[layer], 0, 1)
        w_big = jnp.concatenate([wt[offs[0]:offs[4]], wt[offs[5]:offs[7]]], axis=0).astype(BF16)
        w_small = jnp.concatenate([wt[offs[4]:offs[5]], wt[offs[7]:offs[8]],
                                   jnp.zeros((n_small - 4 * nh - 2 * heads, d), F32)], axis=0).astype(BF16)
        xn = _norm_mod(xs, norm1_g[layer], scale1, shift1, n_ctx)
        big = _matmul_nt(xn, w_big, BF16)
        small = _matmul_nt(xn, w_small, F32)

        hf, hb = _mlstm(big, small[:, :4 * nh], gate_b[layer], nh, dqk, dv, n_ctx, col_q, col_k, col_v)

        xbc = _dwconv_silu(_to_colmajor(big[:, col_x:], n_ctx), conv_w[layer], conv_b[layer], n_ctx)
        dt_raw = _to_colmajor(small[:, 4 * nh:4 * nh + 2 * heads], n_ctx)
        yf, yb = _ssd(xbc, dt_raw, a_log[layer], dt_bias[layer], d_skip[layer], n_ctx, sw, n_state)
        yf, yb = _from_colmajor(yf, n_ctx), _from_colmajor(yb, n_ctx)

        mix = _finish(hf, hb, big, yf, yb, mlstm_g[layer], ssm_g[layer], nh, dv, col_o, col_z)
        xs = _matmul_residual_stationary(mix, w_out, layer, xs, gate1, n_ctx)

        i = layer // 2
        if layer % 2 == 0:
            hn = _norm_mod(xs, norm2_g[layer], scale2, shift2, n_ctx)
            tm = _pick(hn.shape[0], FFN_TM, 8)
            nt = hn.shape[0] // tm
            act = _glu(hn, ffn_w1[i:i + 1], ffn_w3[i:i + 1], tm, jnp.zeros((nt,), jnp.int32), nt)
            xs = _matmul_residual(act, ffn_w2[i].astype(BF16), xs, gate2, n_ctx, act.shape[1] // 4)
        else:
            assert not need_ctx, "an MoE layer that still feeds context is not needed by this block"
            hn, logits = _norm_mod(xs, norm2_g[layer], scale2, shift2, n_ctx, router_w=router_w[i])
            tm = min(MOE_TM, t)
            gates, pos, row_tok, tile_expert, n_used = _moe_plan(logits[n_ctx:], ne, tm)
            xr = _gather_rows(hn, jnp.where(row_tok >= 0, row_tok + n_ctx, -1), BF16)
            act = _glu(xr, moe_w1[i], moe_w3[i], tm, tile_expert, n_used)
            yr = _matmul_expert(act, moe_w2[i], tm, tile_expert, n_used)
            return _moe_combine(xs, yr, pos, gates, gate2[1], n_ctx, final_g)[None]

    return _final_norm(xs, final_g, n_ctx)[None]
```

```python
import functools
import math

import jax
import jax.numpy as jnp
from jax import lax
from jax.experimental import pallas as pl
from jax.experimental.pallas import tpu as pltpu

GRID_W = 64
SSM_GROUPS = 8
TOP_K = 2
EPS = 1e-6

V7X_VMEM_LIMIT_BYTES = 56 * 1024 * 1024
V7X_VMEM_LIMIT_BIG_BYTES = 60 * 1024 * 1024
LANES = 128
SCAN_CHUNK = 256
MLSTM_HEADS_PER_STEP = 4
SSD_GROUPS_PER_STEP = 4
ROW_TILE = 256
MOD_TK = 128
MM_TM = 768
MM_TN = 1024
GLU_TN = 512
FFN_TM = 528
DOWN_TN = 1024
MOE_TM = 512
GATHER_ROWS = 256
COMBINE_ROWS = 128

F32 = jnp.float32
BF16 = jnp.bfloat16
HIGHEST = lax.Precision.HIGHEST


def _pick(dim, pref, mult):
    if dim <= pref:
        return dim
    t = (pref // mult) * mult
    while t >= mult:
        if dim % t == 0:
            return t
        t -= mult
    return dim


def _cparams(*sem, vmem=V7X_VMEM_LIMIT_BYTES):
    return pltpu.CompilerParams(dimension_semantics=sem, vmem_limit_bytes=vmem)


def _sigmoid(v):
    return 1.0 / (1.0 + jnp.exp(-v))


def _silu(v):
    return v * _sigmoid(v)


def _softplus(v):
    return jnp.maximum(v, 0.0) + jnp.log1p(jnp.exp(-jnp.abs(v)))


def _log_sigmoid(v):
    return -_softplus(-v)


def _split3(v):
    hi = v.astype(BF16)
    r = v - hi.astype(F32)
    mid = r.astype(BF16)
    lo = (r - mid.astype(F32)).astype(BF16)
    return hi, mid, lo


def _dot_exact(mat01, v, left):
    m = mat01.astype(BF16)
    out = None
    for piece in _split3(v):
        part = (jnp.dot(m, piece, preferred_element_type=F32) if left
                else jnp.dot(piece, m, preferred_element_type=F32))
        out = part if out is None else out + part
    return out


def _mod_kernel(c_ref, w_ref, b_ref, o_ref, acc, *, nk):
    k = pl.program_id(1)

    @pl.when(k == 0)
    def _():
        acc[...] = jnp.zeros_like(acc)

    s_ctx = _silu(c_ref[0])
    s_lat = _silu(c_ref[1])
    tk = w_ref.shape[0]
    for n0 in range(0, w_ref.shape[1], LANES):
        w = w_ref[:, n0:n0 + LANES]
        acc[0, :, n0:n0 + LANES] += jnp.sum((w * s_ctx).reshape(tk // 8, 8, LANES), axis=0)
        acc[1, :, n0:n0 + LANES] += jnp.sum((w * s_lat).reshape(tk // 8, 8, LANES), axis=0)

    @pl.when(k == nk - 1)
    def _():
        o_ref[...] = jnp.zeros_like(o_ref)
        o_ref[0:1, :] = jnp.sum(acc[0], axis=0, keepdims=True) + b_ref[...]
        o_ref[1:2, :] = jnp.sum(acc[1], axis=0, keepdims=True) + b_ref[...]


def _modulation(c_ctx, c_lat, w_mod, b_mod):
    depth, d, n = w_mod.shape
    tk = _pick(d, MOD_TK, 8)
    nk = d // tk
    cb = jnp.broadcast_to(jnp.stack([c_ctx, c_lat])[:, :, None], (2, d, LANES))
    return pl.pallas_call(
        functools.partial(_mod_kernel, nk=nk),
        grid=(depth, nk),
        in_specs=[pl.BlockSpec((2, tk, LANES), lambda l, k: (0, k, 0)),
                  pl.BlockSpec((None, tk, n), lambda l, k: (l, k, 0)),
                  pl.BlockSpec((None, 1, n), lambda l, k: (l, 0, 0))],
        out_specs=pl.BlockSpec((None, 8, n), lambda l, k: (l, 0, 0)),
        out_shape=jax.ShapeDtypeStruct((depth, 8, n), F32),
        scratch_shapes=[pltpu.VMEM((2, 8, n), F32)],
        compiler_params=_cparams("parallel", "arbitrary"),
        name="adaln_modulation",
    )(cb, w_mod, b_mod.reshape(depth, 1, n))


def _norm_kernel(x_ref, g_ref, sc_ref, sh_ref, o_ref):
    x = x_ref[...]
    r = lax.rsqrt(jnp.mean(x * x, axis=-1, keepdims=True) + EPS)
    o_ref[...] = (x * r * g_ref[...] * (1.0 + sc_ref[...]) + sh_ref[...]).astype(o_ref.dtype)


def _pack_bf16_pairs(v):
    half = v.shape[1] // 2
    hi = lax.bitcast_convert_type(v[:, :half].astype(BF16).astype(F32), jnp.uint32)
    lo = lax.bitcast_convert_type(v[:, half:].astype(BF16).astype(F32), jnp.uint32)
    return (hi & jnp.uint32(0xFFFF0000)) | (lo >> 16)


def _unpack_bf16_pairs(p):
    hi = lax.bitcast_convert_type(p & jnp.uint32(0xFFFF0000), F32).astype(BF16)
    lo = lax.bitcast_convert_type(p << 16, F32).astype(BF16)
    return hi, lo


def _norm_router_kernel(x_ref, g_ref, sc_ref, sh_ref, rw_ref, o_ref, lg_ref):
    x = x_ref[...]
    r = lax.rsqrt(jnp.mean(x * x, axis=-1, keepdims=True) + EPS)
    h = x * r * g_ref[...] * (1.0 + sc_ref[...]) + sh_ref[...]
    o_ref[...] = _pack_bf16_pairs(h)
    lg_ref[...] = jnp.dot(h, rw_ref[...], preferred_element_type=F32, precision=HIGHEST)


def _norm_mod(xs, g, scale, shift, n_ctx, router_w=None):
    s, d = xs.shape
    r = _pick(math.gcd(n_ctx, s - n_ctx), ROW_TILE, 8)
    nct = n_ctx // r
    row_spec = pl.BlockSpec((r, d), lambda i: (i, 0))
    mod_spec = pl.BlockSpec((None, 1, d), lambda i: (jnp.where(i < nct, 0, 1), 0, 0))
    in_specs = [row_spec, pl.BlockSpec((1, d), lambda i: (0, 0)), mod_spec, mod_spec]
    if router_w is None:
        return pl.pallas_call(
            _norm_kernel, grid=(s // r,), in_specs=in_specs, out_specs=row_spec,
            out_shape=jax.ShapeDtypeStruct((s, d), BF16),
            compiler_params=_cparams("parallel"), name="rmsnorm_modulate",
        )(xs, g.reshape(1, d), scale, shift)
    ne = router_w.shape[1]
    rw = jnp.zeros((d, LANES), F32).at[:, :ne].set(router_w)
    return pl.pallas_call(
        _norm_router_kernel, grid=(s // r,),
        in_specs=in_specs + [pl.BlockSpec((d, LANES), lambda i: (0, 0))],
        out_specs=[pl.BlockSpec((r, d // 2), lambda i: (i, 0)), pl.BlockSpec((r, LANES), lambda i: (i, 0))],
        out_shape=[jax.ShapeDtypeStruct((s, d // 2), jnp.uint32), jax.ShapeDtypeStruct((s, LANES), F32)],
        compiler_params=_cparams("parallel"), name="rmsnorm_modulate_router",
    )(xs, g.reshape(1, d), scale, shift, rw)


def _mm_nt_kernel(a_ref, wt_ref, o_ref):
    o_ref[...] = lax.dot_general(a_ref[...], wt_ref[...], (((1,), (1,)), ((), ())),
                                 preferred_element_type=F32).astype(o_ref.dtype)


def _matmul_nt(a, wt, out_dtype):
    m, k = a.shape
    n = wt.shape[0]
    tm = _pick(m, MM_TM, 8)
    tn = _pick(n, MM_TN, LANES)
    return pl.pallas_call(
        _mm_nt_kernel, grid=(m // tm, n // tn),
        in_specs=[pl.BlockSpec((tm, k), lambda i, j: (i, 0)), pl.BlockSpec((tn, k), lambda i, j: (j, 0))],
        out_specs=pl.BlockSpec((tm, tn), lambda i, j: (i, j)),
        out_shape=jax.ShapeDtypeStruct((m, n), out_dtype),
        compiler_params=_cparams("parallel", "parallel"), name="matmul_nt",
    )(a, wt)


def _row_gate(gate_ref, row0, tm, n_ctx):
    rows = row0 + lax.broadcasted_iota(jnp.int32, (tm, 1), 0)
    return jnp.where(rows < n_ctx, gate_ref[0], gate_ref[1])


def _mm_res_kernel(a_ref, w_ref, res_ref, gate_ref, o_ref, acc_ref, *, nk, tm, n_ctx, k_tail):
    k = pl.program_id(2)

    @pl.when(k == 0)
    def _():
        acc_ref[...] = jnp.zeros_like(acc_ref)

    if k_tail:
        @pl.when(k < nk - 1)
        def _():
            acc_ref[...] += jnp.dot(a_ref[...], w_ref[...], preferred_element_type=F32)

        @pl.when(k == nk - 1)
        def _():
            w = w_ref[...]
            w = jnp.where(lax.broadcasted_iota(jnp.int32, w.shape, 0) < k_tail, w, jnp.zeros_like(w))
            acc_ref[...] += jnp.dot(a_ref[...], w, preferred_element_type=F32)
    else:
        acc_ref[...] += jnp.dot(a_ref[...], w_ref[...], preferred_element_type=F32)

    @pl.when(k == nk - 1)
    def _():
        gate = _row_gate(gate_ref, pl.program_id(0) * tm, tm, n_ctx)
        o_ref[...] = res_ref[...] + gate * acc_ref[...]


def _matmul_residual(a, w, res, gate, n_ctx, tk_pref):
    m, kdim = a.shape
    n = w.shape[1]
    tm = _pick(m, MM_TM, 8)
    tn = _pick(n, MM_TN, LANES)
    tk = _pick(kdim, tk_pref, LANES)
    if w.shape[0] <= (kdim // tk - 1) * tk:
        tk = kdim
    nk = kdim // tk
    k_tail = w.shape[0] - (nk - 1) * tk
    assert 0 < k_tail <= tk
    return pl.pallas_call(
        functools.partial(_mm_res_kernel, nk=nk, tm=tm, n_ctx=n_ctx, k_tail=0 if k_tail == tk else k_tail),
        grid=(m // tm, n // tn, nk),
        in_specs=[pl.BlockSpec((tm, tk), lambda i, j, k: (i, k)),
                  pl.BlockSpec((tk, tn), lambda i, j, k: (k, j)),
                  pl.BlockSpec((tm, tn), lambda i, j, k: (i, j)),
                  pl.BlockSpec((2, 1, tn), lambda i, j, k: (0, 0, j))],
        out_specs=pl.BlockSpec((tm, tn), lambda i, j, k: (i, j)),
        out_shape=jax.ShapeDtypeStruct((m, n), F32),
        scratch_shapes=[pltpu.VMEM((tm, tn), F32)],
        compiler_params=_cparams("parallel", "parallel", "arbitrary"), name="matmul_residual",
    )(a, w, res, gate)


def _mm_res_ws_kernel(a_ref, w_ref, res_ref, gate_ref, o_ref, wbuf, *, tm, n_ctx):
    i = pl.program_id(1)

    @pl.when(i == 0)
    def _():
        wbuf[...] = w_ref[...].astype(BF16)

    acc = jnp.dot(a_ref[...], wbuf[...], preferred_element_type=F32)
    o_ref[...] = res_ref[...] + _row_gate(gate_ref, i * tm, tm, n_ctx) * acc


def _matmul_residual_stationary(a, w, layer, res, gate, n_ctx):
    m, k = a.shape
    n = w.shape[2]
    tm = _pick(m, MM_TM, 8)
    tn = _pick(n, GLU_TN, LANES)
    return pl.pallas_call(
        functools.partial(_mm_res_ws_kernel, tm=tm, n_ctx=n_ctx),
        grid=(n // tn, m // tm),
        in_specs=[pl.BlockSpec((tm, k), lambda j, i: (i, 0)),
                  pl.BlockSpec((None, k, tn), lambda j, i: (layer, 0, j)),
                  pl.BlockSpec((tm, tn), lambda j, i: (i, j)),
                  pl.BlockSpec((2, 1, tn), lambda j, i: (0, 0, j))],
        out_specs=pl.BlockSpec((tm, tn), lambda j, i: (i, j)),
        out_shape=jax.ShapeDtypeStruct((m, n), F32),
        scratch_shapes=[pltpu.VMEM((k, tn), BF16)],
        compiler_params=_cparams("arbitrary", "arbitrary"), name="matmul_residual_ws",
    )(a, w, res, gate)


def _stationary_plan(tile_expert, n_used, n_experts, nj):
    n_tiles = tile_expert.shape[0]
    tile_ids = jnp.arange(n_tiles, dtype=jnp.int32)
    cnt = jnp.sum(((tile_expert[:, None] == jnp.arange(n_experts)[None, :])
                   & (tile_ids[:, None] < n_used)).astype(jnp.int32), axis=0)
    seg_end = jnp.cumsum(cnt)
    seg_start = seg_end - cnt
    step = jnp.arange(n_tiles * nj, dtype=jnp.int32)
    used_steps = n_used * nj
    e = jnp.minimum(jnp.searchsorted(seg_end * nj, step, side='right'), n_experts - 1).astype(jnp.int32)
    seg_len = jnp.maximum(cnt[e], 1)
    local = jnp.maximum(step - nj * seg_start[e], 0)
    j = local // seg_len
    i = seg_start[e] + local % seg_len
    used = step < used_steps
    last = used_steps - 1
    fill = step - used_steps
    pick = lambda arr: jnp.where(used, arr, arr[last]).astype(jnp.int32)
    out_i = jnp.where(used, i, n_used + fill // nj).astype(jnp.int32)
    out_j = jnp.where(used, j, fill % nj).astype(jnp.int32)
    return pick(e), pick(j), pick(i), out_i, out_j, used.astype(jnp.int32)


def _stationary_flags(e_ref, jw_ref, used_ref):
    s = pl.program_id(0)
    prev = jnp.maximum(s - 1, 0)
    fresh = jnp.logical_or(s == 0, jnp.logical_or(e_ref[s] != e_ref[prev], jw_ref[s] != jw_ref[prev]))
    used = used_ref[s] > 0
    return used, jnp.logical_and(used, fresh)


def _glu_kernel(e_ref, jw_ref, ia_ref, io_ref, jo_ref, used_ref, a_ref, w1_ref, w3_ref, o_ref, w1b, w3b,
                *, n_valid, ragged):
    used, fresh = _stationary_flags(e_ref, jw_ref, used_ref)

    @pl.when(fresh)
    def _():
        w1 = w1_ref[...]
        w3 = w3_ref[...]
        if ragged:
            col = jw_ref[pl.program_id(0)] * w1.shape[1] + lax.broadcasted_iota(jnp.int32, w1.shape, 1)
            w1 = jnp.where(col < n_valid, w1, 0.0)
            w3 = jnp.where(col < n_valid, w3, 0.0)
        w1b[...] = w1.astype(BF16)
        w3b[...] = w3.astype(BF16)

    @pl.when(used)
    def _():
        a = a_ref[...]
        h1 = jnp.dot(a, w1b[...], preferred_element_type=F32)
        h3 = jnp.dot(a, w3b[...], preferred_element_type=F32)
        o_ref[...] = (_silu(h1) * h3).astype(o_ref.dtype)

    @pl.when(jnp.logical_not(used))
    def _():
        o_ref[...] = jnp.zeros_like(o_ref)


def _stationary_specs(tm, k, tn):
    a_spec = pl.BlockSpec((tm, k), lambda s, e, jw, ia, io, jo, u: (ia[s], 0))
    w_spec = pl.BlockSpec((None, k, tn), lambda s, e, jw, ia, io, jo, u: (e[s], 0, jw[s]))
    o_spec = pl.BlockSpec((tm, tn), lambda s, e, jw, ia, io, jo, u: (io[s], jo[s]))
    return a_spec, w_spec, o_spec


def _glu(a, w1, w3, tm, tile_expert, n_used):
    m, k = a.shape
    ne, _, n = w1.shape
    tn = GLU_TN if n % LANES == 0 and n > GLU_TN else n
    nj = pl.cdiv(n, tn)
    plan = _stationary_plan(tile_expert, n_used, ne, nj)
    a_spec, w_spec, o_spec = _stationary_specs(tm, k, tn)
    return pl.pallas_call(
        functools.partial(_glu_kernel, n_valid=n, ragged=n % tn != 0),
        grid_spec=pltpu.PrefetchScalarGridSpec(
            num_scalar_prefetch=6, grid=(plan[0].shape[0],),
            in_specs=[a_spec, w_spec, w_spec], out_specs=o_spec,
            scratch_shapes=[pltpu.VMEM((k, tn), BF16)] * 2),
        out_shape=jax.ShapeDtypeStruct((m, nj * tn), BF16),
        compiler_params=_cparams("arbitrary", vmem=V7X_VMEM_LIMIT_BIG_BYTES), name="swiglu_up",
    )(*plan, a, w1, w3)


def _mm_expert_kernel(e_ref, jw_ref, ia_ref, io_ref, jo_ref, used_ref, a_ref, w_ref, o_ref, wb):
    used, fresh = _stationary_flags(e_ref, jw_ref, used_ref)

    @pl.when(fresh)
    def _():
        wb[...] = w_ref[...].astype(BF16)

    @pl.when(used)
    def _():
        o_ref[...] = jnp.dot(a_ref[...], wb[...], preferred_element_type=F32).astype(o_ref.dtype)

    @pl.when(jnp.logical_not(used))
    def _():
        o_ref[...] = jnp.zeros_like(o_ref)


def _matmul_expert(a, w, tm, tile_expert, n_used):
    m, k = a.shape
    ne, _, n = w.shape
    tn = _pick(n, DOWN_TN, LANES)
    nj = n // tn
    plan = _stationary_plan(tile_expert, n_used, ne, nj)
    a_spec, w_spec, o_spec = _stationary_specs(tm, k, tn)
    return pl.pallas_call(
        _mm_expert_kernel,
        grid_spec=pltpu.PrefetchScalarGridSpec(
            num_scalar_prefetch=6, grid=(plan[0].shape[0],),
            in_specs=[a_spec, w_spec], out_specs=o_spec,
            scratch_shapes=[pltpu.VMEM((k, tn), BF16)]),
        out_shape=jax.ShapeDtypeStruct((m, n), F32),
        compiler_params=_cparams("arbitrary", vmem=V7X_VMEM_LIMIT_BIG_BYTES), name="expert_down",
    )(*plan, a, w)


def _chunk_maps(nct, nlt):
    def fwd(s):
        return s

    def bwd(s):
        return jnp.where(s < nct, nct - 1 - s, 2 * nct + nlt - 1 - s)

    return fwd, bwd


def _tri(l, lower):
    r = lax.broadcasted_iota(jnp.int32, (l, l), 0)
    c = lax.broadcasted_iota(jnp.int32, (l, l), 1)
    return (r >= c) if lower else (r <= c)


def _mlstm_direction(q, k, v, g, gt, c_ref, n_ref, m_ref, o_ref, *, fwd, scale):
    l = q.shape[0]
    ci, cf = (0, 2) if fwd else (1, 3)
    low = _tri(l, True).astype(F32)
    up = _tri(l, False).astype(F32)
    b_col = _dot_exact(low if fwd else up, _log_sigmoid(g), True)[:, cf:cf + 1]
    b_row = _dot_exact(up if fwd else low, _log_sigmoid(gt), False)[cf:cf + 1, :]
    i_col = g[:, ci:ci + 1]
    i_row = gt[ci:ci + 1, :]
    causal = _tri(l, fwd)
    logw = jnp.where(causal, b_col - b_row + i_row, -jnp.inf)
    m_intra = jnp.max(logw, axis=-1, keepdims=True)
    m_st = m_ref[...]
    inter_log = b_col + m_st
    m_t = jnp.maximum(inter_log, m_intra)
    a_inter = jnp.exp(inter_log - m_t)
    ks = (k.astype(F32) * scale)
    ksb = ks.astype(BF16)
    qk = lax.dot_general(q, ksb, (((1,), (1,)), ((), ())), preferred_element_type=F32)
    s = qk * jnp.exp(logw - m_t)
    c_st = c_ref[...]
    n_st = n_ref[...]
    num = a_inter * jnp.dot(q, c_st.astype(BF16), preferred_element_type=F32) \
        + jnp.dot(s.astype(BF16), v, preferred_element_type=F32)
    den = a_inter * jnp.sum(q.astype(F32) * n_st, axis=-1, keepdims=True) + jnp.sum(s, axis=-1, keepdims=True)
    o_ref[...] = (num / jnp.maximum(jnp.abs(den), jnp.exp(-m_t))).astype(o_ref.dtype)
    b_last = b_col[l - 1:l, :] if fwd else b_col[0:1, :]
    g_col = b_last - b_col + i_col
    m_new = jnp.maximum(b_last + m_st, jnp.max(g_col, axis=0, keepdims=True))
    kw = ks * jnp.exp(g_col - m_new)
    dec = jnp.exp(b_last + m_st - m_new)
    c_ref[...] = dec * c_st + lax.dot_general(kw.astype(BF16), v, (((0,), (0,)), ((), ())),
                                              preferred_element_type=F32)
    n_ref[...] = dec * n_st + jnp.sum(kw, axis=0, keepdims=True)
    m_ref[...] = m_new


def _mlstm_kernel(qf, kf, vf, gf, gtf, qb, kb, vb, gb, gtb, bias, bias_t, hf, hb,
                  cf, nf, mf, cb, nb, mb, *, scale, dqk, dv):
    @pl.when(pl.program_id(1) == 0)
    def _():
        for ref in (cf, nf, mf, cb, nb, mb):
            ref[...] = jnp.zeros_like(ref)

    for h in range(cf.shape[0]):
        qs, vs = slice(h * dqk, (h + 1) * dqk), slice(h * dv, (h + 1) * dv)
        _mlstm_direction(qf[:, qs], kf[:, qs], vf[:, vs], gf[h] + bias[h], gtf[h] + bias_t[h],
                         cf.at[h], nf.at[h], mf.at[h], hf.at[:, pl.ds(h * dv, dv)], fwd=True, scale=scale)
        _mlstm_direction(qb[:, qs], kb[:, qs], vb[:, vs], gb[h] + bias[h], gtb[h] + bias_t[h],
                         cb.at[h], nb.at[h], mb.at[h], hb.at[:, pl.ds(h * dv, dv)], fwd=False, scale=scale)


def _mlstm(big, gates, gate_b, nh, dqk, dv, n_ctx, col_q, col_k, col_v):
    s = big.shape[0]
    l = _pick(math.gcd(n_ctx, s - n_ctx), SCAN_CHUNK, 8)
    nct, nlt = n_ctx // l, (s - n_ctx) // l
    fwd, bwd = _chunk_maps(nct, nlt)
    gh = gates.reshape(s, 4, nh).transpose(2, 0, 1)
    g_cols = jnp.zeros((nh, s, LANES), F32).at[:, :, :4].set(gh)
    g_rows = jnp.zeros((nh, 8, s), F32).at[:, :4, :].set(gh.transpose(0, 2, 1))
    bh = gate_b.reshape(4, nh).T
    b_cols = jnp.zeros((nh, 1, LANES), F32).at[:, 0, :4].set(bh)
    b_rows = jnp.zeros((nh, 8, 1), F32).at[:, :4, 0].set(bh)
    hp = MLSTM_HEADS_PER_STEP if nh % MLSTM_HEADS_PER_STEP == 0 else 1
    qw, vw = hp * dqk, hp * dv
    assert col_q % qw == 0 and col_k % qw == 0 and col_v % vw == 0
    qo, ko, vo = col_q // qw, col_k // qw, col_v // vw

    def specs(cm):
        return [pl.BlockSpec((l, qw), lambda h, t: (cm(t), qo + h)),
                pl.BlockSpec((l, qw), lambda h, t: (cm(t), ko + h)),
                pl.BlockSpec((l, vw), lambda h, t: (cm(t), vo + h)),
                pl.BlockSpec((hp, l, LANES), lambda h, t: (h, cm(t), 0)),
                pl.BlockSpec((hp, 8, l), lambda h, t: (h, 0, cm(t)))]

    out_f = pl.BlockSpec((l, vw), lambda h, t: (fwd(t), h))
    out_b = pl.BlockSpec((l, vw), lambda h, t: (bwd(t), h))
    state = [pltpu.VMEM((hp, dqk, dv), F32), pltpu.VMEM((hp, 1, dqk), F32), pltpu.VMEM((hp, 1, 1), F32)]
    return pl.pallas_call(
        functools.partial(_mlstm_kernel, scale=dqk ** -0.5, dqk=dqk, dv=dv),
        grid=(nh // hp, nct + nlt),
        in_specs=specs(fwd) + specs(bwd) + [pl.BlockSpec((hp, 1, LANES), lambda h, t: (h, 0, 0)),
                                             pl.BlockSpec((hp, 8, 1), lambda h, t: (h, 0, 0))],
        out_specs=[out_f, out_b],
        out_shape=[jax.ShapeDtypeStruct((s, nh * dv), BF16)] * 2,
        scratch_shapes=state + state,
        compiler_params=_cparams("parallel", "arbitrary"), name="mlstm_scan",
    )(big, big, big, g_cols, g_rows, big, big, big, g_cols, g_rows, b_cols, b_rows)


def _conv_kernel(u_ref, prev_ref, next_ref, w_ref, b_ref, o_ref, *, nct, nt, width, halo):
    i = pl.program_id(0)
    first = jnp.logical_or(i == 0, i == nct)
    last = jnp.logical_or(i == nct - 1, i == nt - 1)
    r = u_ref.shape[0]
    prev = jnp.where(first, 0.0, prev_ref[...].astype(F32))
    nxt = jnp.where(last, 0.0, next_ref[...].astype(F32))
    ext = jnp.concatenate([prev, u_ref[...].astype(F32), nxt], axis=0)
    w = w_ref[...]
    acc = b_ref[...] + w[0:1, :] * ext[halo - width // 2: halo - width // 2 + r]
    for j in range(1, width):
        off = halo - width // 2 + j
        acc = acc + w[j:j + 1, :] * ext[off: off + r]
    o_ref[...] = _silu(acc).astype(o_ref.dtype)


def _dwconv_silu(u, conv_w, conv_b, n_ctx):
    s, ch = u.shape
    width = conv_w.shape[0]
    halo = 16
    r = _pick(math.gcd(n_ctx, s - n_ctx), ROW_TILE, halo)
    tc = _pick(ch, 1024, LANES)
    nt, nct, rb = s // r, n_ctx // r, r // halo
    w8 = jnp.zeros((8, ch), F32).at[:width].set(conv_w)
    return pl.pallas_call(
        functools.partial(_conv_kernel, nct=nct, nt=nt, width=width, halo=halo),
        grid=(nt, ch // tc),
        in_specs=[pl.BlockSpec((r, tc), lambda i, j: (i, j)),
                  pl.BlockSpec((halo, tc), lambda i, j: (jnp.maximum(i * rb - 1, 0), j)),
                  pl.BlockSpec((halo, tc), lambda i, j: (jnp.minimum((i + 1) * rb, s // halo - 1), j)),
                  pl.BlockSpec((8, tc), lambda i, j: (0, j)),
                  pl.BlockSpec((1, tc), lambda i, j: (0, j))],
        out_specs=pl.BlockSpec((r, tc), lambda i, j: (i, j)),
        out_shape=jax.ShapeDtypeStruct((s, ch), BF16),
        compiler_params=_cparams("parallel", "parallel"), name="dwconv_silu",
    )(u, u, u, w8, conv_b.reshape(1, ch))


def _per_channel(cols, off, hg, p, lane):
    out = jnp.broadcast_to(cols[:, off + hg - 1:off + hg], lane.shape)
    for h in range(hg - 2, -1, -1):
        out = jnp.where(lane < (h + 1) * p, jnp.broadcast_to(cols[:, off + h:off + h + 1], lane.shape), out)
    return out


def _ssd_direction(x, bm, cm, dtv, da, da_t, st_ref, *, fwd, hg, p):
    l = x.shape[0]
    off = 0 if fwd else hg
    low = _tri(l, True)
    up = _tri(l, False)
    acum_col = _dot_exact(low if fwd else up, da, True)
    acum_row = _dot_exact(up if fwd else low, da_t, False)
    lane = lax.broadcasted_iota(jnp.int32, x.shape, 1)
    acx = _per_channel(acum_col, off, hg, p, lane)
    xdt = x.astype(F32) * _per_channel(dtv, off, hg, p, lane)
    xdt_b = xdt.astype(BF16)
    causal = _tri(l, fwd)
    cb = lax.dot_general(cm, bm, (((1,), (1,)), ((), ())), preferred_element_type=F32)
    y = None
    for h in range(hg):
        seg = acum_col[:, off + h:off + h + 1] - acum_row[off + h:off + h + 1, :]
        mix = (cb * jnp.exp(jnp.where(causal, seg, -jnp.inf))).astype(BF16)
        own = jnp.logical_and(lane >= h * p, lane < (h + 1) * p)
        part = jnp.dot(mix, jnp.where(own, xdt_b, jnp.zeros_like(xdt_b)), preferred_element_type=F32)
        y = part if y is None else y + part
    st = st_ref[...]
    y = y + jnp.dot(cm, st.astype(BF16), preferred_element_type=F32) * jnp.exp(acx)
    total = acx[l - 1:l, :] if fwd else acx[0:1, :]
    xw = (xdt * jnp.exp(total - acx)).astype(BF16)
    st_ref[...] = jnp.exp(total) * st + lax.dot_general(bm, xw, (((0,), (0,)), ((), ())),
                                                        preferred_element_type=F32)
    return y


def _ssd_kernel(xf, bf, cf, dtf, dttf, xb, bb, cb, dtb, dttb, a_c, a_r, db_c, db_r, dsk,
                yf, yb, stf, stb, *, hg, p, n):
    @pl.when(pl.program_id(1) == 0)
    def _():
        stf[...] = jnp.zeros_like(stf)
        stb[...] = jnp.zeros_like(stb)

    gw = hg * p
    for g in range(stf.shape[0]):
        def steps(dt_ref, dtt_ref):
            dtv = _softplus(dt_ref[g] + db_c[g])
            da_t = _softplus(dtt_ref[g] + db_r[g]) * a_r[g]
            return dtv, dtv * a_c[g], da_t

        xs, ns = slice(g * gw, (g + 1) * gw), slice(g * n, (g + 1) * n)
        x = xf[:, xs]
        y = _ssd_direction(x, bf[:, ns], cf[:, ns], *steps(dtf, dttf), stf.at[g], fwd=True, hg=hg, p=p)
        yf[:, xs] = (y + dsk[g] * x.astype(F32)).astype(yf.dtype)
        y = _ssd_direction(xb[:, xs], bb[:, ns], cb[:, ns], *steps(dtb, dttb), stb.at[g], fwd=False, hg=hg, p=p)
        yb[:, xs] = y.astype(yb.dtype)


def _ssd(xbc, dt_raw, a_log, dt_bias, d_skip, n_ctx, sw, n_state):
    s = xbc.shape[0]
    h = a_log.shape[1]
    g = SSM_GROUPS
    hg, p = h // g, sw // h
    gw = hg * p
    l = _pick(math.gcd(n_ctx, s - n_ctx), SCAN_CHUNK, 8)
    nct, nlt = n_ctx // l, (s - n_ctx) // l
    fwd, bwd = _chunk_maps(nct, nlt)
    dtg = dt_raw.reshape(s, 2, g, hg).transpose(2, 0, 1, 3).reshape(g, s, 2 * hg)
    dt_cols = jnp.zeros((g, s, LANES), F32).at[:, :, :2 * hg].set(dtg)
    assert 2 * hg <= 8 and sw % n_state == 0
    dt_rows = jnp.zeros((g, 8, s), F32).at[:, :2 * hg, :].set(dtg.transpose(0, 2, 1))
    a = (-jnp.exp(a_log.astype(F32))).reshape(2, g, hg).transpose(1, 0, 2).reshape(g, 2 * hg)
    db = dt_bias.reshape(2, g, hg).transpose(1, 0, 2).reshape(g, 2 * hg)
    a_c = jnp.zeros((g, 1, LANES), F32).at[:, 0, :2 * hg].set(a)
    a_r = jnp.zeros((g, 8, 1), F32).at[:, :2 * hg, 0].set(a)
    db_c = jnp.zeros((g, 1, LANES), F32).at[:, 0, :2 * hg].set(db)
    db_r = jnp.zeros((g, 8, 1), F32).at[:, :2 * hg, 0].set(db)
    dsk = jnp.repeat(d_skip, p).reshape(g, 1, gw)
    gp = SSD_GROUPS_PER_STEP if g % SSD_GROUPS_PER_STEP == 0 else 1
    xw, nw = gp * gw, gp * n_state
    assert sw % nw == 0 and (g * n_state) % nw == 0
    bo, co = sw // nw, (sw + g * n_state) // nw

    def specs(cm):
        return [pl.BlockSpec((l, xw), lambda gi, t: (cm(t), gi)),
                pl.BlockSpec((l, nw), lambda gi, t: (cm(t), bo + gi)),
                pl.BlockSpec((l, nw), lambda gi, t: (cm(t), co + gi)),
                pl.BlockSpec((gp, l, LANES), lambda gi, t: (gi, cm(t), 0)),
                pl.BlockSpec((gp, 8, l), lambda gi, t: (gi, 0, cm(t)))]

    par_c = pl.BlockSpec((gp, 1, LANES), lambda gi, t: (gi, 0, 0))
    par_r = pl.BlockSpec((gp, 8, 1), lambda gi, t: (gi, 0, 0))
    return pl.pallas_call(
        functools.partial(_ssd_kernel, hg=hg, p=p, n=n_state),
        grid=(g // gp, nct + nlt),
        in_specs=specs(fwd) + specs(bwd) + [par_c, par_r, par_c, par_r,
                                             pl.BlockSpec((gp, 1, gw), lambda gi, t: (gi, 0, 0))],
        out_specs=[pl.BlockSpec((l, xw), lambda gi, t: (fwd(t), gi)),
                   pl.BlockSpec((l, xw), lambda gi, t: (bwd(t), gi))],
        out_shape=[jax.ShapeDtypeStruct((s, sw), BF16)] * 2,
        scratch_shapes=[pltpu.VMEM((gp, n_state, gw), F32)] * 2,
        compiler_params=_cparams("parallel", "arbitrary"), name="ssd_scan",
    )(xbc, xbc, xbc, dt_cols, dt_rows, xbc, xbc, xbc, dt_cols, dt_rows, a_c, a_r, db_c, db_r, dsk)


def _finish_kernel(hf, hb, o, yf, yb, z, mg, sg, out, *, nh, dv):
    mw = nh * dv
    h = hf[...].astype(F32) + hb[...].astype(F32)
    gate = _sigmoid(o[...].astype(F32))
    mgv = mg[...]
    for k in range(nh):
        hk = h[:, k * dv:(k + 1) * dv]
        r = lax.rsqrt(jnp.mean(hk * hk, axis=-1, keepdims=True) + EPS)
        out[:, k * dv:(k + 1) * dv] = (hk * r * mgv[:, k * dv:(k + 1) * dv]
                                       * gate[:, k * dv:(k + 1) * dv]).astype(out.dtype)
    y = (yf[...].astype(F32) + yb[...].astype(F32)) * _silu(z[...].astype(F32))
    r = lax.rsqrt(jnp.mean(y * y, axis=-1, keepdims=True) + EPS)
    out[:, mw:] = (y * r * sg[...]).astype(out.dtype)


def _finish(hf, hb, big, yf, yb, mlstm_g, ssm_g, nh, dv, col_o, col_z):
    s, mw = hf.shape
    sw = yf.shape[1]
    r = _pick(s, ROW_TILE, 8)
    return pl.pallas_call(
        functools.partial(_finish_kernel, nh=nh, dv=dv),
        grid=(s // r,),
        in_specs=[pl.BlockSpec((r, mw), lambda i: (i, 0)), pl.BlockSpec((r, mw), lambda i: (i, 0)),
                  pl.BlockSpec((r, mw), lambda i: (i, col_o // mw)),
                  pl.BlockSpec((r, sw), lambda i: (i, 0)), pl.BlockSpec((r, sw), lambda i: (i, 0)),
                  pl.BlockSpec((r, sw), lambda i: (i, col_z // sw)),
                  pl.BlockSpec((1, mw), lambda i: (0, 0)), pl.BlockSpec((1, sw), lambda i: (0, 0))],
        out_specs=pl.BlockSpec((r, mw + sw), lambda i: (i, 0)),
        out_shape=jax.ShapeDtypeStruct((s, mw + sw), BF16),
        compiler_params=_cparams("parallel"), name="mixer_finish",
    )(hf, hb, big, yf, yb, big, mlstm_g.reshape(1, mw), ssm_g.reshape(1, sw))


def _row_copy(src_ref, row, dst_ref, k, sem):
    return pltpu.make_async_copy(src_ref.at[pl.ds(row, 1)], dst_ref.at[pl.ds(k, 1)], sem)


def _gather_kernel(idx_ref, nxt_ref, src_ref, o_ref, buf, sem):
    i = pl.program_id(0)
    n_rows = o_ref.shape[0]
    slot = lax.rem(i, 2)

    def issue(ids_ref, s):
        def body(k, carry):
            row = ids_ref[0, 0, k]

            @pl.when(row >= 0)
            def _():
                _row_copy(src_ref, row, buf.at[s], k, sem.at[s]).start()
            return carry
        lax.fori_loop(0, n_rows, body, 0)

    @pl.when(i == 0)
    def _():
        buf[...] = jnp.zeros_like(buf)
        issue(idx_ref, 0)

    @pl.when(i + 1 < pl.num_programs(0))
    def _():
        issue(nxt_ref, 1 - slot)

    def wait(k, carry):
        @pl.when(idx_ref[0, 0, k] >= 0)
        def _():
            _row_copy(src_ref, 0, buf.at[slot], k, sem.at[slot]).wait()
        return carry

    lax.fori_loop(0, n_rows, wait, 0)
    hi, lo = _unpack_bf16_pairs(buf[slot])
    half = o_ref.shape[1] // 2
    o_ref[:, :half] = hi
    o_ref[:, half:] = lo


def _gather_rows(src, idx):
    n = idx.shape[0]
    d = 2 * src.shape[1]
    r = _pick(n, GATHER_ROWS, 8)
    nt = n // r
    ids = idx.reshape(nt, 1, r)
    return pl.pallas_call(
        _gather_kernel, grid=(nt,),
        in_specs=[pl.BlockSpec((1, 1, r), lambda i: (i, 0, 0), memory_space=pltpu.SMEM),
                  pl.BlockSpec((1, 1, r), lambda i: (jnp.minimum(i + 1, nt - 1), 0, 0), memory_space=pltpu.SMEM),
                  pl.BlockSpec(memory_space=pl.ANY)],
        out_specs=pl.BlockSpec((r, d), lambda i: (i, 0)),
        out_shape=jax.ShapeDtypeStruct((n, d), BF16),
        scratch_shapes=[pltpu.VMEM((2, r, d // 2), src.dtype), pltpu.SemaphoreType.DMA((2,))],
        compiler_params=_cparams("arbitrary"), name="moe_gather",
    )(ids, ids, src)


def _combine_kernel(p0_ref, p1_ref, q0_ref, q1_ref, yr_ref, x_ref, gk_ref, gate_ref, fg_ref, o_ref, buf, sem,
                    *, final_norm):
    i = pl.program_id(0)
    n_rows = x_ref.shape[0]
    slot = lax.rem(i, 2)

    def issue(a_ref, b_ref, s):
        def body(k, carry):
            _row_copy(yr_ref, a_ref[0, 0, k], buf.at[s, 0], k, sem.at[s]).start()
            _row_copy(yr_ref, b_ref[0, 0, k], buf.at[s, 1], k, sem.at[s]).start()
            return carry
        lax.fori_loop(0, n_rows, body, 0)

    @pl.when(i == 0)
    def _():
        issue(p0_ref, p1_ref, 0)

    @pl.when(i + 1 < pl.num_programs(0))
    def _():
        issue(q0_ref, q1_ref, 1 - slot)

    def wait(k, carry):
        _row_copy(yr_ref, 0, buf.at[slot, 0], k, sem.at[slot]).wait()
        _row_copy(yr_ref, 0, buf.at[slot, 1], k, sem.at[slot]).wait()
        return carry

    lax.fori_loop(0, n_rows, wait, 0)
    gk = gk_ref[...]
    x = x_ref[...] + gate_ref[...] * (gk[:, 0:1] * buf[slot, 0] + gk[:, 1:2] * buf[slot, 1])
    if final_norm:
        x = x * lax.rsqrt(jnp.mean(x * x, axis=-1, keepdims=True) + EPS) * fg_ref[...]
    o_ref[...] = x


def _moe_combine(xs, yr, pos, gates, gate_mod, n_ctx, final_g):
    s, d = xs.shape
    t = s - n_ctx
    r = _pick(math.gcd(n_ctx, t), COMBINE_ROWS, 8)
    nct, nt = n_ctx // r, t // r
    gk = jnp.zeros((t, LANES), F32).at[:, :TOP_K].set(gates)
    p0 = pos[:, 0].reshape(nt, 1, r)
    p1 = pos[:, 1].reshape(nt, 1, r)
    cur = pl.BlockSpec((1, 1, r), lambda i: (i, 0, 0), memory_space=pltpu.SMEM)
    nxt = pl.BlockSpec((1, 1, r), lambda i: (jnp.minimum(i + 1, nt - 1), 0, 0), memory_space=pltpu.SMEM)
    fg = jnp.ones((1, d), F32) if final_g is None else final_g.reshape(1, d).astype(F32)
    return pl.pallas_call(
        functools.partial(_combine_kernel, final_norm=final_g is not None), grid=(nt,),
        in_specs=[cur, cur, nxt, nxt,
                  pl.BlockSpec(memory_space=pl.ANY),
                  pl.BlockSpec((r, d), lambda i: (i + nct, 0)),
                  pl.BlockSpec((r, LANES), lambda i: (i, 0)),
                  pl.BlockSpec((1, d), lambda i: (0, 0)),
                  pl.BlockSpec((1, d), lambda i: (0, 0))],
        out_specs=pl.BlockSpec((r, d), lambda i: (i, 0)),
        out_shape=jax.ShapeDtypeStruct((t, d), F32),
        scratch_shapes=[pltpu.VMEM((2, 2, r, d), F32), pltpu.SemaphoreType.DMA((2,))],
        compiler_params=_cparams("arbitrary"), name="moe_combine",
    )(p0, p1, p0, p1, yr, xs, gk, gate_mod, fg)


def _moe_plan(logits, ne, tm):
    t = logits.shape[0]
    top_val, top_idx = lax.top_k(logits[:, :ne], TOP_K)
    gates = jax.nn.softmax(top_val, axis=-1)
    n_assign = t * TOP_K
    e_flat = top_idx.reshape(-1)
    onehot = (e_flat[:, None] == jnp.arange(ne)[None, :]).astype(jnp.int32)
    rank = jnp.take_along_axis(jnp.cumsum(onehot, axis=0) - onehot, e_flat[:, None], axis=1)[:, 0]
    counts = jnp.sum(onehot, axis=0)
    padded = (counts + tm - 1) // tm * tm
    pend = jnp.cumsum(padded)
    pstart = pend - padded
    pos = (pstart[e_flat] + rank).astype(jnp.int32)
    n_tiles = (n_assign + ne * (tm - 1)) // tm
    n_rows = n_tiles * tm
    tok = jnp.repeat(jnp.arange(t, dtype=jnp.int32), TOP_K)
    row_tok = jnp.full((n_rows,), -1, jnp.int32).at[pos].set(tok)
    tile_start = jnp.arange(n_tiles, dtype=jnp.int32) * tm
    tile_expert = jnp.minimum(jnp.searchsorted(pend, tile_start, side='right'), ne - 1).astype(jnp.int32)
    n_used = (pend[-1] // tm).astype(jnp.int32)
    return gates, pos.reshape(t, TOP_K), row_tok, tile_expert, n_used


def _final_norm_kernel(x_ref, g_ref, o_ref):
    x = x_ref[...]
    o_ref[...] = x * lax.rsqrt(jnp.mean(x * x, axis=-1, keepdims=True) + EPS) * g_ref[...]


def _final_norm(xs, g, n_ctx):
    s, d = xs.shape
    t = s - n_ctx
    r = _pick(math.gcd(n_ctx, t), ROW_TILE, 8)
    nct = n_ctx // r
    return pl.pallas_call(
        _final_norm_kernel, grid=(t // r,),
        in_specs=[pl.BlockSpec((r, d), lambda i: (i + nct, 0)), pl.BlockSpec((1, d), lambda i: (0, 0))],
        out_specs=pl.BlockSpec((r, d), lambda i: (i, 0)),
        out_shape=jax.ShapeDtypeStruct((t, d), F32),
        compiler_params=_cparams("parallel"), name="final_rmsnorm",
    )(xs, g.reshape(1, d))


def _to_colmajor(a, n_ctx):
    t, ch = a.shape[0] - n_ctx, a.shape[1]
    lat = a[n_ctx:].reshape(t // GRID_W, GRID_W, ch).transpose(1, 0, 2).reshape(t, ch)
    return jnp.concatenate([a[:n_ctx], lat], axis=0)


def _from_colmajor(a, n_ctx):
    t, ch = a.shape[0] - n_ctx, a.shape[1]
    lat = a[n_ctx:].reshape(GRID_W, t // GRID_W, ch).transpose(1, 0, 2).reshape(t, ch)
    return jnp.concatenate([a[:n_ctx], lat], axis=0)


def kernel(x, c, ctx, c_ctx, w_mod, b_mod, norm1_g, norm2_g, w_in, gate_b, conv_w, conv_b, a_log, dt_bias, d_skip, mlstm_g, ssm_g, w_out, ffn_w1, ffn_w3, ffn_w2, router_w, moe_w1, moe_w3, moe_w2, final_g):
    bsz, t, d = x.shape
    assert bsz == 1, "the kernels treat the single batch element's tokens as rows"
    n_ctx = ctx.shape[1]
    depth = w_mod.shape[0]
    nh = gate_b.shape[1] // 4
    mw = mlstm_g.shape[1]
    dv = mw // nh
    dqk = dv // 2
    sw = ssm_g.shape[1]
    heads = a_log.shape[2]
    conv_ch = conv_w.shape[2]
    n_state = (conv_ch - sw) // (2 * SSM_GROUPS)
    wq = nh * dqk
    splits = (wq, wq, mw, mw, 4 * nh, sw, conv_ch, 2 * heads)
    offs = [0]
    for w_ in splits:
        offs.append(offs[-1] + w_)
    assert offs[-1] == w_in.shape[2]
    ne = router_w.shape[2]

    xs = jnp.concatenate([ctx[0], x[0]], axis=0).astype(F32)
    mod = _modulation(c_ctx.astype(F32), c[0].astype(F32), w_mod.astype(F32), b_mod.astype(F32))

    col_q, col_k, col_v, col_o, col_z, col_x = 0, wq, 2 * wq, 2 * wq + mw, 2 * wq + 2 * mw, 2 * wq + 2 * mw + sw
    n_small = -(-(4 * nh + 2 * heads) // LANES) * LANES

    for layer in range(depth):
        need_ctx = layer < depth - 1
        m6 = mod[layer, :2].reshape(2, 6, d)
        shift1, scale1, gate1, shift2, scale2, gate2 = (m6[:, i][:, None, :] for i in range(6))

        wt = jnp.swapaxes(w_in[layer], 0, 1)
        w_big = jnp.concatenate([wt[offs[0]:offs[4]], wt[offs[5]:offs[7]]], axis=0).astype(BF16)
        w_small = jnp.concatenate([wt[offs[4]:offs[5]], wt[offs[7]:offs[8]],
                                   jnp.zeros((n_small - 4 * nh - 2 * heads, d), F32)], axis=0).astype(BF16)
        xn = _norm_mod(xs, norm1_g[layer], scale1, shift1, n_ctx)
        big = _matmul_nt(xn, w_big, BF16)
        small = _matmul_nt(xn, w_small, F32)

        hf, hb = _mlstm(big, small[:, :4 * nh], gate_b[layer], nh, dqk, dv, n_ctx, col_q, col_k, col_v)

        xbc = _dwconv_silu(_to_colmajor(big[:, col_x:], n_ctx), conv_w[layer], conv_b[layer], n_ctx)
        dt_raw = _to_colmajor(small[:, 4 * nh:4 * nh + 2 * heads], n_ctx)
        yf, yb = _ssd(xbc, dt_raw, a_log[layer], dt_bias[layer], d_skip[layer], n_ctx, sw, n_state)
        yf, yb = _from_colmajor(yf, n_ctx), _from_colmajor(yb, n_ctx)

        mix = _finish(hf, hb, big, yf, yb, mlstm_g[layer], ssm_g[layer], nh, dv, col_o, col_z)
        xs = _matmul_residual_stationary(mix, w_out, layer, xs, gate1, n_ctx)

        i = layer // 2
        if layer % 2 == 0:
            hn = _norm_mod(xs, norm2_g[layer], scale2, shift2, n_ctx)
            tm = _pick(hn.shape[0], FFN_TM, 8)
            nt = hn.shape[0] // tm
            act = _glu(hn, ffn_w1[i:i + 1], ffn_w3[i:i + 1], tm, jnp.zeros((nt,), jnp.int32), nt)
            xs = _matmul_residual(act, ffn_w2[i].astype(BF16), xs, gate2, n_ctx, act.shape[1] // 4)
        else:
            assert not need_ctx, "an MoE layer that still feeds context is not needed by this block"
            hn, logits = _norm_mod(xs, norm2_g[layer], scale2, shift2, n_ctx, router_w=router_w[i])
            tm = min(MOE_TM, t)
            gates, pos, row_tok, tile_expert, n_used = _moe_plan(logits[n_ctx:], ne, tm)
            xr = _gather_rows(hn, jnp.where(row_tok >= 0, row_tok + n_ctx, -1))
            act = _glu(xr, moe_w1[i], moe_w3[i], tm, tile_expert, n_used)
            yr = _matmul_expert(act, moe_w2[i], tm, tile_expert, n_used)
            return _moe_combine(xs, yr, pos, gates, gate2[1], n_ctx, final_g)[None]

    return _final_norm(xs, final_g, n_ctx)[None]
```

```python
import functools
import math

import jax
import jax.numpy as jnp
from jax import lax
from jax.experimental import pallas as pl
from jax.experimental.pallas import tpu as pltpu

GRID_W = 64
SSM_GROUPS = 8
TOP_K = 2
EPS = 1e-6

V7X_VMEM_LIMIT_BYTES = 56 * 1024 * 1024
V7X_VMEM_LIMIT_BIG_BYTES = 60 * 1024 * 1024
LANES = 128
SCAN_CHUNK = 256
MLSTM_HEADS_PER_STEP = 4
SSD_GROUPS_PER_STEP = 4
ROW_TILE = 256
MOD_TK = 128
MM_TM = 768
MM_TN = 1024
GLU_TN = 512
FFN_TM = 528
DOWN_TN = 1024
MOE_TM = 512
GATHER_ROWS = 256
COMBINE_ROWS = 128

F32 = jnp.float32
BF16 = jnp.bfloat16
HIGHEST = lax.Precision.HIGHEST


def _pick(dim, pref, mult):
    if dim <= pref:
        return dim
    t = (pref // mult) * mult
    while t >= mult:
        if dim % t == 0:
            return t
        t -= mult
    return dim


def _cparams(*sem, vmem=V7X_VMEM_LIMIT_BYTES):
    return pltpu.CompilerParams(dimension_semantics=sem, vmem_limit_bytes=vmem)


def _sigmoid(v):
    return 1.0 / (1.0 + jnp.exp(-v))


def _silu(v):
    return v * _sigmoid(v)


def _softplus(v):
    return jnp.maximum(v, 0.0) + jnp.log1p(jnp.exp(-jnp.abs(v)))


def _log_sigmoid(v):
    return -_softplus(-v)


def _split3(v):
    hi = v.astype(BF16)
    r = v - hi.astype(F32)
    mid = r.astype(BF16)
    lo = (r - mid.astype(F32)).astype(BF16)
    return hi, mid, lo


def _dot_exact(mat01, v, left):
    m = mat01.astype(BF16)
    out = None
    for piece in _split3(v):
        part = (jnp.dot(m, piece, preferred_element_type=F32) if left
                else jnp.dot(piece, m, preferred_element_type=F32))
        out = part if out is None else out + part
    return out


def _mod_kernel(c_ref, w_ref, b_ref, o_ref, acc, *, nk):
    k = pl.program_id(1)

    @pl.when(k == 0)
    def _():
        acc[...] = jnp.zeros_like(acc)

    s_ctx = _silu(c_ref[0])
    s_lat = _silu(c_ref[1])
    tk = w_ref.shape[0]
    for n0 in range(0, w_ref.shape[1], LANES):
        w = w_ref[:, n0:n0 + LANES]
        acc[0, :, n0:n0 + LANES] += jnp.sum((w * s_ctx).reshape(tk // 8, 8, LANES), axis=0)
        acc[1, :, n0:n0 + LANES] += jnp.sum((w * s_lat).reshape(tk // 8, 8, LANES), axis=0)

    @pl.when(k == nk - 1)
    def _():
        o_ref[...] = jnp.zeros_like(o_ref)
        o_ref[0:1, :] = jnp.sum(acc[0], axis=0, keepdims=True) + b_ref[...]
        o_ref[1:2, :] = jnp.sum(acc[1], axis=0, keepdims=True) + b_ref[...]


def _modulation(c_ctx, c_lat, w_mod, b_mod):
    depth, d, n = w_mod.shape
    tk = _pick(d, MOD_TK, 8)
    nk = d // tk
    cb = jnp.broadcast_to(jnp.stack([c_ctx, c_lat])[:, :, None], (2, d, LANES))
    return pl.pallas_call(
        functools.partial(_mod_kernel, nk=nk),
        grid=(depth, nk),
        in_specs=[pl.BlockSpec((2, tk, LANES), lambda l, k: (0, k, 0)),
                  pl.BlockSpec((None, tk, n), lambda l, k: (l, k, 0)),
                  pl.BlockSpec((None, 1, n), lambda l, k: (l, 0, 0))],
        out_specs=pl.BlockSpec((None, 8, n), lambda l, k: (l, 0, 0)),
        out_shape=jax.ShapeDtypeStruct((depth, 8, n), F32),
        scratch_shapes=[pltpu.VMEM((2, 8, n), F32)],
        compiler_params=_cparams("parallel", "arbitrary"),
        name="adaln_modulation",
    )(cb, w_mod, b_mod.reshape(depth, 1, n))


def _norm_kernel(x_ref, g_ref, sc_ref, sh_ref, o_ref):
    x = x_ref[...]
    r = lax.rsqrt(jnp.mean(x * x, axis=-1, keepdims=True) + EPS)
    o_ref[...] = (x * r * g_ref[...] * (1.0 + sc_ref[...]) + sh_ref[...]).astype(o_ref.dtype)


def _pack_bf16_pairs(v):
    half = v.shape[1] // 2
    hi = lax.bitcast_convert_type(v[:, :half].astype(BF16).astype(F32), jnp.uint32)
    lo = lax.bitcast_convert_type(v[:, half:].astype(BF16).astype(F32), jnp.uint32)
    return (hi & jnp.uint32(0xFFFF0000)) | (lo >> 16)


def _unpack_bf16_pairs(p):
    hi = lax.bitcast_convert_type(p & jnp.uint32(0xFFFF0000), F32).astype(BF16)
    lo = lax.bitcast_convert_type(p << 16, F32).astype(BF16)
    return hi, lo


def _norm_router_kernel(x_ref, g_ref, sc_ref, sh_ref, rw_ref, o_ref, lg_ref):
    x = x_ref[...]
    r = lax.rsqrt(jnp.mean(x * x, axis=-1, keepdims=True) + EPS)
    h = x * r * g_ref[...] * (1.0 + sc_ref[...]) + sh_ref[...]
    o_ref[...] = _pack_bf16_pairs(h)
    lg_ref[...] = jnp.dot(h, rw_ref[...], preferred_element_type=F32, precision=HIGHEST)


def _norm_mod(xs, g, scale, shift, n_ctx, router_w=None):
    s, d = xs.shape
    r = _pick(math.gcd(n_ctx, s - n_ctx), ROW_TILE, 8)
    nct = n_ctx // r
    row_spec = pl.BlockSpec((r, d), lambda i: (i, 0))
    mod_spec = pl.BlockSpec((None, 1, d), lambda i: (jnp.where(i < nct, 0, 1), 0, 0))
    in_specs = [row_spec, pl.BlockSpec((1, d), lambda i: (0, 0)), mod_spec, mod_spec]
    if router_w is None:
        return pl.pallas_call(
            _norm_kernel, grid=(s // r,), in_specs=in_specs, out_specs=row_spec,
            out_shape=jax.ShapeDtypeStruct((s, d), BF16),
            compiler_params=_cparams("parallel"), name="rmsnorm_modulate",
        )(xs, g.reshape(1, d), scale, shift)
    ne = router_w.shape[1]
    rw = jnp.zeros((d, LANES), F32).at[:, :ne].set(router_w)
    return pl.pallas_call(
        _norm_router_kernel, grid=(s // r,),
        in_specs=in_specs + [pl.BlockSpec((d, LANES), lambda i: (0, 0))],
        out_specs=[pl.BlockSpec((r, d // 2), lambda i: (i, 0)), pl.BlockSpec((r, LANES), lambda i: (i, 0))],
        out_shape=[jax.ShapeDtypeStruct((s, d // 2), jnp.uint32), jax.ShapeDtypeStruct((s, LANES), F32)],
        compiler_params=_cparams("parallel"), name="rmsnorm_modulate_router",
    )(xs, g.reshape(1, d), scale, shift, rw)


def _dot_nt(a, wt):
    return lax.dot_general(a, wt, (((1,), (1,)), ((), ())), preferred_element_type=F32)


def _inproj_kernel(a_ref, wt_ref, ws_ref, o_ref, os_ref):
    a = a_ref[...]
    o_ref[...] = _dot_nt(a, wt_ref[...]).astype(o_ref.dtype)

    @pl.when(pl.program_id(1) == 0)
    def _():
        os_ref[...] = _dot_nt(a, ws_ref[...])


def _inproj(a, wt, ws):
    m, k = a.shape
    n, ns = wt.shape[0], ws.shape[0]
    tm = _pick(m, MM_TM, 8)
    tn = _pick(n, MM_TN, LANES)
    return pl.pallas_call(
        _inproj_kernel, grid=(m // tm, n // tn),
        in_specs=[pl.BlockSpec((tm, k), lambda i, j: (i, 0)), pl.BlockSpec((tn, k), lambda i, j: (j, 0)),
                  pl.BlockSpec((ns, k), lambda i, j: (0, 0))],
        out_specs=[pl.BlockSpec((tm, tn), lambda i, j: (i, j)), pl.BlockSpec((tm, ns), lambda i, j: (i, 0))],
        out_shape=[jax.ShapeDtypeStruct((m, n), BF16), jax.ShapeDtypeStruct((m, ns), F32)],
        compiler_params=_cparams("parallel", "arbitrary"), name="in_projection",
    )(a, wt, ws)


def _row_gate(gate_ref, row0, tm, n_ctx):
    rows = row0 + lax.broadcasted_iota(jnp.int32, (tm, 1), 0)
    return jnp.where(rows < n_ctx, gate_ref[0], gate_ref[1])


def _mm_res_kernel(a_ref, w_ref, res_ref, gate_ref, o_ref, acc_ref, *, nk, tm, n_ctx, k_tail):
    k = pl.program_id(2)

    @pl.when(k == 0)
    def _():
        acc_ref[...] = jnp.zeros_like(acc_ref)

    if k_tail:
        @pl.when(k < nk - 1)
        def _():
            acc_ref[...] += jnp.dot(a_ref[...], w_ref[...], preferred_element_type=F32)

        @pl.when(k == nk - 1)
        def _():
            w = w_ref[...]
            w = jnp.where(lax.broadcasted_iota(jnp.int32, w.shape, 0) < k_tail, w, jnp.zeros_like(w))
            acc_ref[...] += jnp.dot(a_ref[...], w, preferred_element_type=F32)
    else:
        acc_ref[...] += jnp.dot(a_ref[...], w_ref[...], preferred_element_type=F32)

    @pl.when(k == nk - 1)
    def _():
        gate = _row_gate(gate_ref, pl.program_id(0) * tm, tm, n_ctx)
        o_ref[...] = res_ref[...] + gate * acc_ref[...]


def _matmul_residual(a, w, res, gate, n_ctx, tk_pref):
    m, kdim = a.shape
    n = w.shape[1]
    tm = _pick(m, MM_TM, 8)
    tn = _pick(n, MM_TN, LANES)
    tk = _pick(kdim, tk_pref, LANES)
    if w.shape[0] <= (kdim // tk - 1) * tk:
        tk = kdim
    nk = kdim // tk
    k_tail = w.shape[0] - (nk - 1) * tk
    assert 0 < k_tail <= tk
    return pl.pallas_call(
        functools.partial(_mm_res_kernel, nk=nk, tm=tm, n_ctx=n_ctx, k_tail=0 if k_tail == tk else k_tail),
        grid=(m // tm, n // tn, nk),
        in_specs=[pl.BlockSpec((tm, tk), lambda i, j, k: (i, k)),
                  pl.BlockSpec((tk, tn), lambda i, j, k: (k, j)),
                  pl.BlockSpec((tm, tn), lambda i, j, k: (i, j)),
                  pl.BlockSpec((2, 1, tn), lambda i, j, k: (0, 0, j))],
        out_specs=pl.BlockSpec((tm, tn), lambda i, j, k: (i, j)),
        out_shape=jax.ShapeDtypeStruct((m, n), F32),
        scratch_shapes=[pltpu.VMEM((tm, tn), F32)],
        compiler_params=_cparams("parallel", "parallel", "arbitrary"), name="matmul_residual",
    )(a, w, res, gate)


def _mm_res_ws_kernel(a_ref, w_ref, res_ref, gate_ref, o_ref, wbuf, *, tm, n_ctx):
    i = pl.program_id(1)

    @pl.when(i == 0)
    def _():
        wbuf[...] = w_ref[...].astype(BF16)

    acc = jnp.dot(a_ref[...], wbuf[...], preferred_element_type=F32)
    o_ref[...] = res_ref[...] + _row_gate(gate_ref, i * tm, tm, n_ctx) * acc


def _matmul_residual_stationary(a, w, layer, res, gate, n_ctx):
    m, k = a.shape
    n = w.shape[2]
    tm = _pick(m, MM_TM, 8)
    tn = _pick(n, GLU_TN, LANES)
    return pl.pallas_call(
        functools.partial(_mm_res_ws_kernel, tm=tm, n_ctx=n_ctx),
        grid=(n // tn, m // tm),
        in_specs=[pl.BlockSpec((tm, k), lambda j, i: (i, 0)),
                  pl.BlockSpec((None, k, tn), lambda j, i: (layer, 0, j)),
                  pl.BlockSpec((tm, tn), lambda j, i: (i, j)),
                  pl.BlockSpec((2, 1, tn), lambda j, i: (0, 0, j))],
        out_specs=pl.BlockSpec((tm, tn), lambda j, i: (i, j)),
        out_shape=jax.ShapeDtypeStruct((m, n), F32),
        scratch_shapes=[pltpu.VMEM((k, tn), BF16)],
        compiler_params=_cparams("arbitrary", "arbitrary"), name="matmul_residual_ws",
    )(a, w, res, gate)


def _stationary_plan(tile_expert, n_used, n_experts, nj):
    n_tiles = tile_expert.shape[0]
    tile_ids = jnp.arange(n_tiles, dtype=jnp.int32)
    cnt = jnp.sum(((tile_expert[:, None] == jnp.arange(n_experts)[None, :])
                   & (tile_ids[:, None] < n_used)).astype(jnp.int32), axis=0)
    seg_end = jnp.cumsum(cnt)
    seg_start = seg_end - cnt
    step = jnp.arange(n_tiles * nj, dtype=jnp.int32)
    used_steps = n_used * nj
    e = jnp.minimum(jnp.searchsorted(seg_end * nj, step, side='right'), n_experts - 1).astype(jnp.int32)
    seg_len = jnp.maximum(cnt[e], 1)
    local = jnp.maximum(step - nj * seg_start[e], 0)
    j = local // seg_len
    i = seg_start[e] + local % seg_len
    used = step < used_steps
    last = used_steps - 1
    fill = step - used_steps
    pick = lambda arr: jnp.where(used, arr, arr[last]).astype(jnp.int32)
    out_i = jnp.where(used, i, n_used + fill // nj).astype(jnp.int32)
    out_j = jnp.where(used, j, fill % nj).astype(jnp.int32)
    return pick(e), pick(j), pick(i), out_i, out_j, used.astype(jnp.int32)


def _stationary_flags(e_ref, jw_ref, used_ref):
    s = pl.program_id(0)
    prev = jnp.maximum(s - 1, 0)
    fresh = jnp.logical_or(s == 0, jnp.logical_or(e_ref[s] != e_ref[prev], jw_ref[s] != jw_ref[prev]))
    used = used_ref[s] > 0
    return used, jnp.logical_and(used, fresh)


def _glu_kernel(e_ref, jw_ref, ia_ref, io_ref, jo_ref, used_ref, a_ref, w1_ref, w3_ref, o_ref, w1b, w3b,
                *, n_valid, ragged):
    used, fresh = _stationary_flags(e_ref, jw_ref, used_ref)

    @pl.when(fresh)
    def _():
        w1 = w1_ref[...]
        w3 = w3_ref[...]
        if ragged:
            col = jw_ref[pl.program_id(0)] * w1.shape[1] + lax.broadcasted_iota(jnp.int32, w1.shape, 1)
            w1 = jnp.where(col < n_valid, w1, 0.0)
            w3 = jnp.where(col < n_valid, w3, 0.0)
        w1b[...] = w1.astype(BF16)
        w3b[...] = w3.astype(BF16)

    @pl.when(used)
    def _():
        a = a_ref[...]
        h1 = jnp.dot(a, w1b[...], preferred_element_type=F32)
        h3 = jnp.dot(a, w3b[...], preferred_element_type=F32)
        o_ref[...] = (_silu(h1) * h3).astype(o_ref.dtype)

    @pl.when(jnp.logical_not(used))
    def _():
        o_ref[...] = jnp.zeros_like(o_ref)


def _stationary_specs(tm, k, tn):
    a_spec = pl.BlockSpec((tm, k), lambda s, e, jw, ia, io, jo, u: (ia[s], 0))
    w_spec = pl.BlockSpec((None, k, tn), lambda s, e, jw, ia, io, jo, u: (e[s], 0, jw[s]))
    o_spec = pl.BlockSpec((tm, tn), lambda s, e, jw, ia, io, jo, u: (io[s], jo[s]))
    return a_spec, w_spec, o_spec


def _glu(a, w1, w3, tm, tile_expert, n_used):
    m, k = a.shape
    ne, _, n = w1.shape
    tn = GLU_TN if n % LANES == 0 and n > GLU_TN else n
    nj = pl.cdiv(n, tn)
    plan = _stationary_plan(tile_expert, n_used, ne, nj)
    a_spec, w_spec, o_spec = _stationary_specs(tm, k, tn)
    return pl.pallas_call(
        functools.partial(_glu_kernel, n_valid=n, ragged=n % tn != 0),
        grid_spec=pltpu.PrefetchScalarGridSpec(
            num_scalar_prefetch=6, grid=(plan[0].shape[0],),
            in_specs=[a_spec, w_spec, w_spec], out_specs=o_spec,
            scratch_shapes=[pltpu.VMEM((k, tn), BF16)] * 2),
        out_shape=jax.ShapeDtypeStruct((m, nj * tn), BF16),
        compiler_params=_cparams("arbitrary", vmem=V7X_VMEM_LIMIT_BIG_BYTES), name="swiglu_up",
    )(*plan, a, w1, w3)


def _mm_expert_kernel(e_ref, jw_ref, ia_ref, io_ref, jo_ref, used_ref, a_ref, w_ref, o_ref, wb):
    used, fresh = _stationary_flags(e_ref, jw_ref, used_ref)

    @pl.when(fresh)
    def _():
        wb[...] = w_ref[...].astype(BF16)

    @pl.when(used)
    def _():
        o_ref[...] = jnp.dot(a_ref[...], wb[...], preferred_element_type=F32).astype(o_ref.dtype)

    @pl.when(jnp.logical_not(used))
    def _():
        o_ref[...] = jnp.zeros_like(o_ref)


def _matmul_expert(a, w, tm, tile_expert, n_used):
    m, k = a.shape
    ne, _, n = w.shape
    tn = _pick(n, DOWN_TN, LANES)
    nj = n // tn
    plan = _stationary_plan(tile_expert, n_used, ne, nj)
    a_spec, w_spec, o_spec = _stationary_specs(tm, k, tn)
    return pl.pallas_call(
        _mm_expert_kernel,
        grid_spec=pltpu.PrefetchScalarGridSpec(
            num_scalar_prefetch=6, grid=(plan[0].shape[0],),
            in_specs=[a_spec, w_spec], out_specs=o_spec,
            scratch_shapes=[pltpu.VMEM((k, tn), BF16)]),
        out_shape=jax.ShapeDtypeStruct((m, n), F32),
        compiler_params=_cparams("arbitrary", vmem=V7X_VMEM_LIMIT_BIG_BYTES), name="expert_down",
    )(*plan, a, w)


def _chunk_maps(nct, nlt):
    def fwd(s):
        return s

    def bwd(s):
        return jnp.where(s < nct, nct - 1 - s, 2 * nct + nlt - 1 - s)

    return fwd, bwd


def _tri(l, lower):
    r = lax.broadcasted_iota(jnp.int32, (l, l), 0)
    c = lax.broadcasted_iota(jnp.int32, (l, l), 1)
    return (r >= c) if lower else (r <= c)


def _mlstm_direction(q, k, v, g, gt, c_ref, n_ref, m_ref, o_ref, *, fwd, scale):
    l = q.shape[0]
    ci, cf = (0, 2) if fwd else (1, 3)
    low = _tri(l, True).astype(F32)
    up = _tri(l, False).astype(F32)
    b_col = _dot_exact(low if fwd else up, _log_sigmoid(g), True)[:, cf:cf + 1]
    b_row = _dot_exact(up if fwd else low, _log_sigmoid(gt), False)[cf:cf + 1, :]
    i_col = g[:, ci:ci + 1]
    i_row = gt[ci:ci + 1, :]
    causal = _tri(l, fwd)
    logw = jnp.where(causal, b_col - b_row + i_row, -jnp.inf)
    m_intra = jnp.max(logw, axis=-1, keepdims=True)
    m_st = m_ref[...]
    inter_log = b_col + m_st
    m_t = jnp.maximum(inter_log, m_intra)
    a_inter = jnp.exp(inter_log - m_t)
    ks = (k.astype(F32) * scale)
    ksb = ks.astype(BF16)
    qk = lax.dot_general(q, ksb, (((1,), (1,)), ((), ())), preferred_element_type=F32)
    s = qk * jnp.exp(logw - m_t)
    c_st = c_ref[...]
    n_st = n_ref[...]
    num = a_inter * jnp.dot(q, c_st.astype(BF16), preferred_element_type=F32) \
        + jnp.dot(s.astype(BF16), v, preferred_element_type=F32)
    den = a_inter * jnp.sum(q.astype(F32) * n_st, axis=-1, keepdims=True) + jnp.sum(s, axis=-1, keepdims=True)
    o_ref[...] = (num / jnp.maximum(jnp.abs(den), jnp.exp(-m_t))).astype(o_ref.dtype)
    b_last = b_col[l - 1:l, :] if fwd else b_col[0:1, :]
    g_col = b_last - b_col + i_col
    m_new = jnp.maximum(b_last + m_st, jnp.max(g_col, axis=0, keepdims=True))
    kw = ks * jnp.exp(g_col - m_new)
    dec = jnp.exp(b_last + m_st - m_new)
    c_ref[...] = dec * c_st + lax.dot_general(kw.astype(BF16), v, (((0,), (0,)), ((), ())),
                                              preferred_element_type=F32)
    n_ref[...] = dec * n_st + jnp.sum(kw, axis=0, keepdims=True)
    m_ref[...] = m_new


def _mlstm_kernel(qf, kf, vf, gf, gtf, qb, kb, vb, gb, gtb, bias, bias_t, hf, hb,
                  cf, nf, mf, cb, nb, mb, *, scale, dqk, dv):
    @pl.when(pl.program_id(1) == 0)
    def _():
        for ref in (cf, nf, mf, cb, nb, mb):
            ref[...] = jnp.zeros_like(ref)

    for h in range(cf.shape[0]):
        qs, vs = slice(h * dqk, (h + 1) * dqk), slice(h * dv, (h + 1) * dv)
        _mlstm_direction(qf[:, qs], kf[:, qs], vf[:, vs], gf[h] + bias[h], gtf[h] + bias_t[h],
                         cf.at[h], nf.at[h], mf.at[h], hf.at[:, pl.ds(h * dv, dv)], fwd=True, scale=scale)
        _mlstm_direction(qb[:, qs], kb[:, qs], vb[:, vs], gb[h] + bias[h], gtb[h] + bias_t[h],
                         cb.at[h], nb.at[h], mb.at[h], hb.at[:, pl.ds(h * dv, dv)], fwd=False, scale=scale)


def _mlstm(big, gates, gate_b, nh, dqk, dv, n_ctx, col_q, col_k, col_v):
    s = big.shape[0]
    l = _pick(math.gcd(n_ctx, s - n_ctx), SCAN_CHUNK, 8)
    nct, nlt = n_ctx // l, (s - n_ctx) // l
    fwd, bwd = _chunk_maps(nct, nlt)
    gh = gates.reshape(s, 4, nh).transpose(2, 0, 1)
    g_cols = jnp.zeros((nh, s, LANES), F32).at[:, :, :4].set(gh)
    g_rows = jnp.zeros((nh, 8, s), F32).at[:, :4, :].set(gh.transpose(0, 2, 1))
    bh = gate_b.reshape(4, nh).T
    b_cols = jnp.zeros((nh, 1, LANES), F32).at[:, 0, :4].set(bh)
    b_rows = jnp.zeros((nh, 8, 1), F32).at[:, :4, 0].set(bh)
    hp = MLSTM_HEADS_PER_STEP if nh % MLSTM_HEADS_PER_STEP == 0 else 1
    qw, vw = hp * dqk, hp * dv
    assert col_q % qw == 0 and col_k % qw == 0 and col_v % vw == 0
    qo, ko, vo = col_q // qw, col_k // qw, col_v // vw

    def specs(cm):
        return [pl.BlockSpec((l, qw), lambda h, t: (cm(t), qo + h)),
                pl.BlockSpec((l, qw), lambda h, t: (cm(t), ko + h)),
                pl.BlockSpec((l, vw), lambda h, t: (cm(t), vo + h)),
                pl.BlockSpec((hp, l, LANES), lambda h, t: (h, cm(t), 0)),
                pl.BlockSpec((hp, 8, l), lambda h, t: (h, 0, cm(t)))]

    out_f = pl.BlockSpec((l, vw), lambda h, t: (fwd(t), h))
    out_b = pl.BlockSpec((l, vw), lambda h, t: (bwd(t), h))
    state = [pltpu.VMEM((hp, dqk, dv), F32), pltpu.VMEM((hp, 1, dqk), F32), pltpu.VMEM((hp, 1, 1), F32)]
    return pl.pallas_call(
        functools.partial(_mlstm_kernel, scale=dqk ** -0.5, dqk=dqk, dv=dv),
        grid=(nh // hp, nct + nlt),
        in_specs=specs(fwd) + specs(bwd) + [pl.BlockSpec((hp, 1, LANES), lambda h, t: (h, 0, 0)),
                                             pl.BlockSpec((hp, 8, 1), lambda h, t: (h, 0, 0))],
        out_specs=[out_f, out_b],
        out_shape=[jax.ShapeDtypeStruct((s, nh * dv), BF16)] * 2,
        scratch_shapes=state + state,
        compiler_params=_cparams("parallel", "arbitrary"), name="mlstm_scan",
    )(big, big, big, g_cols, g_rows, big, big, big, g_cols, g_rows, b_cols, b_rows)


def _conv_kernel(u_ref, prev_ref, next_ref, w_ref, b_ref, o_ref, *, nct, nt, width, halo):
    i = pl.program_id(0)
    first = jnp.logical_or(i == 0, i == nct)
    last = jnp.logical_or(i == nct - 1, i == nt - 1)
    r = u_ref.shape[0]
    prev = jnp.where(first, 0.0, prev_ref[...].astype(F32))
    nxt = jnp.where(last, 0.0, next_ref[...].astype(F32))
    ext = jnp.concatenate([prev, u_ref[...].astype(F32), nxt], axis=0)
    w = w_ref[...]
    acc = b_ref[...] + w[0:1, :] * ext[halo - width // 2: halo - width // 2 + r]
    for j in range(1, width):
        off = halo - width // 2 + j
        acc = acc + w[j:j + 1, :] * ext[off: off + r]
    o_ref[...] = _silu(acc).astype(o_ref.dtype)


def _dwconv_silu(u, conv_w, conv_b, n_ctx):
    s, ch = u.shape
    width = conv_w.shape[0]
    halo = 16
    r = _pick(math.gcd(n_ctx, s - n_ctx), ROW_TILE, halo)
    tc = _pick(ch, 1024, LANES)
    nt, nct, rb = s // r, n_ctx // r, r // halo
    w8 = jnp.zeros((8, ch), F32).at[:width].set(conv_w)
    return pl.pallas_call(
        functools.partial(_conv_kernel, nct=nct, nt=nt, width=width, halo=halo),
        grid=(nt, ch // tc),
        in_specs=[pl.BlockSpec((r, tc), lambda i, j: (i, j)),
                  pl.BlockSpec((halo, tc), lambda i, j: (jnp.maximum(i * rb - 1, 0), j)),
                  pl.BlockSpec((halo, tc), lambda i, j: (jnp.minimum((i + 1) * rb, s // halo - 1), j)),
                  pl.BlockSpec((8, tc), lambda i, j: (0, j)),
                  pl.BlockSpec((1, tc), lambda i, j: (0, j))],
        out_specs=pl.BlockSpec((r, tc), lambda i, j: (i, j)),
        out_shape=jax.ShapeDtypeStruct((s, ch), BF16),
        compiler_params=_cparams("parallel", "parallel"), name="dwconv_silu",
    )(u, u, u, w8, conv_b.reshape(1, ch))


def _per_channel(cols, off, hg, p, lane):
    out = jnp.broadcast_to(cols[:, off + hg - 1:off + hg], lane.shape)
    for h in range(hg - 2, -1, -1):
        out = jnp.where(lane < (h + 1) * p, jnp.broadcast_to(cols[:, off + h:off + h + 1], lane.shape), out)
    return out


def _ssd_direction(x, bm, cm, dtv, da, da_t, st_ref, *, fwd, hg, p):
    l = x.shape[0]
    off = 0 if fwd else hg
    low = _tri(l, True)
    up = _tri(l, False)
    acum_col = _dot_exact(low if fwd else up, da, True)
    acum_row = _dot_exact(up if fwd else low, da_t, False)
    lane = lax.broadcasted_iota(jnp.int32, x.shape, 1)
    acx = _per_channel(acum_col, off, hg, p, lane)
    xdt = x.astype(F32) * _per_channel(dtv, off, hg, p, lane)
    xdt_b = xdt.astype(BF16)
    causal = _tri(l, fwd)
    cb = lax.dot_general(cm, bm, (((1,), (1,)), ((), ())), preferred_element_type=F32)
    y = None
    for h in range(hg):
        seg = acum_col[:, off + h:off + h + 1] - acum_row[off + h:off + h + 1, :]
        mix = (cb * jnp.exp(jnp.where(causal, seg, -jnp.inf))).astype(BF16)
        own = jnp.logical_and(lane >= h * p, lane < (h + 1) * p)
        part = jnp.dot(mix, jnp.where(own, xdt_b, jnp.zeros_like(xdt_b)), preferred_element_type=F32)
        y = part if y is None else y + part
    st = st_ref[...]
    y = y + jnp.dot(cm, st.astype(BF16), preferred_element_type=F32) * jnp.exp(acx)
    total = acx[l - 1:l, :] if fwd else acx[0:1, :]
    xw = (xdt * jnp.exp(total - acx)).astype(BF16)
    st_ref[...] = jnp.exp(total) * st + lax.dot_general(bm, xw, (((0,), (0,)), ((), ())),
                                                        preferred_element_type=F32)
    return y


def _ssd_kernel(xf, bf, cf, dtf, dttf, xb, bb, cb, dtb, dttb, a_c, a_r, db_c, db_r, dsk,
                yf, yb, stf, stb, *, hg, p, n):
    @pl.when(pl.program_id(1) == 0)
    def _():
        stf[...] = jnp.zeros_like(stf)
        stb[...] = jnp.zeros_like(stb)

    gw = hg * p
    for g in range(stf.shape[0]):
        def steps(dt_ref, dtt_ref):
            dtv = _softplus(dt_ref[g] + db_c[g])
            da_t = _softplus(dtt_ref[g] + db_r[g]) * a_r[g]
            return dtv, dtv * a_c[g], da_t

        xs, ns = slice(g * gw, (g + 1) * gw), slice(g * n, (g + 1) * n)
        x = xf[:, xs]
        y = _ssd_direction(x, bf[:, ns], cf[:, ns], *steps(dtf, dttf), stf.at[g], fwd=True, hg=hg, p=p)
        yf[:, xs] = (y + dsk[g] * x.astype(F32)).astype(yf.dtype)
        y = _ssd_direction(xb[:, xs], bb[:, ns], cb[:, ns], *steps(dtb, dttb), stb.at[g], fwd=False, hg=hg, p=p)
        yb[:, xs] = y.astype(yb.dtype)


def _ssd(xbc, dt_raw, a_log, dt_bias, d_skip, n_ctx, sw, n_state):
    s = xbc.shape[0]
    h = a_log.shape[1]
    g = SSM_GROUPS
    hg, p = h // g, sw // h
    gw = hg * p
    l = _pick(math.gcd(n_ctx, s - n_ctx), SCAN_CHUNK, 8)
    nct, nlt = n_ctx // l, (s - n_ctx) // l
    fwd, bwd = _chunk_maps(nct, nlt)
    dtg = dt_raw.reshape(s, 2, g, hg).transpose(2, 0, 1, 3).reshape(g, s, 2 * hg)
    dt_cols = jnp.zeros((g, s, LANES), F32).at[:, :, :2 * hg].set(dtg)
    assert 2 * hg <= 8 and sw % n_state == 0
    dt_rows = jnp.zeros((g, 8, s), F32).at[:, :2 * hg, :].set(dtg.transpose(0, 2, 1))
    a = (-jnp.exp(a_log.astype(F32))).reshape(2, g, hg).transpose(1, 0, 2).reshape(g, 2 * hg)
    db = dt_bias.reshape(2, g, hg).transpose(1, 0, 2).reshape(g, 2 * hg)
    a_c = jnp.zeros((g, 1, LANES), F32).at[:, 0, :2 * hg].set(a)
    a_r = jnp.zeros((g, 8, 1), F32).at[:, :2 * hg, 0].set(a)
    db_c = jnp.zeros((g, 1, LANES), F32).at[:, 0, :2 * hg].set(db)
    db_r = jnp.zeros((g, 8, 1), F32).at[:, :2 * hg, 0].set(db)
    dsk = jnp.repeat(d_skip, p).reshape(g, 1, gw)
    gp = SSD_GROUPS_PER_STEP if g % SSD_GROUPS_PER_STEP == 0 else 1
    xw, nw = gp * gw, gp * n_state
    assert sw % nw == 0 and (g * n_state) % nw == 0
    bo, co = sw // nw, (sw + g * n_state) // nw

    def specs(cm):
        return [pl.BlockSpec((l, xw), lambda gi, t: (cm(t), gi)),
                pl.BlockSpec((l, nw), lambda gi, t: (cm(t), bo + gi)),
                pl.BlockSpec((l, nw), lambda gi, t: (cm(t), co + gi)),
                pl.BlockSpec((gp, l, LANES), lambda gi, t: (gi, cm(t), 0)),
                pl.BlockSpec((gp, 8, l), lambda gi, t: (gi, 0, cm(t)))]

    par_c = pl.BlockSpec((gp, 1, LANES), lambda gi, t: (gi, 0, 0))
    par_r = pl.BlockSpec((gp, 8, 1), lambda gi, t: (gi, 0, 0))
    return pl.pallas_call(
        functools.partial(_ssd_kernel, hg=hg, p=p, n=n_state),
        grid=(g // gp, nct + nlt),
        in_specs=specs(fwd) + specs(bwd) + [par_c, par_r, par_c, par_r,
                                             pl.BlockSpec((gp, 1, gw), lambda gi, t: (gi, 0, 0))],
        out_specs=[pl.BlockSpec((l, xw), lambda gi, t: (fwd(t), gi)),
                   pl.BlockSpec((l, xw), lambda gi, t: (bwd(t), gi))],
        out_shape=[jax.ShapeDtypeStruct((s, sw), BF16)] * 2,
        scratch_shapes=[pltpu.VMEM((gp, n_state, gw), F32)] * 2,
        compiler_params=_cparams("parallel", "arbitrary"), name="ssd_scan",
    )(xbc, xbc, xbc, dt_cols, dt_rows, xbc, xbc, xbc, dt_cols, dt_rows, a_c, a_r, db_c, db_r, dsk)


def _finish_kernel(hf, hb, o, yf, yb, z, mg, sg, out, *, nh, dv):
    mw = nh * dv
    h = hf[...].astype(F32) + hb[...].astype(F32)
    gate = _sigmoid(o[...].astype(F32))
    mgv = mg[...]
    for k in range(nh):
        hk = h[:, k * dv:(k + 1) * dv]
        r = lax.rsqrt(jnp.mean(hk * hk, axis=-1, keepdims=True) + EPS)
        out[:, k * dv:(k + 1) * dv] = (hk * r * mgv[:, k * dv:(k + 1) * dv]
                                       * gate[:, k * dv:(k + 1) * dv]).astype(out.dtype)
    y = (yf[...].astype(F32) + yb[...].astype(F32)) * _silu(z[...].astype(F32))
    r = lax.rsqrt(jnp.mean(y * y, axis=-1, keepdims=True) + EPS)
    out[:, mw:] = (y * r * sg[...]).astype(out.dtype)


def _finish(hf, hb, big, yf, yb, mlstm_g, ssm_g, nh, dv, col_o, col_z):
    s, mw = hf.shape
    sw = yf.shape[1]
    r = _pick(s, ROW_TILE, 8)
    return pl.pallas_call(
        functools.partial(_finish_kernel, nh=nh, dv=dv),
        grid=(s // r,),
        in_specs=[pl.BlockSpec((r, mw), lambda i: (i, 0)), pl.BlockSpec((r, mw), lambda i: (i, 0)),
                  pl.BlockSpec((r, mw), lambda i: (i, col_o // mw)),
                  pl.BlockSpec((r, sw), lambda i: (i, 0)), pl.BlockSpec((r, sw), lambda i: (i, 0)),
                  pl.BlockSpec((r, sw), lambda i: (i, col_z // sw)),
                  pl.BlockSpec((1, mw), lambda i: (0, 0)), pl.BlockSpec((1, sw), lambda i: (0, 0))],
        out_specs=pl.BlockSpec((r, mw + sw), lambda i: (i, 0)),
        out_shape=jax.ShapeDtypeStruct((s, mw + sw), BF16),
        compiler_params=_cparams("parallel"), name="mixer_finish",
    )(hf, hb, big, yf, yb, big, mlstm_g.reshape(1, mw), ssm_g.reshape(1, sw))


def _row_copy(src_ref, row, dst_ref, k, sem):
    return pltpu.make_async_copy(src_ref.at[pl.ds(row, 1)], dst_ref.at[pl.ds(k, 1)], sem)


def _gather_kernel(idx_ref, nxt_ref, src_ref, o_ref, buf, sem):
    i = pl.program_id(0)
    n_rows = o_ref.shape[0]
    slot = lax.rem(i, 2)

    groups = n_rows // 8
    spread = groups & (groups - 1) == 0

    def issue(ids_ref, s):
        def body(k, carry):
            kk = (k & (groups - 1)) * 8 + lax.shift_right_logical(k, groups.bit_length() - 1) if spread else k
            row = ids_ref[0, 0, kk]

            @pl.when(row >= 0)
            def _():
                _row_copy(src_ref, row, buf.at[s], kk, sem.at[s]).start()
            return carry
        lax.fori_loop(0, n_rows, body, 0)

    @pl.when(i == 0)
    def _():
        buf[...] = jnp.zeros_like(buf)
        issue(idx_ref, 0)

    @pl.when(i + 1 < pl.num_programs(0))
    def _():
        issue(nxt_ref, 1 - slot)

    def wait(k, carry):
        @pl.when(idx_ref[0, 0, k] >= 0)
        def _():
            _row_copy(src_ref, 0, buf.at[slot], k, sem.at[slot]).wait()
        return carry

    lax.fori_loop(0, n_rows, wait, 0)
    hi, lo = _unpack_bf16_pairs(buf[slot])
    half = o_ref.shape[1] // 2
    o_ref[:, :half] = hi
    o_ref[:, half:] = lo


def _gather_rows(src, idx):
    n = idx.shape[0]
    d = 2 * src.shape[1]
    r = _pick(n, GATHER_ROWS, 8)
    nt = n // r
    ids = idx.reshape(nt, 1, r)
    return pl.pallas_call(
        _gather_kernel, grid=(nt,),
        in_specs=[pl.BlockSpec((1, 1, r), lambda i: (i, 0, 0), memory_space=pltpu.SMEM),
                  pl.BlockSpec((1, 1, r), lambda i: (jnp.minimum(i + 1, nt - 1), 0, 0), memory_space=pltpu.SMEM),
                  pl.BlockSpec(memory_space=pl.ANY)],
        out_specs=pl.BlockSpec((r, d), lambda i: (i, 0)),
        out_shape=jax.ShapeDtypeStruct((n, d), BF16),
        scratch_shapes=[pltpu.VMEM((2, r, d // 2), src.dtype), pltpu.SemaphoreType.DMA((2,))],
        compiler_params=_cparams("arbitrary"), name="moe_gather",
    )(ids, ids, src)


def _combine_kernel(p0_ref, p1_ref, q0_ref, q1_ref, yr_ref, x_ref, gk_ref, gate_ref, fg_ref, o_ref, buf, sem,
                    *, final_norm):
    i = pl.program_id(0)
    n_rows = x_ref.shape[0]
    slot = lax.rem(i, 2)

    def issue(a_ref, b_ref, s):
        def body(k, carry):
            _row_copy(yr_ref, a_ref[0, 0, k], buf.at[s, 0], k, sem.at[s]).start()
            _row_copy(yr_ref, b_ref[0, 0, k], buf.at[s, 1], k, sem.at[s]).start()
            return carry
        lax.fori_loop(0, n_rows, body, 0)

    @pl.when(i == 0)
    def _():
        issue(p0_ref, p1_ref, 0)

    @pl.when(i + 1 < pl.num_programs(0))
    def _():
        issue(q0_ref, q1_ref, 1 - slot)

    def wait(k, carry):
        _row_copy(yr_ref, 0, buf.at[slot, 0], k, sem.at[slot]).wait()
        _row_copy(yr_ref, 0, buf.at[slot, 1], k, sem.at[slot]).wait()
        return carry

    lax.fori_loop(0, n_rows, wait, 0)
    gk = gk_ref[...]
    x = x_ref[...] + gate_ref[...] * (gk[:, 0:1] * buf[slot, 0] + gk[:, 1:2] * buf[slot, 1])
    if final_norm:
        x = x * lax.rsqrt(jnp.mean(x * x, axis=-1, keepdims=True) + EPS) * fg_ref[...]
    o_ref[...] = x


def _moe_combine(xs, yr, pos, gates, gate_mod, n_ctx, final_g):
    s, d = xs.shape
    t = s - n_ctx
    r = _pick(math.gcd(n_ctx, t), COMBINE_ROWS, 8)
    nct, nt = n_ctx // r, t // r
    gk = jnp.zeros((t, LANES), F32).at[:, :TOP_K].set(gates)
    p0 = pos[:, 0].reshape(nt, 1, r)
    p1 = pos[:, 1].reshape(nt, 1, r)
    cur = pl.BlockSpec((1, 1, r), lambda i: (i, 0, 0), memory_space=pltpu.SMEM)
    nxt = pl.BlockSpec((1, 1, r), lambda i: (jnp.minimum(i + 1, nt - 1), 0, 0), memory_space=pltpu.SMEM)
    fg = jnp.ones((1, d), F32) if final_g is None else final_g.reshape(1, d).astype(F32)
    return pl.pallas_call(
        functools.partial(_combine_kernel, final_norm=final_g is not None), grid=(nt,),
        in_specs=[cur, cur, nxt, nxt,
                  pl.BlockSpec(memory_space=pl.ANY),
                  pl.BlockSpec((r, d), lambda i: (i + nct, 0)),
                  pl.BlockSpec((r, LANES), lambda i: (i, 0)),
                  pl.BlockSpec((1, d), lambda i: (0, 0)),
                  pl.BlockSpec((1, d), lambda i: (0, 0))],
        out_specs=pl.BlockSpec((r, d), lambda i: (i, 0)),
        out_shape=jax.ShapeDtypeStruct((t, d), F32),
        scratch_shapes=[pltpu.VMEM((2, 2, r, d), F32), pltpu.SemaphoreType.DMA((2,))],
        compiler_params=_cparams("arbitrary"), name="moe_combine",
    )(p0, p1, p0, p1, yr, xs, gk, gate_mod, fg)


def _moe_plan(logits, ne, tm):
    t = logits.shape[0]
    top_val, top_idx = lax.top_k(logits[:, :ne], TOP_K)
    gates = jax.nn.softmax(top_val, axis=-1)
    n_assign = t * TOP_K
    e_flat = top_idx.reshape(-1)
    onehot = (e_flat[:, None] == jnp.arange(ne)[None, :]).astype(jnp.int32)
    rank = jnp.take_along_axis(jnp.cumsum(onehot, axis=0) - onehot, e_flat[:, None], axis=1)[:, 0]
    counts = jnp.sum(onehot, axis=0)
    padded = (counts + tm - 1) // tm * tm
    pend = jnp.cumsum(padded)
    pstart = pend - padded
    pos = (pstart[e_flat] + rank).astype(jnp.int32)
    n_tiles = (n_assign + ne * (tm - 1)) // tm
    n_rows = n_tiles * tm
    tok = jnp.repeat(jnp.arange(t, dtype=jnp.int32), TOP_K)
    row_tok = jnp.full((n_rows,), -1, jnp.int32).at[pos].set(tok)
    tile_start = jnp.arange(n_tiles, dtype=jnp.int32) * tm
    tile_expert = jnp.minimum(jnp.searchsorted(pend, tile_start, side='right'), ne - 1).astype(jnp.int32)
    n_used = (pend[-1] // tm).astype(jnp.int32)
    return gates, pos.reshape(t, TOP_K), row_tok, tile_expert, n_used


def _final_norm_kernel(x_ref, g_ref, o_ref):
    x = x_ref[...]
    o_ref[...] = x * lax.rsqrt(jnp.mean(x * x, axis=-1, keepdims=True) + EPS) * g_ref[...]


def _final_norm(xs, g, n_ctx):
    s, d = xs.shape
    t = s - n_ctx
    r = _pick(math.gcd(n_ctx, t), ROW_TILE, 8)
    nct = n_ctx // r
    return pl.pallas_call(
        _final_norm_kernel, grid=(t // r,),
        in_specs=[pl.BlockSpec((r, d), lambda i: (i + nct, 0)), pl.BlockSpec((1, d), lambda i: (0, 0))],
        out_specs=pl.BlockSpec((r, d), lambda i: (i, 0)),
        out_shape=jax.ShapeDtypeStruct((t, d), F32),
        compiler_params=_cparams("parallel"), name="final_rmsnorm",
    )(xs, g.reshape(1, d))


def _to_colmajor(a, n_ctx):
    t, ch = a.shape[0] - n_ctx, a.shape[1]
    lat = a[n_ctx:].reshape(t // GRID_W, GRID_W, ch).transpose(1, 0, 2).reshape(t, ch)
    return jnp.concatenate([a[:n_ctx], lat], axis=0)


def _from_colmajor(a, n_ctx):
    t, ch = a.shape[0] - n_ctx, a.shape[1]
    lat = a[n_ctx:].reshape(GRID_W, t // GRID_W, ch).transpose(1, 0, 2).reshape(t, ch)
    return jnp.concatenate([a[:n_ctx], lat], axis=0)


def kernel(x, c, ctx, c_ctx, w_mod, b_mod, norm1_g, norm2_g, w_in, gate_b, conv_w, conv_b, a_log, dt_bias, d_skip, mlstm_g, ssm_g, w_out, ffn_w1, ffn_w3, ffn_w2, router_w, moe_w1, moe_w3, moe_w2, final_g):
    bsz, t, d = x.shape
    assert bsz == 1, "the kernels treat the single batch element's tokens as rows"
    n_ctx = ctx.shape[1]
    depth = w_mod.shape[0]
    nh = gate_b.shape[1] // 4
    mw = mlstm_g.shape[1]
    dv = mw // nh
    dqk = dv // 2
    sw = ssm_g.shape[1]
    heads = a_log.shape[2]
    conv_ch = conv_w.shape[2]
    n_state = (conv_ch - sw) // (2 * SSM_GROUPS)
    wq = nh * dqk
    splits = (wq, wq, mw, mw, 4 * nh, sw, conv_ch, 2 * heads)
    offs = [0]
    for w_ in splits:
        offs.append(offs[-1] + w_)
    assert offs[-1] == w_in.shape[2]
    ne = router_w.shape[2]

    xs = jnp.concatenate([ctx[0], x[0]], axis=0).astype(F32)
    mod = _modulation(c_ctx.astype(F32), c[0].astype(F32), w_mod.astype(F32), b_mod.astype(F32))

    col_q, col_k, col_v, col_o, col_z, col_x = 0, wq, 2 * wq, 2 * wq + mw, 2 * wq + 2 * mw, 2 * wq + 2 * mw + sw
    n_small = -(-(4 * nh + 2 * heads) // LANES) * LANES

    for layer in range(depth):
        need_ctx = layer < depth - 1
        m6 = mod[layer, :2].reshape(2, 6, d)
        shift1, scale1, gate1, shift2, scale2, gate2 = (m6[:, i][:, None, :] for i in range(6))

        wt = jnp.swapaxes(w_in[layer], 0, 1)
        w_big = jnp.concatenate([wt[offs[0]:offs[4]], wt[offs[5]:offs[7]]], axis=0).astype(BF16)
        w_small = jnp.concatenate([wt[offs[4]:offs[5]], wt[offs[7]:offs[8]],
                                   jnp.zeros((n_small - 4 * nh - 2 * heads, d), F32)], axis=0).astype(BF16)
        xn = _norm_mod(xs, norm1_g[layer], scale1, shift1, n_ctx)
        big, small = _inproj(xn, w_big, w_small)

        hf, hb = _mlstm(big, small[:, :4 * nh], gate_b[layer], nh, dqk, dv, n_ctx, col_q, col_k, col_v)

        xbc = _dwconv_silu(_to_colmajor(big[:, col_x:], n_ctx), conv_w[layer], conv_b[layer], n_ctx)
        dt_raw = _to_colmajor(small[:, 4 * nh:4 * nh + 2 * heads], n_ctx)
        yf, yb = _ssd(xbc, dt_raw, a_log[layer], dt_bias[layer], d_skip[layer], n_ctx, sw, n_state)
        yf, yb = _from_colmajor(yf, n_ctx), _from_colmajor(yb, n_ctx)

        mix = _finish(hf, hb, big, yf, yb, mlstm_g[layer], ssm_g[layer], nh, dv, col_o, col_z)
        xs = _matmul_residual_stationary(mix, w_out, layer, xs, gate1, n_ctx)

        i = layer // 2
        if layer % 2 == 0:
            hn = _norm_mod(xs, norm2_g[layer], scale2, shift2, n_ctx)
            tm = _pick(hn.shape[0], FFN_TM, 8)
            nt = hn.shape[0] // tm
            act = _glu(hn, ffn_w1[i:i + 1], ffn_w3[i:i + 1], tm, jnp.zeros((nt,), jnp.int32), nt)
            xs = _matmul_residual(act, ffn_w2[i].astype(BF16), xs, gate2, n_ctx, act.shape[1] // 4)
        else:
            assert not need_ctx, "an MoE layer that still feeds context is not needed by this block"
            hn, logits = _norm_mod(xs, norm2_g[layer], scale2, shift2, n_ctx, router_w=router_w[i])
            tm = min(MOE_TM, t)
            gates, pos, row_tok, tile_expert, n_used = _moe_plan(logits[n_ctx:], ne, tm)
            xr = _gather_rows(hn, jnp.where(row_tok >= 0, row_tok + n_ctx, -1))
            act = _glu(xr, moe_w1[i], moe_w3[i], tm, tile_expert, n_used)
            yr = _matmul_expert(act, moe_w2[i], tm, tile_expert, n_used)
            return _moe_combine(xs, yr, pos, gates, gate2[1], n_ctx, final_g)[None]

    return _final_norm(xs, final_g, n_ctx)[None]
```

```python
import functools
import math

import jax
import jax.numpy as jnp
from jax import lax
from jax.experimental import pallas as pl
from jax.experimental.pallas import tpu as pltpu

GRID_W = 64
SSM_GROUPS = 8
TOP_K = 2
EPS = 1e-6

V7X_VMEM_LIMIT_BYTES = 56 * 1024 * 1024
V7X_VMEM_LIMIT_BIG_BYTES = 60 * 1024 * 1024
LANES = 128
SCAN_CHUNK = 256
MLSTM_HEADS_PER_STEP = 4
SSD_GROUPS_PER_STEP = 4
ROW_TILE = 256
MOD_TK = 128
MM_TM = 768
MM_TN = 1024
GLU_TN = 512
FFN_TM = 528
DOWN_TN = 1024
MOE_TM = 512
GATHER_ROWS = 256
COMBINE_ROWS = 128

F32 = jnp.float32
BF16 = jnp.bfloat16
HIGHEST = lax.Precision.HIGHEST


def _pick(dim, pref, mult):
    if dim <= pref:
        return dim
    t = (pref // mult) * mult
    while t >= mult:
        if dim % t == 0:
            return t
        t -= mult
    return dim


def _cparams(*sem, vmem=V7X_VMEM_LIMIT_BYTES):
    return pltpu.CompilerParams(dimension_semantics=sem, vmem_limit_bytes=vmem)


def _sigmoid(v):
    return 1.0 / (1.0 + jnp.exp(-v))


def _silu(v):
    return v * _sigmoid(v)


def _softplus(v):
    return jnp.maximum(v, 0.0) + jnp.log1p(jnp.exp(-jnp.abs(v)))


def _log_sigmoid(v):
    return -_softplus(-v)


def _split3(v):
    hi = v.astype(BF16)
    r = v - hi.astype(F32)
    mid = r.astype(BF16)
    lo = (r - mid.astype(F32)).astype(BF16)
    return hi, mid, lo


def _dot_exact(mat01, v, left):
    m = mat01.astype(BF16)
    out = None
    for piece in _split3(v):
        part = (jnp.dot(m, piece, preferred_element_type=F32) if left
                else jnp.dot(piece, m, preferred_element_type=F32))
        out = part if out is None else out + part
    return out


def _mod_kernel(c_ref, w_ref, b_ref, o_ref, acc, *, nk):
    k = pl.program_id(1)

    @pl.when(k == 0)
    def _():
        acc[...] = jnp.zeros_like(acc)

    s_ctx = _silu(c_ref[0])
    s_lat = _silu(c_ref[1])
    tk = w_ref.shape[0]
    for n0 in range(0, w_ref.shape[1], LANES):
        w = w_ref[:, n0:n0 + LANES]
        acc[0, :, n0:n0 + LANES] += jnp.sum((w * s_ctx).reshape(tk // 8, 8, LANES), axis=0)
        acc[1, :, n0:n0 + LANES] += jnp.sum((w * s_lat).reshape(tk // 8, 8, LANES), axis=0)

    @pl.when(k == nk - 1)
    def _():
        o_ref[...] = jnp.zeros_like(o_ref)
        o_ref[0:1, :] = jnp.sum(acc[0], axis=0, keepdims=True) + b_ref[...]
        o_ref[1:2, :] = jnp.sum(acc[1], axis=0, keepdims=True) + b_ref[...]


def _modulation(c_ctx, c_lat, w_mod, b_mod):
    depth, d, n = w_mod.shape
    tk = _pick(d, MOD_TK, 8)
    nk = d // tk
    cb = jnp.broadcast_to(jnp.stack([c_ctx, c_lat])[:, :, None], (2, d, LANES))
    return pl.pallas_call(
        functools.partial(_mod_kernel, nk=nk),
        grid=(depth, nk),
        in_specs=[pl.BlockSpec((2, tk, LANES), lambda l, k: (0, k, 0)),
                  pl.BlockSpec((None, tk, n), lambda l, k: (l, k, 0)),
                  pl.BlockSpec((None, 1, n), lambda l, k: (l, 0, 0))],
        out_specs=pl.BlockSpec((None, 8, n), lambda l, k: (l, 0, 0)),
        out_shape=jax.ShapeDtypeStruct((depth, 8, n), F32),
        scratch_shapes=[pltpu.VMEM((2, 8, n), F32)],
        compiler_params=_cparams("parallel", "arbitrary"),
        name="adaln_modulation",
    )(cb, w_mod, b_mod.reshape(depth, 1, n))


def _norm_kernel(x_ref, g_ref, sc_ref, sh_ref, o_ref):
    x = x_ref[...]
    r = lax.rsqrt(jnp.mean(x * x, axis=-1, keepdims=True) + EPS)
    o_ref[...] = (x * r * g_ref[...] * (1.0 + sc_ref[...]) + sh_ref[...]).astype(o_ref.dtype)


def _pack_bf16_pairs(v):
    half = v.shape[1] // 2
    hi = lax.bitcast_convert_type(v[:, :half].astype(BF16).astype(F32), jnp.uint32)
    lo = lax.bitcast_convert_type(v[:, half:].astype(BF16).astype(F32), jnp.uint32)
    return (hi & jnp.uint32(0xFFFF0000)) | (lo >> 16)


def _unpack_bf16_pairs(p):
    hi = lax.bitcast_convert_type(p & jnp.uint32(0xFFFF0000), F32).astype(BF16)
    lo = lax.bitcast_convert_type(p << 16, F32).astype(BF16)
    return hi, lo


def _norm_router_kernel(x_ref, g_ref, sc_ref, sh_ref, rw_ref, o_ref, lg_ref):
    x = x_ref[...]
    r = lax.rsqrt(jnp.mean(x * x, axis=-1, keepdims=True) + EPS)
    h = x * r * g_ref[...] * (1.0 + sc_ref[...]) + sh_ref[...]
    o_ref[...] = _pack_bf16_pairs(h)
    lg_ref[...] = jnp.dot(h, rw_ref[...], preferred_element_type=F32, precision=HIGHEST)


def _norm_mod(xs, g, scale, shift, n_ctx, router_w=None):
    s, d = xs.shape
    r = _pick(math.gcd(n_ctx, s - n_ctx), ROW_TILE, 8)
    nct = n_ctx // r
    row_spec = pl.BlockSpec((r, d), lambda i: (i, 0))
    mod_spec = pl.BlockSpec((None, 1, d), lambda i: (jnp.where(i < nct, 0, 1), 0, 0))
    in_specs = [row_spec, pl.BlockSpec((1, d), lambda i: (0, 0)), mod_spec, mod_spec]
    if router_w is None:
        return pl.pallas_call(
            _norm_kernel, grid=(s // r,), in_specs=in_specs, out_specs=row_spec,
            out_shape=jax.ShapeDtypeStruct((s, d), BF16),
            compiler_params=_cparams("parallel"), name="rmsnorm_modulate",
        )(xs, g.reshape(1, d), scale, shift)
    ne = router_w.shape[1]
    rw = jnp.zeros((d, LANES), F32).at[:, :ne].set(router_w)
    return pl.pallas_call(
        _norm_router_kernel, grid=(s // r,),
        in_specs=in_specs + [pl.BlockSpec((d, LANES), lambda i: (0, 0))],
        out_specs=[pl.BlockSpec((r, d // 2), lambda i: (i, 0)), pl.BlockSpec((r, LANES), lambda i: (i, 0))],
        out_shape=[jax.ShapeDtypeStruct((s, d // 2), jnp.uint32), jax.ShapeDtypeStruct((s, LANES), F32)],
        compiler_params=_cparams("parallel"), name="rmsnorm_modulate_router",
    )(xs, g.reshape(1, d), scale, shift, rw)


def _dot_nt(a, wt):
    return lax.dot_general(a, wt, (((1,), (1,)), ((), ())), preferred_element_type=F32)


def _inproj_kernel(a_ref, wt_ref, ws_ref, o_ref, os_ref):
    a = a_ref[...]
    o_ref[...] = _dot_nt(a, wt_ref[...]).astype(o_ref.dtype)

    @pl.when(pl.program_id(1) == 0)
    def _():
        os_ref[...] = _dot_nt(a, ws_ref[...])


def _mm_nt_kernel(a_ref, wt_ref, o_ref):
    o_ref[...] = _dot_nt(a_ref[...], wt_ref[...]).astype(o_ref.dtype)


def _matmul_nt(a, wt):
    m, k = a.shape
    n = wt.shape[0]
    tm = _pick(m, MM_TM, 8)
    tn = _pick(n, MM_TN, LANES)
    return pl.pallas_call(
        _mm_nt_kernel, grid=(m // tm, n // tn),
        in_specs=[pl.BlockSpec((tm, k), lambda i, j: (i, 0)), pl.BlockSpec((tn, k), lambda i, j: (j, 0))],
        out_specs=pl.BlockSpec((tm, tn), lambda i, j: (i, j)),
        out_shape=jax.ShapeDtypeStruct((m, n), BF16),
        compiler_params=_cparams("parallel", "parallel"), name="matmul_nt",
    )(a, wt)


def _inproj(a, wt, ws):
    m, k = a.shape
    n, ns = wt.shape[0], ws.shape[0]
    tm = _pick(m, MM_TM, 8)
    tn = _pick(n, MM_TN, LANES)
    return pl.pallas_call(
        _inproj_kernel, grid=(m // tm, n // tn),
        in_specs=[pl.BlockSpec((tm, k), lambda i, j: (i, 0)), pl.BlockSpec((tn, k), lambda i, j: (j, 0)),
                  pl.BlockSpec((ns, k), lambda i, j: (0, 0))],
        out_specs=[pl.BlockSpec((tm, tn), lambda i, j: (i, j)), pl.BlockSpec((tm, ns), lambda i, j: (i, 0))],
        out_shape=[jax.ShapeDtypeStruct((m, n), BF16), jax.ShapeDtypeStruct((m, ns), F32)],
        compiler_params=_cparams("parallel", "arbitrary"), name="in_projection",
    )(a, wt, ws)


def _row_gate(gate_ref, row0, tm, n_ctx):
    rows = row0 + lax.broadcasted_iota(jnp.int32, (tm, 1), 0)
    return jnp.where(rows < n_ctx, gate_ref[0], gate_ref[1])


def _mm_res_kernel(a_ref, w_ref, res_ref, gate_ref, o_ref, acc_ref, *, nk, tm, n_ctx, k_tail):
    k = pl.program_id(2)

    @pl.when(k == 0)
    def _():
        acc_ref[...] = jnp.zeros_like(acc_ref)

    if k_tail:
        @pl.when(k < nk - 1)
        def _():
            acc_ref[...] += jnp.dot(a_ref[...], w_ref[...], preferred_element_type=F32)

        @pl.when(k == nk - 1)
        def _():
            w = w_ref[...]
            w = jnp.where(lax.broadcasted_iota(jnp.int32, w.shape, 0) < k_tail, w, jnp.zeros_like(w))
            acc_ref[...] += jnp.dot(a_ref[...], w, preferred_element_type=F32)
    else:
        acc_ref[...] += jnp.dot(a_ref[...], w_ref[...], preferred_element_type=F32)

    @pl.when(k == nk - 1)
    def _():
        gate = _row_gate(gate_ref, pl.program_id(0) * tm, tm, n_ctx)
        o_ref[...] = res_ref[...] + gate * acc_ref[...]


def _matmul_residual(a, w, res, gate, n_ctx, tk_pref):
    m, kdim = a.shape
    n = w.shape[1]
    tm = _pick(m, MM_TM, 8)
    tn = _pick(n, MM_TN, LANES)
    tk = _pick(kdim, tk_pref, LANES)
    if w.shape[0] <= (kdim // tk - 1) * tk:
        tk = kdim
    nk = kdim // tk
    k_tail = w.shape[0] - (nk - 1) * tk
    assert 0 < k_tail <= tk
    return pl.pallas_call(
        functools.partial(_mm_res_kernel, nk=nk, tm=tm, n_ctx=n_ctx, k_tail=0 if k_tail == tk else k_tail),
        grid=(m // tm, n // tn, nk),
        in_specs=[pl.BlockSpec((tm, tk), lambda i, j, k: (i, k)),
                  pl.BlockSpec((tk, tn), lambda i, j, k: (k, j)),
                  pl.BlockSpec((tm, tn), lambda i, j, k: (i, j)),
                  pl.BlockSpec((2, 1, tn), lambda i, j, k: (0, 0, j))],
        out_specs=pl.BlockSpec((tm, tn), lambda i, j, k: (i, j)),
        out_shape=jax.ShapeDtypeStruct((m, n), F32),
        scratch_shapes=[pltpu.VMEM((tm, tn), F32)],
        compiler_params=_cparams("parallel", "parallel", "arbitrary"), name="matmul_residual",
    )(a, w, res, gate)


def _mm_res_ws_kernel(a_ref, w_ref, res_ref, gate_ref, o_ref, wbuf, *, tm, n_ctx):
    i = pl.program_id(1)

    @pl.when(i == 0)
    def _():
        wbuf[...] = w_ref[...].astype(BF16)

    acc = jnp.dot(a_ref[...], wbuf[...], preferred_element_type=F32)
    o_ref[...] = res_ref[...] + _row_gate(gate_ref, i * tm, tm, n_ctx) * acc


def _matmul_residual_stationary(a, w, layer, res, gate, n_ctx):
    m, k = a.shape
    n = w.shape[2]
    tm = _pick(m, MM_TM, 8)
    tn = _pick(n, GLU_TN, LANES)
    return pl.pallas_call(
        functools.partial(_mm_res_ws_kernel, tm=tm, n_ctx=n_ctx),
        grid=(n // tn, m // tm),
        in_specs=[pl.BlockSpec((tm, k), lambda j, i: (i, 0)),
                  pl.BlockSpec((None, k, tn), lambda j, i: (layer, 0, j)),
                  pl.BlockSpec((tm, tn), lambda j, i: (i, j)),
                  pl.BlockSpec((2, 1, tn), lambda j, i: (0, 0, j))],
        out_specs=pl.BlockSpec((tm, tn), lambda j, i: (i, j)),
        out_shape=jax.ShapeDtypeStruct((m, n), F32),
        scratch_shapes=[pltpu.VMEM((k, tn), BF16)],
        compiler_params=_cparams("arbitrary", "arbitrary"), name="matmul_residual_ws",
    )(a, w, res, gate)


def _stationary_plan(tile_expert, n_used, n_experts, nj):
    n_tiles = tile_expert.shape[0]
    tile_ids = jnp.arange(n_tiles, dtype=jnp.int32)
    cnt = jnp.sum(((tile_expert[:, None] == jnp.arange(n_experts)[None, :])
                   & (tile_ids[:, None] < n_used)).astype(jnp.int32), axis=0)
    seg_end = jnp.cumsum(cnt)
    seg_start = seg_end - cnt
    step = jnp.arange(n_tiles * nj, dtype=jnp.int32)
    used_steps = n_used * nj
    e = jnp.minimum(jnp.searchsorted(seg_end * nj, step, side='right'), n_experts - 1).astype(jnp.int32)
    seg_len = jnp.maximum(cnt[e], 1)
    local = jnp.maximum(step - nj * seg_start[e], 0)
    j = local // seg_len
    i = seg_start[e] + local % seg_len
    used = step < used_steps
    last = used_steps - 1
    fill = step - used_steps
    pick = lambda arr: jnp.where(used, arr, arr[last]).astype(jnp.int32)
    out_i = jnp.where(used, i, n_used + fill // nj).astype(jnp.int32)
    out_j = jnp.where(used, j, fill % nj).astype(jnp.int32)
    return pick(e), pick(j), pick(i), out_i, out_j, used.astype(jnp.int32)


def _stationary_flags(e_ref, jw_ref, used_ref):
    s = pl.program_id(0)
    prev = jnp.maximum(s - 1, 0)
    fresh = jnp.logical_or(s == 0, jnp.logical_or(e_ref[s] != e_ref[prev], jw_ref[s] != jw_ref[prev]))
    used = used_ref[s] > 0
    return used, jnp.logical_and(used, fresh)


def _glu_kernel(e_ref, jw_ref, ia_ref, io_ref, jo_ref, used_ref, a_ref, w1_ref, w3_ref, o_ref, w1b, w3b,
                *, n_valid, ragged):
    used, fresh = _stationary_flags(e_ref, jw_ref, used_ref)

    @pl.when(fresh)
    def _():
        w1 = w1_ref[...]
        w3 = w3_ref[...]
        if ragged:
            col = jw_ref[pl.program_id(0)] * w1.shape[1] + lax.broadcasted_iota(jnp.int32, w1.shape, 1)
            w1 = jnp.where(col < n_valid, w1, 0.0)
            w3 = jnp.where(col < n_valid, w3, 0.0)
        w1b[...] = w1.astype(BF16)
        w3b[...] = w3.astype(BF16)

    @pl.when(used)
    def _():
        a = a_ref[...]
        h1 = jnp.dot(a, w1b[...], preferred_element_type=F32)
        h3 = jnp.dot(a, w3b[...], preferred_element_type=F32)
        o_ref[...] = (_silu(h1) * h3).astype(o_ref.dtype)

    @pl.when(jnp.logical_not(used))
    def _():
        o_ref[...] = jnp.zeros_like(o_ref)


def _stationary_specs(tm, k, tn):
    a_spec = pl.BlockSpec((tm, k), lambda s, e, jw, ia, io, jo, u: (ia[s], 0))
    w_spec = pl.BlockSpec((None, k, tn), lambda s, e, jw, ia, io, jo, u: (e[s], 0, jw[s]))
    o_spec = pl.BlockSpec((tm, tn), lambda s, e, jw, ia, io, jo, u: (io[s], jo[s]))
    return a_spec, w_spec, o_spec


def _glu(a, w1, w3, tm, tile_expert, n_used):
    m, k = a.shape
    ne, _, n = w1.shape
    tn = GLU_TN if n % LANES == 0 and n > GLU_TN else n
    nj = pl.cdiv(n, tn)
    plan = _stationary_plan(tile_expert, n_used, ne, nj)
    a_spec, w_spec, o_spec = _stationary_specs(tm, k, tn)
    return pl.pallas_call(
        functools.partial(_glu_kernel, n_valid=n, ragged=n % tn != 0),
        grid_spec=pltpu.PrefetchScalarGridSpec(
            num_scalar_prefetch=6, grid=(plan[0].shape[0],),
            in_specs=[a_spec, w_spec, w_spec], out_specs=o_spec,
            scratch_shapes=[pltpu.VMEM((k, tn), BF16)] * 2),
        out_shape=jax.ShapeDtypeStruct((m, nj * tn), BF16),
        compiler_params=_cparams("arbitrary", vmem=V7X_VMEM_LIMIT_BIG_BYTES), name="swiglu_up",
    )(*plan, a, w1, w3)


def _mm_expert_kernel(e_ref, jw_ref, ia_ref, io_ref, jo_ref, used_ref, a_ref, w_ref, o_ref, wb):
    used, fresh = _stationary_flags(e_ref, jw_ref, used_ref)

    @pl.when(fresh)
    def _():
        wb[...] = w_ref[...].astype(BF16)

    @pl.when(used)
    def _():
        o_ref[...] = jnp.dot(a_ref[...], wb[...], preferred_element_type=F32).astype(o_ref.dtype)

    @pl.when(jnp.logical_not(used))
    def _():
        o_ref[...] = jnp.zeros_like(o_ref)


def _matmul_expert(a, w, tm, tile_expert, n_used):
    m, k = a.shape
    ne, _, n = w.shape
    tn = _pick(n, DOWN_TN, LANES)
    nj = n // tn
    plan = _stationary_plan(tile_expert, n_used, ne, nj)
    a_spec, w_spec, o_spec = _stationary_specs(tm, k, tn)
    return pl.pallas_call(
        _mm_expert_kernel,
        grid_spec=pltpu.PrefetchScalarGridSpec(
            num_scalar_prefetch=6, grid=(plan[0].shape[0],),
            in_specs=[a_spec, w_spec], out_specs=o_spec,
            scratch_shapes=[pltpu.VMEM((k, tn), BF16)]),
        out_shape=jax.ShapeDtypeStruct((m, n), F32),
        compiler_params=_cparams("arbitrary", vmem=V7X_VMEM_LIMIT_BIG_BYTES), name="expert_down",
    )(*plan, a, w)


def _chunk_maps(nct, nlt):
    def fwd(s):
        return s

    def bwd(s):
        return jnp.where(s < nct, nct - 1 - s, 2 * nct + nlt - 1 - s)

    return fwd, bwd


def _tri(l, lower):
    r = lax.broadcasted_iota(jnp.int32, (l, l), 0)
    c = lax.broadcasted_iota(jnp.int32, (l, l), 1)
    return (r >= c) if lower else (r <= c)


def _mlstm_direction(q, k, v, g, gt, c_ref, n_ref, m_ref, o_ref, *, fwd, scale):
    l = q.shape[0]
    ci, cf = (0, 2) if fwd else (1, 3)
    low = _tri(l, True).astype(F32)
    up = _tri(l, False).astype(F32)
    b_col = _dot_exact(low if fwd else up, _log_sigmoid(g), True)[:, cf:cf + 1]
    b_row = _dot_exact(up if fwd else low, _log_sigmoid(gt), False)[cf:cf + 1, :]
    i_col = g[:, ci:ci + 1]
    i_row = gt[ci:ci + 1, :]
    causal = _tri(l, fwd)
    logw = jnp.where(causal, b_col - b_row + i_row, -jnp.inf)
    m_intra = jnp.max(logw, axis=-1, keepdims=True)
    m_st = m_ref[...]
    inter_log = b_col + m_st
    m_t = jnp.maximum(inter_log, m_intra)
    a_inter = jnp.exp(inter_log - m_t)
    ks = (k.astype(F32) * scale)
    ksb = ks.astype(BF16)
    qk = lax.dot_general(q, ksb, (((1,), (1,)), ((), ())), preferred_element_type=F32)
    s = qk * jnp.exp(logw - m_t)
    c_st = c_ref[...]
    n_st = n_ref[...]
    num = a_inter * jnp.dot(q, c_st.astype(BF16), preferred_element_type=F32) \
        + jnp.dot(s.astype(BF16), v, preferred_element_type=F32)
    den = a_inter * jnp.sum(q.astype(F32) * n_st, axis=-1, keepdims=True) + jnp.sum(s, axis=-1, keepdims=True)
    o_ref[...] = (num / jnp.maximum(jnp.abs(den), jnp.exp(-m_t))).astype(o_ref.dtype)
    b_last = b_col[l - 1:l, :] if fwd else b_col[0:1, :]
    g_col = b_last - b_col + i_col
    m_new = jnp.maximum(b_last + m_st, jnp.max(g_col, axis=0, keepdims=True))
    kw = ks * jnp.exp(g_col - m_new)
    dec = jnp.exp(b_last + m_st - m_new)
    c_ref[...] = dec * c_st + lax.dot_general(kw.astype(BF16), v, (((0,), (0,)), ((), ())),
                                              preferred_element_type=F32)
    n_ref[...] = dec * n_st + jnp.sum(kw, axis=0, keepdims=True)
    m_ref[...] = m_new


def _mlstm_kernel(qf, kf, vf, gf, gtf, qb, kb, vb, gb, gtb, bias, bias_t, hf, hb,
                  cf, nf, mf, cb, nb, mb, *, scale, dqk, dv):
    @pl.when(pl.program_id(1) == 0)
    def _():
        for ref in (cf, nf, mf, cb, nb, mb):
            ref[...] = jnp.zeros_like(ref)

    for h in range(cf.shape[0]):
        qs, vs = slice(h * dqk, (h + 1) * dqk), slice(h * dv, (h + 1) * dv)
        _mlstm_direction(qf[:, qs], kf[:, qs], vf[:, vs], gf[h] + bias[h], gtf[h] + bias_t[h],
                         cf.at[h], nf.at[h], mf.at[h], hf.at[:, pl.ds(h * dv, dv)], fwd=True, scale=scale)
        _mlstm_direction(qb[:, qs], kb[:, qs], vb[:, vs], gb[h] + bias[h], gtb[h] + bias_t[h],
                         cb.at[h], nb.at[h], mb.at[h], hb.at[:, pl.ds(h * dv, dv)], fwd=False, scale=scale)


def _mlstm(big, gates, gate_b, nh, dqk, dv, n_ctx, col_q, col_k, col_v):
    s = big.shape[0]
    l = _pick(math.gcd(n_ctx, s - n_ctx), SCAN_CHUNK, 8)
    nct, nlt = n_ctx // l, (s - n_ctx) // l
    fwd, bwd = _chunk_maps(nct, nlt)
    gh = gates.reshape(s, 4, nh).transpose(2, 0, 1)
    g_cols = jnp.zeros((nh, s, LANES), F32).at[:, :, :4].set(gh)
    g_rows = jnp.zeros((nh, 8, s), F32).at[:, :4, :].set(gh.transpose(0, 2, 1))
    bh = gate_b.reshape(4, nh).T
    b_cols = jnp.zeros((nh, 1, LANES), F32).at[:, 0, :4].set(bh)
    b_rows = jnp.zeros((nh, 8, 1), F32).at[:, :4, 0].set(bh)
    hp = MLSTM_HEADS_PER_STEP if nh % MLSTM_HEADS_PER_STEP == 0 else 1
    qw, vw = hp * dqk, hp * dv
    assert col_q % qw == 0 and col_k % qw == 0 and col_v % vw == 0
    qo, ko, vo = col_q // qw, col_k // qw, col_v // vw

    def specs(cm):
        return [pl.BlockSpec((l, qw), lambda h, t: (cm(t), qo + h)),
                pl.BlockSpec((l, qw), lambda h, t: (cm(t), ko + h)),
                pl.BlockSpec((l, vw), lambda h, t: (cm(t), vo + h)),
                pl.BlockSpec((hp, l, LANES), lambda h, t: (h, cm(t), 0)),
                pl.BlockSpec((hp, 8, l), lambda h, t: (h, 0, cm(t)))]

    out_f = pl.BlockSpec((l, vw), lambda h, t: (fwd(t), h))
    out_b = pl.BlockSpec((l, vw), lambda h, t: (bwd(t), h))
    state = [pltpu.VMEM((hp, dqk, dv), F32), pltpu.VMEM((hp, 1, dqk), F32), pltpu.VMEM((hp, 1, 1), F32)]
    return pl.pallas_call(
        functools.partial(_mlstm_kernel, scale=dqk ** -0.5, dqk=dqk, dv=dv),
        grid=(nh // hp, nct + nlt),
        in_specs=specs(fwd) + specs(bwd) + [pl.BlockSpec((hp, 1, LANES), lambda h, t: (h, 0, 0)),
                                             pl.BlockSpec((hp, 8, 1), lambda h, t: (h, 0, 0))],
        out_specs=[out_f, out_b],
        out_shape=[jax.ShapeDtypeStruct((s, nh * dv), BF16)] * 2,
        scratch_shapes=state + state,
        compiler_params=_cparams("parallel", "arbitrary"), name="mlstm_scan",
    )(big, big, big, g_cols, g_rows, big, big, big, g_cols, g_rows, b_cols, b_rows)


def _conv_kernel(u_ref, prev_ref, next_ref, w_ref, b_ref, o_ref, *, nct, nt, width, halo):
    i = pl.program_id(0)
    first = jnp.logical_or(i == 0, i == nct)
    last = jnp.logical_or(i == nct - 1, i == nt - 1)
    r = u_ref.shape[0]
    prev = jnp.where(first, 0.0, prev_ref[...].astype(F32))
    nxt = jnp.where(last, 0.0, next_ref[...].astype(F32))
    ext = jnp.concatenate([prev, u_ref[...].astype(F32), nxt], axis=0)
    w = w_ref[...]
    acc = b_ref[...] + w[0:1, :] * ext[halo - width // 2: halo - width // 2 + r]
    for j in range(1, width):
        off = halo - width // 2 + j
        acc = acc + w[j:j + 1, :] * ext[off: off + r]
    o_ref[...] = _silu(acc).astype(o_ref.dtype)


def _dwconv_silu(u, conv_w, conv_b, n_ctx):
    s, ch = u.shape
    width = conv_w.shape[0]
    halo = 16
    r = _pick(math.gcd(n_ctx, s - n_ctx), ROW_TILE, halo)
    tc = _pick(ch, 1024, LANES)
    nt, nct, rb = s // r, n_ctx // r, r // halo
    w8 = jnp.zeros((8, ch), F32).at[:width].set(conv_w)
    return pl.pallas_call(
        functools.partial(_conv_kernel, nct=nct, nt=nt, width=width, halo=halo),
        grid=(nt, ch // tc),
        in_specs=[pl.BlockSpec((r, tc), lambda i, j: (i, j)),
                  pl.BlockSpec((halo, tc), lambda i, j: (jnp.maximum(i * rb - 1, 0), j)),
                  pl.BlockSpec((halo, tc), lambda i, j: (jnp.minimum((i + 1) * rb, s // halo - 1), j)),
                  pl.BlockSpec((8, tc), lambda i, j: (0, j)),
                  pl.BlockSpec((1, tc), lambda i, j: (0, j))],
        out_specs=pl.BlockSpec((r, tc), lambda i, j: (i, j)),
        out_shape=jax.ShapeDtypeStruct((s, ch), BF16),
        compiler_params=_cparams("parallel", "parallel"), name="dwconv_silu",
    )(u, u, u, w8, conv_b.reshape(1, ch))


def _per_channel(cols, off, hg, p, lane):
    out = jnp.broadcast_to(cols[:, off + hg - 1:off + hg], lane.shape)
    for h in range(hg - 2, -1, -1):
        out = jnp.where(lane < (h + 1) * p, jnp.broadcast_to(cols[:, off + h:off + h + 1], lane.shape), out)
    return out


def _ssd_direction(x, bm, cm, dtv, da, da_t, st_ref, *, fwd, hg, p):
    l = x.shape[0]
    off = 0 if fwd else hg
    low = _tri(l, True)
    up = _tri(l, False)
    acum_col = _dot_exact(low if fwd else up, da, True)
    acum_row = _dot_exact(up if fwd else low, da_t, False)
    lane = lax.broadcasted_iota(jnp.int32, x.shape, 1)
    acx = _per_channel(acum_col, off, hg, p, lane)
    xdt = x.astype(F32) * _per_channel(dtv, off, hg, p, lane)
    xdt_b = xdt.astype(BF16)
    causal = _tri(l, fwd)
    cb = lax.dot_general(cm, bm, (((1,), (1,)), ((), ())), preferred_element_type=F32)
    y = None
    for h in range(hg):
        seg = acum_col[:, off + h:off + h + 1] - acum_row[off + h:off + h + 1, :]
        mix = (cb * jnp.exp(jnp.where(causal, seg, -jnp.inf))).astype(BF16)
        own = jnp.logical_and(lane >= h * p, lane < (h + 1) * p)
        part = jnp.dot(mix, jnp.where(own, xdt_b, jnp.zeros_like(xdt_b)), preferred_element_type=F32)
        y = part if y is None else y + part
    st = st_ref[...]
    y = y + jnp.dot(cm, st.astype(BF16), preferred_element_type=F32) * jnp.exp(acx)
    total = acx[l - 1:l, :] if fwd else acx[0:1, :]
    xw = (xdt * jnp.exp(total - acx)).astype(BF16)
    st_ref[...] = jnp.exp(total) * st + lax.dot_general(bm, xw, (((0,), (0,)), ((), ())),
                                                        preferred_element_type=F32)
    return y


def _ssd_kernel(xf, bf, cf, dtf, dttf, xb, bb, cb, dtb, dttb, a_c, a_r, db_c, db_r, dsk,
                yf, yb, stf, stb, *, hg, p, n):
    @pl.when(pl.program_id(1) == 0)
    def _():
        stf[...] = jnp.zeros_like(stf)
        stb[...] = jnp.zeros_like(stb)

    gw = hg * p
    for g in range(stf.shape[0]):
        def steps(dt_ref, dtt_ref):
            dtv = _softplus(dt_ref[g] + db_c[g])
            da_t = _softplus(dtt_ref[g] + db_r[g]) * a_r[g]
            return dtv, dtv * a_c[g], da_t

        xs, ns = slice(g * gw, (g + 1) * gw), slice(g * n, (g + 1) * n)
        x = xf[:, xs]
        y = _ssd_direction(x, bf[:, ns], cf[:, ns], *steps(dtf, dttf), stf.at[g], fwd=True, hg=hg, p=p)
        yf[:, xs] = (y + dsk[g] * x.astype(F32)).astype(yf.dtype)
        y = _ssd_direction(xb[:, xs], bb[:, ns], cb[:, ns], *steps(dtb, dttb), stb.at[g], fwd=False, hg=hg, p=p)
        yb[:, xs] = y.astype(yb.dtype)


def _ssd(xbc, dt_raw, a_log, dt_bias, d_skip, n_ctx, sw, n_state):
    s = xbc.shape[0]
    h = a_log.shape[1]
    g = SSM_GROUPS
    hg, p = h // g, sw // h
    gw = hg * p
    l = _pick(math.gcd(n_ctx, s - n_ctx), SCAN_CHUNK, 8)
    nct, nlt = n_ctx // l, (s - n_ctx) // l
    fwd, bwd = _chunk_maps(nct, nlt)
    dtg = dt_raw.reshape(s, 2, g, hg).transpose(2, 0, 1, 3).reshape(g, s, 2 * hg)
    dt_cols = jnp.zeros((g, s, LANES), F32).at[:, :, :2 * hg].set(dtg)
    assert 2 * hg <= 8 and sw % n_state == 0
    dt_rows = jnp.zeros((g, 8, s), F32).at[:, :2 * hg, :].set(dtg.transpose(0, 2, 1))
    a = (-jnp.exp(a_log.astype(F32))).reshape(2, g, hg).transpose(1, 0, 2).reshape(g, 2 * hg)
    db = dt_bias.reshape(2, g, hg).transpose(1, 0, 2).reshape(g, 2 * hg)
    a_c = jnp.zeros((g, 1, LANES), F32).at[:, 0, :2 * hg].set(a)
    a_r = jnp.zeros((g, 8, 1), F32).at[:, :2 * hg, 0].set(a)
    db_c = jnp.zeros((g, 1, LANES), F32).at[:, 0, :2 * hg].set(db)
    db_r = jnp.zeros((g, 8, 1), F32).at[:, :2 * hg, 0].set(db)
    dsk = jnp.repeat(d_skip, p).reshape(g, 1, gw)
    gp = SSD_GROUPS_PER_STEP if g % SSD_GROUPS_PER_STEP == 0 else 1
    xw, nw = gp * gw, gp * n_state
    assert sw % nw == 0 and (g * n_state) % nw == 0
    bo, co = sw // nw, (sw + g * n_state) // nw

    def specs(cm):
        return [pl.BlockSpec((l, xw), lambda gi, t: (cm(t), gi)),
                pl.BlockSpec((l, nw), lambda gi, t: (cm(t), bo + gi)),
                pl.BlockSpec((l, nw), lambda gi, t: (cm(t), co + gi)),
                pl.BlockSpec((gp, l, LANES), lambda gi, t: (gi, cm(t), 0)),
                pl.BlockSpec((gp, 8, l), lambda gi, t: (gi, 0, cm(t)))]

    par_c = pl.BlockSpec((gp, 1, LANES), lambda gi, t: (gi, 0, 0))
    par_r = pl.BlockSpec((gp, 8, 1), lambda gi, t: (gi, 0, 0))
    return pl.pallas_call(
        functools.partial(_ssd_kernel, hg=hg, p=p, n=n_state),
        grid=(g // gp, nct + nlt),
        in_specs=specs(fwd) + specs(bwd) + [par_c, par_r, par_c, par_r,
                                             pl.BlockSpec((gp, 1, gw), lambda gi, t: (gi, 0, 0))],
        out_specs=[pl.BlockSpec((l, xw), lambda gi, t: (fwd(t), gi)),
                   pl.BlockSpec((l, xw), lambda gi, t: (bwd(t), gi))],
        out_shape=[jax.ShapeDtypeStruct((s, sw), BF16)] * 2,
        scratch_shapes=[pltpu.VMEM((gp, n_state, gw), F32)] * 2,
        compiler_params=_cparams("parallel", "arbitrary"), name="ssd_scan",
    )(xbc, xbc, xbc, dt_cols, dt_rows, xbc, xbc, xbc, dt_cols, dt_rows, a_c, a_r, db_c, db_r, dsk)


def _finish_kernel(hf, hb, o, yf, yb, z, mg, sg, out, *, nh, dv):
    mw = nh * dv
    h = hf[...].astype(F32) + hb[...].astype(F32)
    gate = _sigmoid(o[...].astype(F32))
    mgv = mg[...]
    for k in range(nh):
        hk = h[:, k * dv:(k + 1) * dv]
        r = lax.rsqrt(jnp.mean(hk * hk, axis=-1, keepdims=True) + EPS)
        out[:, k * dv:(k + 1) * dv] = (hk * r * mgv[:, k * dv:(k + 1) * dv]
                                       * gate[:, k * dv:(k + 1) * dv]).astype(out.dtype)
    y = (yf[...].astype(F32) + yb[...].astype(F32)) * _silu(z[...].astype(F32))
    r = lax.rsqrt(jnp.mean(y * y, axis=-1, keepdims=True) + EPS)
    out[:, mw:] = (y * r * sg[...]).astype(out.dtype)


def _finish(hf, hb, big, yf, yb, mlstm_g, ssm_g, nh, dv, col_o, col_z):
    s, mw = hf.shape
    sw = yf.shape[1]
    r = _pick(s, ROW_TILE, 8)
    return pl.pallas_call(
        functools.partial(_finish_kernel, nh=nh, dv=dv),
        grid=(s // r,),
        in_specs=[pl.BlockSpec((r, mw), lambda i: (i, 0)), pl.BlockSpec((r, mw), lambda i: (i, 0)),
                  pl.BlockSpec((r, mw), lambda i: (i, col_o // mw)),
                  pl.BlockSpec((r, sw), lambda i: (i, 0)), pl.BlockSpec((r, sw), lambda i: (i, 0)),
                  pl.BlockSpec((r, sw), lambda i: (i, col_z // sw)),
                  pl.BlockSpec((1, mw), lambda i: (0, 0)), pl.BlockSpec((1, sw), lambda i: (0, 0))],
        out_specs=pl.BlockSpec((r, mw + sw), lambda i: (i, 0)),
        out_shape=jax.ShapeDtypeStruct((s, mw + sw), BF16),
        compiler_params=_cparams("parallel"), name="mixer_finish",
    )(hf, hb, big, yf, yb, big, mlstm_g.reshape(1, mw), ssm_g.reshape(1, sw))


def _row_copy(src_ref, row, dst_ref, k, sem):
    return pltpu.make_async_copy(src_ref.at[pl.ds(row, 1)], dst_ref.at[pl.ds(k, 1)], sem)


def _gather_kernel(idx_ref, nxt_ref, src_ref, o_ref, buf, sem):
    i = pl.program_id(0)
    n_rows = o_ref.shape[0]
    slot = lax.rem(i, 2)

    groups = n_rows // 8
    spread = groups & (groups - 1) == 0

    def issue(ids_ref, s):
        def body(k, carry):
            kk = (k & (groups - 1)) * 8 + lax.shift_right_logical(k, groups.bit_length() - 1) if spread else k
            row = ids_ref[0, 0, kk]

            @pl.when(row >= 0)
            def _():
                _row_copy(src_ref, row, buf.at[s], kk, sem.at[s]).start()
            return carry
        lax.fori_loop(0, n_rows, body, 0)

    @pl.when(i == 0)
    def _():
        buf[...] = jnp.zeros_like(buf)
        issue(idx_ref, 0)

    @pl.when(i + 1 < pl.num_programs(0))
    def _():
        issue(nxt_ref, 1 - slot)

    def wait(k, carry):
        @pl.when(idx_ref[0, 0, k] >= 0)
        def _():
            _row_copy(src_ref, 0, buf.at[slot], k, sem.at[slot]).wait()
        return carry

    lax.fori_loop(0, n_rows, wait, 0)
    hi, lo = _unpack_bf16_pairs(buf[slot])
    half = o_ref.shape[1] // 2
    o_ref[:, :half] = hi
    o_ref[:, half:] = lo


def _gather_rows(src, idx):
    n = idx.shape[0]
    d = 2 * src.shape[1]
    r = _pick(n, GATHER_ROWS, 8)
    nt = n // r
    ids = idx.reshape(nt, 1, r)
    return pl.pallas_call(
        _gather_kernel, grid=(nt,),
        in_specs=[pl.BlockSpec((1, 1, r), lambda i: (i, 0, 0), memory_space=pltpu.SMEM),
                  pl.BlockSpec((1, 1, r), lambda i: (jnp.minimum(i + 1, nt - 1), 0, 0), memory_space=pltpu.SMEM),
                  pl.BlockSpec(memory_space=pl.ANY)],
        out_specs=pl.BlockSpec((r, d), lambda i: (i, 0)),
        out_shape=jax.ShapeDtypeStruct((n, d), BF16),
        scratch_shapes=[pltpu.VMEM((2, r, d // 2), src.dtype), pltpu.SemaphoreType.DMA((2,))],
        compiler_params=_cparams("arbitrary"), name="moe_gather",
    )(ids, ids, src)


def _combine_kernel(p0_ref, p1_ref, q0_ref, q1_ref, yr_ref, x_ref, gk_ref, gate_ref, fg_ref, o_ref, buf, sem,
                    *, final_norm):
    i = pl.program_id(0)
    n_rows = x_ref.shape[0]
    slot = lax.rem(i, 2)

    def issue(a_ref, b_ref, s):
        def body(k, carry):
            _row_copy(yr_ref, a_ref[0, 0, k], buf.at[s, 0], k, sem.at[s]).start()
            _row_copy(yr_ref, b_ref[0, 0, k], buf.at[s, 1], k, sem.at[s]).start()
            return carry
        lax.fori_loop(0, n_rows, body, 0)

    @pl.when(i == 0)
    def _():
        issue(p0_ref, p1_ref, 0)

    @pl.when(i + 1 < pl.num_programs(0))
    def _():
        issue(q0_ref, q1_ref, 1 - slot)

    def wait(k, carry):
        _row_copy(yr_ref, 0, buf.at[slot, 0], k, sem.at[slot]).wait()
        _row_copy(yr_ref, 0, buf.at[slot, 1], k, sem.at[slot]).wait()
        return carry

    lax.fori_loop(0, n_rows, wait, 0)
    gk = gk_ref[...]
    x = x_ref[...] + gate_ref[...] * (gk[:, 0:1] * buf[slot, 0] + gk[:, 1:2] * buf[slot, 1])
    if final_norm:
        x = x * lax.rsqrt(jnp.mean(x * x, axis=-1, keepdims=True) + EPS) * fg_ref[...]
    o_ref[...] = x


def _moe_combine(xs, yr, pos, gates, gate_mod, n_ctx, final_g):
    s, d = xs.shape
    t = s - n_ctx
    r = _pick(math.gcd(n_ctx, t), COMBINE_ROWS, 8)
    nct, nt = n_ctx // r, t // r
    gk = jnp.zeros((t, LANES), F32).at[:, :TOP_K].set(gates)
    p0 = pos[:, 0].reshape(nt, 1, r)
    p1 = pos[:, 1].reshape(nt, 1, r)
    cur = pl.BlockSpec((1, 1, r), lambda i: (i, 0, 0), memory_space=pltpu.SMEM)
    nxt = pl.BlockSpec((1, 1, r), lambda i: (jnp.minimum(i + 1, nt - 1), 0, 0), memory_space=pltpu.SMEM)
    fg = jnp.ones((1, d), F32) if final_g is None else final_g.reshape(1, d).astype(F32)
    return pl.pallas_call(
        functools.partial(_combine_kernel, final_norm=final_g is not None), grid=(nt,),
        in_specs=[cur, cur, nxt, nxt,
                  pl.BlockSpec(memory_space=pl.ANY),
                  pl.BlockSpec((r, d), lambda i: (i + nct, 0)),
                  pl.BlockSpec((r, LANES), lambda i: (i, 0)),
                  pl.BlockSpec((1, d), lambda i: (0, 0)),
                  pl.BlockSpec((1, d), lambda i: (0, 0))],
        out_specs=pl.BlockSpec((r, d), lambda i: (i, 0)),
        out_shape=jax.ShapeDtypeStruct((t, d), F32),
        scratch_shapes=[pltpu.VMEM((2, 2, r, d), F32), pltpu.SemaphoreType.DMA((2,))],
        compiler_params=_cparams("arbitrary"), name="moe_combine",
    )(p0, p1, p0, p1, yr, xs, gk, gate_mod, fg)


def _moe_plan(logits, ne, tm):
    t = logits.shape[0]
    top_val, top_idx = lax.top_k(logits[:, :ne], TOP_K)
    gates = jax.nn.softmax(top_val, axis=-1)
    n_assign = t * TOP_K
    e_flat = top_idx.reshape(-1)
    onehot = (e_flat[:, None] == jnp.arange(ne)[None, :]).astype(jnp.int32)
    rank = jnp.take_along_axis(jnp.cumsum(onehot, axis=0) - onehot, e_flat[:, None], axis=1)[:, 0]
    counts = jnp.sum(onehot, axis=0)
    padded = (counts + tm - 1) // tm * tm
    pend = jnp.cumsum(padded)
    pstart = pend - padded
    pos = (pstart[e_flat] + rank).astype(jnp.int32)
    n_tiles = (n_assign + ne * (tm - 1)) // tm
    n_rows = n_tiles * tm
    tok = jnp.repeat(jnp.arange(t, dtype=jnp.int32), TOP_K)
    row_tok = jnp.full((n_rows,), -1, jnp.int32).at[pos].set(tok)
    tile_start = jnp.arange(n_tiles, dtype=jnp.int32) * tm
    tile_expert = jnp.minimum(jnp.searchsorted(pend, tile_start, side='right'), ne - 1).astype(jnp.int32)
    n_used = (pend[-1] // tm).astype(jnp.int32)
    return gates, pos.reshape(t, TOP_K), row_tok, tile_expert, n_used


def _final_norm_kernel(x_ref, g_ref, o_ref):
    x = x_ref[...]
    o_ref[...] = x * lax.rsqrt(jnp.mean(x * x, axis=-1, keepdims=True) + EPS) * g_ref[...]


def _final_norm(xs, g, n_ctx):
    s, d = xs.shape
    t = s - n_ctx
    r = _pick(math.gcd(n_ctx, t), ROW_TILE, 8)
    nct = n_ctx // r
    return pl.pallas_call(
        _final_norm_kernel, grid=(t // r,),
        in_specs=[pl.BlockSpec((r, d), lambda i: (i + nct, 0)), pl.BlockSpec((1, d), lambda i: (0, 0))],
        out_specs=pl.BlockSpec((r, d), lambda i: (i, 0)),
        out_shape=jax.ShapeDtypeStruct((t, d), F32),
        compiler_params=_cparams("parallel"), name="final_rmsnorm",
    )(xs, g.reshape(1, d))


def _to_colmajor(a, n_ctx):
    t, ch = a.shape[0] - n_ctx, a.shape[1]
    lat = a[n_ctx:].reshape(t // GRID_W, GRID_W, ch).transpose(1, 0, 2).reshape(t, ch)
    return jnp.concatenate([a[:n_ctx], lat], axis=0)


def _from_colmajor(a, n_ctx):
    t, ch = a.shape[0] - n_ctx, a.shape[1]
    lat = a[n_ctx:].reshape(GRID_W, t // GRID_W, ch).transpose(1, 0, 2).reshape(t, ch)
    return jnp.concatenate([a[:n_ctx], lat], axis=0)


def kernel(x, c, ctx, c_ctx, w_mod, b_mod, norm1_g, norm2_g, w_in, gate_b, conv_w, conv_b, a_log, dt_bias, d_skip, mlstm_g, ssm_g, w_out, ffn_w1, ffn_w3, ffn_w2, router_w, moe_w1, moe_w3, moe_w2, final_g):
    bsz, t, d = x.shape
    assert bsz == 1, "the kernels treat the single batch element's tokens as rows"
    n_ctx = ctx.shape[1]
    depth = w_mod.shape[0]
    nh = gate_b.shape[1] // 4
    mw = mlstm_g.shape[1]
    dv = mw // nh
    dqk = dv // 2
    sw = ssm_g.shape[1]
    heads = a_log.shape[2]
    conv_ch = conv_w.shape[2]
    n_state = (conv_ch - sw) // (2 * SSM_GROUPS)
    wq = nh * dqk
    splits = (wq, wq, mw, mw, 4 * nh, sw, conv_ch, 2 * heads)
    offs = [0]
    for w_ in splits:
        offs.append(offs[-1] + w_)
    assert offs[-1] == w_in.shape[2]
    ne = router_w.shape[2]

    xs = jnp.concatenate([ctx[0], x[0]], axis=0).astype(F32)
    mod = _modulation(c_ctx.astype(F32), c[0].astype(F32), w_mod.astype(F32), b_mod.astype(F32))

    col_q, col_k, col_v, col_o, col_z, col_x = 0, wq, 2 * wq, 2 * wq + mw, 2 * wq + 2 * mw, 2 * wq + 2 * mw + sw
    n_small = -(-(4 * nh + 2 * heads) // LANES) * LANES

    for layer in range(depth):
        need_ctx = layer < depth - 1
        m6 = mod[layer, :2].reshape(2, 6, d)
        shift1, scale1, gate1, shift2, scale2, gate2 = (m6[:, i][:, None, :] for i in range(6))

        wt = jnp.swapaxes(w_in[layer], 0, 1)
        w_big = jnp.concatenate([wt[offs[0]:offs[4]], wt[offs[5]:offs[6]]], axis=0).astype(BF16)
        w_xbc = wt[offs[6]:offs[7]].astype(BF16)
        w_small = jnp.concatenate([wt[offs[4]:offs[5]], wt[offs[7]:offs[8]],
                                   jnp.zeros((n_small - 4 * nh - 2 * heads, d), F32)], axis=0).astype(BF16)
        xn = _norm_mod(xs, norm1_g[layer], scale1, shift1, n_ctx)
        xbc_raw, small = _inproj(xn, w_xbc, w_small)
        big = _matmul_nt(xn, w_big)

        hf, hb = _mlstm(big, small[:, :4 * nh], gate_b[layer], nh, dqk, dv, n_ctx, col_q, col_k, col_v)

        xbc = _dwconv_silu(_to_colmajor(xbc_raw, n_ctx), conv_w[layer], conv_b[layer], n_ctx)
        dt_raw = _to_colmajor(small[:, 4 * nh:4 * nh + 2 * heads], n_ctx)
        yf, yb = _ssd(xbc, dt_raw, a_log[layer], dt_bias[layer], d_skip[layer], n_ctx, sw, n_state)
        yf, yb = _from_colmajor(yf, n_ctx), _from_colmajor(yb, n_ctx)

        mix = _finish(hf, hb, big, yf, yb, mlstm_g[layer], ssm_g[layer], nh, dv, col_o, col_z)
        xs = _matmul_residual_stationary(mix, w_out, layer, xs, gate1, n_ctx)

        i = layer // 2
        if layer % 2 == 0:
            hn = _norm_mod(xs, norm2_g[layer], scale2, shift2, n_ctx)
            tm = _pick(hn.shape[0], FFN_TM, 8)
            nt = hn.shape[0] // tm
            act = _glu(hn, ffn_w1[i:i + 1], ffn_w3[i:i + 1], tm, jnp.zeros((nt,), jnp.int32), nt)
            xs = _matmul_residual(act, ffn_w2[i].astype(BF16), xs, gate2, n_ctx, act.shape[1] // 4)
        else:
            assert not need_ctx, "an MoE layer that still feeds context is not needed by this block"
            hn, logits = _norm_mod(xs, norm2_g[layer], scale2, shift2, n_ctx, router_w=router_w[i])
            tm = min(MOE_TM, t)
            gates, pos, row_tok, tile_expert, n_used = _moe_plan(logits[n_ctx:], ne, tm)
            xr = _gather_rows(hn, jnp.where(row_tok >= 0, row_tok + n_ctx, -1))
            act = _glu(xr, moe_w1[i], moe_w3[i], tm, tile_expert, n_used)
            yr = _matmul_expert(act, moe_w2[i], tm, tile_expert, n_used)
            return _moe_combine(xs, yr, pos, gates, gate2[1], n_ctx, final_g)[None]

    return _final_norm(xs, final_g, n_ctx)[None]
```
